```python
import jax, jax.numpy as jnp
from jax import lax
import numpy as np

D_MODEL = 4096
BATCH = 1
SEQ = 8192
DEPTH = 1

PLE_DIM = 256
D_FF = 11008
EPS = 1e-6
MACARON_WEIGHT = 0.5
FOX_HEAD_DIM = 128
FOX_WIDTH = D_MODEL // 2
FOX_HEADS = FOX_WIDTH // FOX_HEAD_DIM
Q_BLOCK = 128
GLA_WIDTH = D_MODEL - FOX_WIDTH
GLA_HEADS = 4
GLA_HEAD_V = GLA_WIDTH // GLA_HEADS
GLA_KEY_WIDTH = GLA_WIDTH // 2
GLA_HEAD_K = GLA_KEY_WIDTH // GLA_HEADS
GLA_GATE_RANK = 16
GLA_TAU = 16.0
GLA_CHUNK = 64
IN_SPLIT_SIZES = (FOX_WIDTH, FOX_WIDTH, FOX_WIDTH, FOX_HEADS,
                  GLA_KEY_WIDTH, GLA_KEY_WIDTH, GLA_WIDTH, GLA_WIDTH, GLA_GATE_RANK)
W_IN_COLS = 3 * FOX_WIDTH + FOX_HEADS + 2 * GLA_KEY_WIDTH + 2 * GLA_WIDTH + GLA_GATE_RANK

kernel_name = "hymba_fox_gla_macaron_sandwich_ple"


def rms_norm(x, g):
    xf = x.astype(jnp.float32)
    y = xf * lax.rsqrt(jnp.mean(xf * xf, axis=-1, keepdims=True) + EPS)
    return (y * g.astype(jnp.float32)).astype(x.dtype)


def swiglu(x, w_gate, w_up, w_down):
    return (jax.nn.silu(x @ w_gate) * (x @ w_up)) @ w_down


def forgetting_attention(q, k, v, log_f):
    B, S, H, dh = q.shape
    n_blk = S // Q_BLOCK
    scale = dh ** -0.5
    qf = jnp.transpose(q, (0, 2, 1, 3)).astype(jnp.float32) * scale
    kf = jnp.transpose(k, (0, 2, 1, 3)).astype(jnp.float32)
    vf = jnp.transpose(v, (0, 2, 1, 3)).astype(jnp.float32)
    c = jnp.transpose(jnp.cumsum(log_f, axis=1), (0, 2, 1))
    k_pos = jnp.arange(S)

    def block(i):
        start = i * Q_BLOCK
        qb = lax.dynamic_slice_in_dim(qf, start, Q_BLOCK, axis=2)
        cb = lax.dynamic_slice_in_dim(c, start, Q_BLOCK, axis=2)
        s = jnp.einsum('bhqd,bhkd->bhqk', qb, kf) + cb[..., :, None] - c[:, :, None, :]
        q_pos = start + jnp.arange(Q_BLOCK)
        s = jnp.where(k_pos[None, :] <= q_pos[:, None], s, -jnp.inf)
        pr = jax.nn.softmax(s, axis=-1)
        return jnp.einsum('bhqk,bhkd->bhqd', pr, vf)

    o = lax.map(block, jnp.arange(n_blk))
    o = jnp.transpose(o, (1, 0, 3, 2, 4)).reshape(B, S, H, dh)
    return o.astype(q.dtype)


def gla_chunked(q, k, v, log_a):
    B, S, H, dk = q.shape
    dv = v.shape[-1]
    C = GLA_CHUNK
    n = S // C

    def chunks(t):
        return jnp.transpose(t.astype(jnp.float32).reshape(B, n, C, H, t.shape[-1]), (0, 3, 1, 2, 4))

    qc = chunks(q) * (dk ** -0.5)
    kc = chunks(k)
    vc = chunks(v)
    b = jnp.cumsum(chunks(log_a), axis=3)
    b_last = b[:, :, :, -1:, :]
    q_dec = qc * jnp.exp(b)
    k_inv = kc * jnp.exp(-b)
    k_to_end = kc * jnp.exp(b_last - b)
    causal = jnp.tril(jnp.ones((C, C), dtype=bool))
    A = jnp.where(causal, jnp.einsum('bhnqd,bhnkd->bhnqk', q_dec, k_inv), 0.0)
    o_intra = jnp.einsum('bhnqk,bhnke->bhnqe', A, vc)

    def step(state, inp):
        qd, kd, vv, dl = inp
        o = jnp.einsum('bhcd,bhde->bhce', qd, state)
        state = jnp.exp(dl)[..., None] * state + jnp.einsum('bhcd,bhce->bhde', kd, vv)
        return state, o

    xs = (jnp.transpose(q_dec, (2, 0, 1, 3, 4)),
          jnp.transpose(k_to_end, (2, 0, 1, 3, 4)),
          jnp.transpose(vc, (2, 0, 1, 3, 4)),
          jnp.transpose(b_last[:, :, :, 0, :], (2, 0, 1, 3)))
    state0 = jnp.zeros((B, H, dk, dv), jnp.float32)
    _, o_inter = lax.scan(step, state0, xs)
    o = o_intra + jnp.transpose(o_inter, (1, 2, 0, 3, 4))
    return jnp.transpose(o, (0, 2, 3, 1, 4)).reshape(B, S, H, dv)


def _w(key, shape, fan_in):
    return jax.random.normal(key, shape, jnp.float32) * (fan_in ** -0.5)


def _g(key, shape):
    return 1.0 + 0.02 * jax.random.normal(key, shape, jnp.float32)


def setup_inputs(seed: int = 0) -> dict:
    key = jax.random.key(seed)
    ks = jax.random.split(key, 23)
    L = DEPTH
    return {
        "x": jax.random.normal(ks[0], (BATCH, SEQ, D_MODEL), jnp.float32),
        "p": jax.random.normal(ks[1], (DEPTH, BATCH, SEQ, PLE_DIM), jnp.float32),
        "ffn1_norm_pre": _g(ks[2], (L, D_MODEL)),
        "ffn1_w_gate": _w(ks[3], (L, D_MODEL, D_FF), D_MODEL),
        "ffn1_w_up": _w(ks[4], (L, D_MODEL, D_FF), D_MODEL),
        "ffn1_w_down": _w(ks[5], (L, D_FF, D_MODEL), D_FF),
        "ffn1_norm_post": _g(ks[6], (L, D_MODEL)),
        "mix_norm_pre": _g(ks[7], (L, D_MODEL)),
        "w_in": _w(ks[8], (L, D_MODEL, W_IN_COLS), D_MODEL),
        "fox_b_f": 0.1 * jax.random.normal(ks[9], (L, FOX_HEADS), jnp.float32),
        "gla_w_gate": _w(ks[10], (L, GLA_GATE_RANK, GLA_KEY_WIDTH), GLA_GATE_RANK),
        "gla_b_gate": 0.1 * jax.random.normal(ks[11], (L, GLA_KEY_WIDTH), jnp.float32),
        "gla_norm_g": _g(ks[12], (L, GLA_HEAD_V)),
        "w_o": _w(ks[13], (L, D_MODEL, D_MODEL), D_MODEL),
        "mix_norm_post": _g(ks[14], (L, D_MODEL)),
        "ffn2_norm_pre": _g(ks[15], (L, D_MODEL)),
        "ffn2_w_gate": _w(ks[16], (L, D_MODEL, D_FF), D_MODEL),
        "ffn2_w_up": _w(ks[17], (L, D_MODEL, D_FF), D_MODEL),
        "ffn2_w_down": _w(ks[18], (L, D_FF, D_MODEL), D_FF),
        "ffn2_norm_post": _g(ks[19], (L, D_MODEL)),
        "ple_w_proj": _w(ks[20], (L, PLE_DIM, D_MODEL), PLE_DIM),
        "ple_norm": _g(ks[21], (L, D_MODEL)),
        "ple_w_gate": _w(ks[22], (L, D_MODEL, D_MODEL), D_MODEL),
    }


def reference(x, p, ffn1_norm_pre, ffn1_w_gate, ffn1_w_up, ffn1_w_down, ffn1_norm_post,
              mix_norm_pre, w_in, fox_b_f, gla_w_gate, gla_b_gate, gla_norm_g, w_o, mix_norm_post,
              ffn2_norm_pre, ffn2_w_gate, ffn2_w_up, ffn2_w_down, ffn2_norm_post,
              ple_w_proj, ple_norm, ple_w_gate):
    B, S, _ = x.shape
    offsets = np.cumsum(IN_SPLIT_SIZES)[:-1].tolist()
    h = x
    for i in range(DEPTH):
        f1 = swiglu(rms_norm(h, ffn1_norm_pre[i]), ffn1_w_gate[i], ffn1_w_up[i], ffn1_w_down[i])
        h = h + MACARON_WEIGHT * rms_norm(f1, ffn1_norm_post[i])

        a = rms_norm(h, mix_norm_pre[i])
        proj = a @ w_in[i]
        fq, fk, fv, ff, gq, gk, gv, gr, glr = jnp.split(proj, offsets, axis=-1)

        log_f = jax.nn.log_sigmoid((ff + fox_b_f[i]).astype(jnp.float32))
        o_fox = forgetting_attention(fq.reshape(B, S, FOX_HEADS, FOX_HEAD_DIM),
                                     fk.reshape(B, S, FOX_HEADS, FOX_HEAD_DIM),
                                     fv.reshape(B, S, FOX_HEADS, FOX_HEAD_DIM), log_f)

        log_a = jax.nn.log_sigmoid((glr @ gla_w_gate[i] + gla_b_gate[i]).astype(jnp.float32)) / GLA_TAU
        o_gla = gla_chunked(gq.reshape(B, S, GLA_HEADS, GLA_HEAD_K),
                            gk.reshape(B, S, GLA_HEADS, GLA_HEAD_K),
                            gv.reshape(B, S, GLA_HEADS, GLA_HEAD_V),
                            log_a.reshape(B, S, GLA_HEADS, GLA_HEAD_K))
        o_gla = rms_norm(o_gla, gla_norm_g[i]).astype(h.dtype) * jax.nn.silu(gr).reshape(B, S, GLA_HEADS, GLA_HEAD_V)

        mixed = jnp.concatenate([o_fox.reshape(B, S, FOX_WIDTH).astype(h.dtype),
                                 o_gla.reshape(B, S, GLA_WIDTH)], axis=-1)
        h = h + rms_norm(mixed @ w_o[i], mix_norm_post[i])

        f2 = swiglu(rms_norm(h, ffn2_norm_pre[i]), ffn2_w_gate[i], ffn2_w_up[i], ffn2_w_down[i])
        h = h + MACARON_WEIGHT * rms_norm(f2, ffn2_norm_post[i])

        e = rms_norm(p[i] @ ple_w_proj[i], ple_norm[i])
        h = h + e * jax.nn.sigmoid(h @ ple_w_gate[i])
    return h
```

```python
import functools

import jax
import jax.numpy as jnp
from jax import lax
from jax.experimental import pallas as pl
from jax.experimental.pallas import tpu as pltpu

F32 = jnp.float32
BF16 = jnp.bfloat16
HIGHEST = lax.Precision.HIGHEST

EPS = 1e-6
MACARON_WEIGHT = 0.5
FOX_HEAD_DIM = 128
FOX_HEADS = 16
FOX_WIDTH = FOX_HEADS * FOX_HEAD_DIM
GLA_HEADS = 4
GLA_WIDTH = 2048
GLA_HEAD_V = GLA_WIDTH // GLA_HEADS
GLA_KEY_WIDTH = GLA_WIDTH // 2
GLA_HEAD_K = GLA_KEY_WIDTH // GLA_HEADS
GLA_GATE_RANK = 16
GLA_TAU = 16.0
GLA_CHUNK = 64

LANES = 128
VMEM_LIMIT_BYTES = 56 * 1024 * 1024

MAIN_COLS = 3 * FOX_WIDTH + 2 * GLA_WIDTH + 2 * GLA_KEY_WIDTH
OFF_FQ = 0
OFF_FK = FOX_WIDTH
OFF_FV = 2 * FOX_WIDTH
OFF_GV = 3 * FOX_WIDTH
OFF_GR = OFF_GV + GLA_WIDTH
OFF_GQ = OFF_GR + GLA_WIDTH
OFF_GK = OFF_GQ + GLA_KEY_WIDTH

ROW_CHUNK = 32


def _params(*semantics):
    return pltpu.CompilerParams(dimension_semantics=semantics,
                                vmem_limit_bytes=VMEM_LIMIT_BYTES)


def _rms(x, g):
    ms = jnp.mean(x * x, axis=-1, keepdims=True)
    return x * lax.rsqrt(ms + EPS) * g


def _log_sigmoid(x):
    return jnp.minimum(x, 0.0) - jnp.log1p(jnp.exp(-jnp.abs(x)))


def _silu(x):
    return x * jax.nn.sigmoid(x)


def _for_row_chunks(n_rows, fn):
    def body(r, carry):
        fn(pl.ds(pl.multiple_of(r * ROW_CHUNK, ROW_CHUNK), ROW_CHUNK))
        return carry
    lax.fori_loop(0, n_rows // ROW_CHUNK, body, 0)


def _ffn_kernel(x_ref, gpre_ref, wg_ref, wu_ref, wd_ref, gpost_ref, o_ref, xn_ref):
    f = pl.program_id(1)
    bm = x_ref.shape[0]

    @pl.when(f == 0)
    def _prologue():
        def rows_fn(rows):
            xn_ref[rows, :] = _rms(x_ref[rows, :], gpre_ref[...]).astype(BF16)
            o_ref[rows, :] = jnp.zeros((ROW_CHUNK, o_ref.shape[1]), F32)
        _for_row_chunks(bm, rows_fn)

    xn = xn_ref[...]
    g = jnp.dot(xn, wg_ref[...], preferred_element_type=F32)
    u = jnp.dot(xn, wu_ref[...], preferred_element_type=F32)
    hid = (_silu(g) * u).astype(BF16)
    o_ref[...] += jnp.dot(hid, wd_ref[...], preferred_element_type=F32)

    @pl.when(f == pl.num_programs(1) - 1)
    def _epilogue():
        def rows_fn(rows):
            y = _rms(o_ref[rows, :], gpost_ref[...])
            o_ref[rows, :] = x_ref[rows, :] + MACARON_WEIGHT * y
        _for_row_chunks(bm, rows_fn)


def _ffn(x, g_pre, wg, wu, wd, g_post, *, bm, bf):
    S, D = x.shape
    d_ff = wg.shape[1]
    return pl.pallas_call(
        _ffn_kernel,
        name="ffn",
        grid=(S // bm, d_ff // bf),
        in_specs=[
            pl.BlockSpec((bm, D), lambda i, f: (i, 0)),
            pl.BlockSpec((1, D), lambda i, f: (0, 0)),
            pl.BlockSpec((D, bf), lambda i, f: (0, f)),
            pl.BlockSpec((D, bf), lambda i, f: (0, f)),
            pl.BlockSpec((bf, D), lambda i, f: (f, 0)),
            pl.BlockSpec((1, D), lambda i, f: (0, 0)),
        ],
        out_specs=pl.BlockSpec((bm, D), lambda i, f: (i, 0)),
        out_shape=jax.ShapeDtypeStruct((S, D), F32),
        scratch_shapes=[pltpu.VMEM((bm, D), BF16)],
        compiler_params=_params("parallel", "arbitrary"),
    )(x, g_pre, wg, wu, wd, g_post)


def _inproj_kernel(h_ref, g_ref, w_ref, ws_ref, scale_ref, o_ref, os_ref, a_ref):
    n = pl.program_id(1)

    @pl.when(n == 0)
    def _prologue():
        def rows_fn(rows):
            a_ref[rows, :] = _rms(h_ref[rows, :], g_ref[...]).astype(BF16)
        _for_row_chunks(h_ref.shape[0], rows_fn)
        os_ref[...] = jnp.dot(a_ref[...], ws_ref[...], preferred_element_type=F32)

    acc = jnp.dot(a_ref[...], w_ref[...], preferred_element_type=F32)
    o_ref[...] = (acc * scale_ref[...]).astype(BF16)


def _inproj(h, g, w_main, w_small, col_scale, *, bm, bn):
    S, D = h.shape
    n_cols = w_main.shape[1]
    return pl.pallas_call(
        _inproj_kernel,
        name="inproj",
        grid=(S // bm, n_cols // bn),
        in_specs=[
            pl.BlockSpec((bm, D), lambda i, n: (i, 0)),
            pl.BlockSpec((1, D), lambda i, n: (0, 0)),
            pl.BlockSpec((D, bn), lambda i, n: (0, n)),
            pl.BlockSpec((D, LANES), lambda i, n: (0, 0)),
            pl.BlockSpec((1, bn), lambda i, n: (0, n)),
        ],
        out_specs=[
            pl.BlockSpec((bm, bn), lambda i, n: (i, n)),
            pl.BlockSpec((bm, LANES), lambda i, n: (i, 0)),
        ],
        out_shape=[
            jax.ShapeDtypeStruct((S, n_cols), BF16),
            jax.ShapeDtypeStruct((S, LANES), F32),
        ],
        scratch_shapes=[pltpu.VMEM((bm, D), BF16)],
        compiler_params=_params("parallel", "arbitrary"),
    )(h, g, w_main, w_small, col_scale)


def _fox_gate_kernel(small_ref, bias_ref, c_ref, carry_ref):
    t = pl.program_id(0)
    T = small_ref.shape[0]

    @pl.when(t == 0)
    def _init():
        carry_ref[...] = jnp.zeros_like(carry_ref)

    log_f = _log_sigmoid(small_ref[...] + bias_ref[...])
    row = lax.broadcasted_iota(jnp.int32, (T, T), 0)
    col = lax.broadcasted_iota(jnp.int32, (T, T), 1)
    tri = jnp.where(row >= col, 1.0, 0.0).astype(F32)
    c = jnp.dot(tri, log_f, precision=HIGHEST, preferred_element_type=F32) + carry_ref[...]
    c_ref[...] = c
    carry_ref[...] = c[T - 1:T, :]


def _fox_gate(small, bias, *, bt):
    S = small.shape[0]
    return pl.pallas_call(
        _fox_gate_kernel,
        name="fox_gate",
        grid=(S // bt,),
        in_specs=[
            pl.BlockSpec((bt, LANES), lambda t: (t, 0)),
            pl.BlockSpec((1, LANES), lambda t: (0, 0)),
        ],
        out_specs=pl.BlockSpec((bt, LANES), lambda t: (t, 0)),
        out_shape=jax.ShapeDtypeStruct((S, LANES), F32),
        scratch_shapes=[pltpu.VMEM((1, LANES), F32)],
        compiler_params=_params("arbitrary"),
    )(small, bias)


def _fox_kernel(q_ref, k_ref, v_ref, ccol_ref, crow_ref, o_ref):
    h = pl.program_id(0)
    i = pl.program_id(1)
    bq = q_ref.shape[0]
    q = q_ref[...]
    lane = lax.broadcasted_iota(jnp.int32, ccol_ref.shape, 1)
    cq = jnp.sum(jnp.where(lane == h, ccol_ref[...], 0.0), axis=1, keepdims=True)

    def chunk(j, carry, masked):
        m, l, acc = carry
        ks = pl.ds(pl.multiple_of(j * bq, bq), bq)
        s = lax.dot_general(q, k_ref[ks, :], (((1,), (1,)), ((), ())),
                            preferred_element_type=F32)
        s = s + (cq - crow_ref[:, ks])
        if masked:
            row = lax.broadcasted_iota(jnp.int32, (bq, bq), 0)
            col = lax.broadcasted_iota(jnp.int32, (bq, bq), 1)
            s = jnp.where(col <= row, s, -jnp.inf)
        m_new = jnp.maximum(m, jnp.max(s, axis=1, keepdims=True))
        alpha = jnp.exp(m - m_new)
        p = jnp.exp(s - m_new)
        l = alpha * l + jnp.sum(p, axis=1, keepdims=True)
        acc = alpha * acc + jnp.dot(p.astype(BF16), v_ref[ks, :], preferred_element_type=F32)
        return m_new, l, acc

    init = (jnp.full((bq, 1), -jnp.inf, F32), jnp.zeros((bq, 1), F32),
            jnp.zeros((bq, FOX_HEAD_DIM), F32))
    carry = lax.fori_loop(0, i, lambda j, c: chunk(j, c, False), init)
    _, l, acc = chunk(i, carry, True)
    o_ref[...] = (acc / l).astype(o_ref.dtype)


def _fox(proj, c_col, c_row, *, bq):
    S = proj.shape[0]
    dh = FOX_HEAD_DIM
    return pl.pallas_call(
        _fox_kernel,
        name="fox",
        grid=(FOX_HEADS, S // bq),
        in_specs=[
            pl.BlockSpec((bq, dh), lambda h, i: (i, OFF_FQ // dh + h)),
            pl.BlockSpec((S, dh), lambda h, i: (0, OFF_FK // dh + h)),
            pl.BlockSpec((S, dh), lambda h, i: (0, OFF_FV // dh + h)),
            pl.BlockSpec((bq, LANES), lambda h, i: (i, 0)),
            pl.BlockSpec((None, 1, S), lambda h, i: (h, 0, 0)),
        ],
        out_specs=pl.BlockSpec((bq, dh), lambda h, i: (i, h)),
        out_shape=jax.ShapeDtypeStruct((S, FOX_WIDTH), BF16),
        compiler_params=_params("parallel", "arbitrary"),
    )(proj, proj, proj, c_col, c_row)


def _gla_kernel(q_ref, k_ref, v_ref, gr_ref, small_ref, wgate_ref, bgate_ref,
                gnorm_ref, o_ref, st_ref):
    t = pl.program_id(1)
    T = q_ref.shape[0]
    C = GLA_CHUNK

    @pl.when(t == 0)
    def _init():
        st_ref[...] = jnp.zeros_like(st_ref)

    gate = jnp.dot(small_ref[...], wgate_ref[...], precision=HIGHEST,
                   preferred_element_type=F32) + bgate_ref[...]
    log_a = _log_sigmoid(gate) * (1.0 / GLA_TAU)

    row = lax.broadcasted_iota(jnp.int32, (T, T), 0)
    col = lax.broadcasted_iota(jnp.int32, (T, T), 1)
    same_chunk = (row // C) == (col // C)
    causal = jnp.logical_and(same_chunk, row >= col)
    b = jnp.dot(jnp.where(causal, 1.0, 0.0).astype(F32), log_a, precision=HIGHEST,
                preferred_element_type=F32)
    b_tot = jnp.dot(jnp.where(same_chunk, 1.0, 0.0).astype(F32), log_a, precision=HIGHEST,
                    preferred_element_type=F32)

    q = q_ref[...].astype(F32)
    k = k_ref[...].astype(F32)
    v = v_ref[...]
    q_dec = (q * jnp.exp(b)).astype(BF16)
    k_inv = (k * jnp.exp(-b)).astype(BF16)
    k_end = (k * jnp.exp(b_tot - b)).astype(BF16)
    a = lax.dot_general(q_dec, k_inv, (((1,), (1,)), ((), ())), preferred_element_type=F32)
    a = jnp.where(causal, a, 0.0).astype(BF16)
    o_intra = jnp.dot(a, v, preferred_element_type=F32)

    for ci in range(T // C):
        lo, hi = ci * C, (ci + 1) * C
        st = st_ref[...]
        o_c = o_intra[lo:hi, :] + lax.dot_general(
            q_dec[lo:hi, :], st.astype(BF16), (((1,), (1,)), ((), ())),
            preferred_element_type=F32)
        decay = jnp.exp(b_tot[lo:lo + 1, :])
        st_ref[...] = st * decay + lax.dot_general(
            v[lo:hi, :], k_end[lo:hi, :], (((0,), (0,)), ((), ())),
            preferred_element_type=F32)
        y = _rms(o_c, gnorm_ref[...])
        o_ref[lo:hi, :] = (y * _silu(gr_ref[lo:hi, :].astype(F32))).astype(o_ref.dtype)


def _gla(proj, small, wgate_pad, bgate, gnorm, *, bt):
    S = proj.shape[0]
    dk, dv = GLA_HEAD_K, GLA_HEAD_V
    return pl.pallas_call(
        _gla_kernel,
        name="gla",
        grid=(GLA_HEADS, S // bt),
        in_specs=[
            pl.BlockSpec((bt, dk), lambda h, t: (t, OFF_GQ // dk + h)),
            pl.BlockSpec((bt, dk), lambda h, t: (t, OFF_GK // dk + h)),
            pl.BlockSpec((bt, dv), lambda h, t: (t, OFF_GV // dv + h)),
            pl.BlockSpec((bt, dv), lambda h, t: (t, OFF_GR // dv + h)),
            pl.BlockSpec((bt, LANES), lambda h, t: (t, 0)),
            pl.BlockSpec((LANES, dk), lambda h, t: (0, h)),
            pl.BlockSpec((1, dk), lambda h, t: (0, h)),
            pl.BlockSpec((1, dv), lambda h, t: (0, 0)),
        ],
        out_specs=pl.BlockSpec((bt, dv), lambda h, t: (t, h)),
        out_shape=jax.ShapeDtypeStruct((S, GLA_WIDTH), BF16),
        scratch_shapes=[pltpu.VMEM((dv, dk), F32)],
        compiler_params=_params("parallel", "arbitrary"),
    )(proj, proj, proj, proj, small, wgate_pad, bgate, gnorm)


def _outproj_kernel(fox_ref, gla_ref, wt_ref, wb_ref, h_ref, g_ref, o_ref):
    n = pl.program_id(1)
    bn = wt_ref.shape[1]
    cols = pl.ds(pl.multiple_of(n * bn, bn), bn)
    o_ref[:, cols] = (jnp.dot(fox_ref[...], wt_ref[...], preferred_element_type=F32)
                      + jnp.dot(gla_ref[...], wb_ref[...], preferred_element_type=F32))

    @pl.when(n == pl.num_programs(1) - 1)
    def _epilogue():
        def rows_fn(rows):
            o_ref[rows, :] = h_ref[rows, :] + _rms(o_ref[rows, :], g_ref[...])
        _for_row_chunks(h_ref.shape[0], rows_fn)


def _outproj(o_fox, o_gla, w_o, h, g, *, bm, bn):
    S, D = h.shape
    half = o_fox.shape[1]
    return pl.pallas_call(
        _outproj_kernel,
        name="outproj",
        grid=(S // bm, D // bn),
        in_specs=[
            pl.BlockSpec((bm, half), lambda i, n: (i, 0)),
            pl.BlockSpec((bm, half), lambda i, n: (i, 0)),
            pl.BlockSpec((half, bn), lambda i, n: (0, n)),
            pl.BlockSpec((half, bn), lambda i, n: (1, n)),
            pl.BlockSpec((bm, D), lambda i, n: (i, 0)),
            pl.BlockSpec((1, D), lambda i, n: (0, 0)),
        ],
        out_specs=pl.BlockSpec((bm, D), lambda i, n: (i, 0)),
        out_shape=jax.ShapeDtypeStruct((S, D), F32),
        compiler_params=_params("parallel", "arbitrary"),
    )(o_fox, o_gla, w_o, w_o, h, g)


def _ple_kernel(h_ref, p_ref, wp_ref, g_ref, wg_ref, o_ref, hb_ref, e_ref):
    n = pl.program_id(1)
    bn = wg_ref.shape[1]

    @pl.when(n == 0)
    def _prologue():
        e_ref[...] = jnp.dot(p_ref[...].astype(BF16), wp_ref[...], preferred_element_type=F32)

        def rows_fn(rows):
            hb_ref[rows, :] = h_ref[rows, :].astype(BF16)
            e_ref[rows, :] = _rms(e_ref[rows, :], g_ref[...])
        _for_row_chunks(h_ref.shape[0], rows_fn)

    cols = pl.ds(pl.multiple_of(n * bn, bn), bn)
    gate = jax.nn.sigmoid(jnp.dot(hb_ref[...], wg_ref[...], preferred_element_type=F32))
    o_ref[...] = h_ref[:, cols] + e_ref[:, cols] * gate


def _ple(h, p, w_proj, g, w_gate, *, bm, bn):
    S, D = h.shape
    dp = p.shape[1]
    return pl.pallas_call(
        _ple_kernel,
        name="ple",
        grid=(S // bm, D // bn),
        in_specs=[
            pl.BlockSpec((bm, D), lambda i, n: (i, 0)),
            pl.BlockSpec((bm, dp), lambda i, n: (i, 0)),
            pl.BlockSpec((dp, D), lambda i, n: (0, 0)),
            pl.BlockSpec((1, D), lambda i, n: (0, 0)),
            pl.BlockSpec((D, bn), lambda i, n: (0, n)),
        ],
        out_specs=pl.BlockSpec((bm, bn), lambda i, n: (i, n)),
        out_shape=jax.ShapeDtypeStruct((S, D), F32),
        scratch_shapes=[pltpu.VMEM((bm, D), BF16), pltpu.VMEM((bm, D), F32)],
        compiler_params=_params("parallel", "arbitrary"),
    )(h, p, w_proj, g, w_gate)


def _tile(n, preferred):
    t = min(n, preferred)
    while n % t:
        t //= 2
    return t


def _split_w_in(w_in):
    D = w_in.shape[0]
    o = 0
    parts = {}
    for name, width in (("fq", FOX_WIDTH), ("fk", FOX_WIDTH), ("fv", FOX_WIDTH), ("ff", FOX_HEADS),
                        ("gq", GLA_KEY_WIDTH), ("gk", GLA_KEY_WIDTH), ("gv", GLA_WIDTH),
                        ("gr", GLA_WIDTH), ("glr", GLA_GATE_RANK)):
        parts[name] = w_in[:, o:o + width]
        o += width
    main = jnp.concatenate([parts[n] for n in ("fq", "fk", "fv", "gv", "gr", "gq", "gk")],
                           axis=1).astype(BF16)
    pad = jnp.zeros((D, LANES - FOX_HEADS - GLA_GATE_RANK), w_in.dtype)
    small = jnp.concatenate([parts["ff"], parts["glr"], pad], axis=1).astype(BF16)
    return main, small


def kernel(x, p, ffn1_norm_pre, ffn1_w_gate, ffn1_w_up, ffn1_w_down, ffn1_norm_post, mix_norm_pre, w_in, fox_b_f, gla_w_gate, gla_b_gate, gla_norm_g, w_o, mix_norm_post, ffn2_norm_pre, ffn2_w_gate, ffn2_w_up, ffn2_w_down, ffn2_norm_post, ple_w_proj, ple_norm, ple_w_gate):
    B, S, D = x.shape
    assert B == 1, "the attention kernels treat the row axis as one sequence"
    depth = w_in.shape[0]
    h = x.reshape(S, D)

    bm = _tile(S, 512)
    bf = _tile(ffn1_w_gate.shape[-1], 256)
    bq = _tile(S, 512)

    col_scale = jnp.ones((1, MAIN_COLS), F32)
    col_scale = col_scale.at[:, OFF_FQ:OFF_FQ + FOX_WIDTH].set(FOX_HEAD_DIM ** -0.5)
    col_scale = col_scale.at[:, OFF_GQ:OFF_GQ + GLA_KEY_WIDTH].set(GLA_HEAD_K ** -0.5)

    def row(v):
        return v.reshape(1, -1).astype(F32)

    for i in range(depth):
        h = _ffn(h, row(ffn1_norm_pre[i]), ffn1_w_gate[i].astype(BF16), ffn1_w_up[i].astype(BF16),
                 ffn1_w_down[i].astype(BF16), row(ffn1_norm_post[i]), bm=bm, bf=bf)

        w_main, w_small = _split_w_in(w_in[i])
        proj, small = _inproj(h, row(mix_norm_pre[i]), w_main, w_small, col_scale, bm=bm, bn=512)

        bias_f = jnp.zeros((1, LANES), F32).at[0, :FOX_HEADS].set(fox_b_f[i])
        c_col = _fox_gate(small, bias_f, bt=bq)
        c_row = c_col[:, :FOX_HEADS].T.reshape(FOX_HEADS, 1, S)
        o_fox = _fox(proj, c_col, c_row, bq=bq)

        wgate_pad = jnp.zeros((LANES, GLA_KEY_WIDTH), F32).at[
            FOX_HEADS:FOX_HEADS + GLA_GATE_RANK, :].set(gla_w_gate[i])
        o_gla = _gla(proj, small, wgate_pad, row(gla_b_gate[i]), row(gla_norm_g[i]), bt=bq)

        h = _outproj(o_fox, o_gla, w_o[i].astype(BF16), h, row(mix_norm_post[i]), bm=bm, bn=512)

        h = _ffn(h, row(ffn2_norm_pre[i]), ffn2_w_gate[i].astype(BF16), ffn2_w_up[i].astype(BF16),
                 ffn2_w_down[i].astype(BF16), row(ffn2_norm_post[i]), bm=bm, bf=bf)

        h = _ple(h, p[i].reshape(S, -1), ple_w_proj[i].astype(BF16), row(ple_norm[i]),
                 ple_w_gate[i].astype(BF16), bm=bm, bn=512)
    return h.reshape(B, S, D)
```

```python
import functools

import jax
import jax.numpy as jnp
from jax import lax
from jax.experimental import pallas as pl
from jax.experimental.pallas import tpu as pltpu

F32 = jnp.float32
BF16 = jnp.bfloat16
HIGHEST = lax.Precision.HIGHEST

EPS = 1e-6
LOG2_E = 1.4426950408889634
MACARON_WEIGHT = 0.5
FOX_HEAD_DIM = 128
FOX_HEADS = 16
FOX_WIDTH = FOX_HEADS * FOX_HEAD_DIM
GLA_HEADS = 4
GLA_WIDTH = 2048
GLA_HEAD_V = GLA_WIDTH // GLA_HEADS
GLA_KEY_WIDTH = GLA_WIDTH // 2
GLA_HEAD_K = GLA_KEY_WIDTH // GLA_HEADS
GLA_GATE_RANK = 16
GLA_TAU = 16.0
GLA_CHUNK = 64

LANES = 128
VMEM_LIMIT_BYTES = 56 * 1024 * 1024

MAIN_COLS = 3 * FOX_WIDTH + 2 * GLA_WIDTH + 2 * GLA_KEY_WIDTH
OFF_FQ = 0
OFF_FK = FOX_WIDTH
OFF_FV = 2 * FOX_WIDTH
OFF_GV = 3 * FOX_WIDTH
OFF_GR = OFF_GV + GLA_WIDTH
OFF_GQ = OFF_GR + GLA_WIDTH
OFF_GK = OFF_GQ + GLA_KEY_WIDTH

ROW_CHUNK = 32


def _params(*semantics):
    return pltpu.CompilerParams(dimension_semantics=semantics,
                                vmem_limit_bytes=VMEM_LIMIT_BYTES)


def _rms(x, g):
    ms = jnp.mean(x * x, axis=-1, keepdims=True)
    return x * lax.rsqrt(ms + EPS) * g


def _log_sigmoid(x):
    return jnp.minimum(x, 0.0) - jnp.log1p(jnp.exp(-jnp.abs(x)))


def _silu(x):
    return x * jax.nn.sigmoid(x)


def _for_row_chunks(n_rows, fn):
    def body(r, carry):
        fn(pl.ds(pl.multiple_of(r * ROW_CHUNK, ROW_CHUNK), ROW_CHUNK))
        return carry
    lax.fori_loop(0, n_rows // ROW_CHUNK, body, 0)


def _ffn_kernel(x_ref, gpre_ref, wg_ref, wu_ref, wd_ref, gpost_ref, o_ref, xn_ref):
    f = pl.program_id(1)
    bm = x_ref.shape[0]

    @pl.when(f == 0)
    def _prologue():
        def rows_fn(rows):
            xn_ref[rows, :] = _rms(x_ref[rows, :], gpre_ref[...]).astype(BF16)
            o_ref[rows, :] = jnp.zeros((ROW_CHUNK, o_ref.shape[1]), F32)
        _for_row_chunks(bm, rows_fn)

    xn = xn_ref[...]
    g = jnp.dot(xn, wg_ref[...], preferred_element_type=F32)
    u = jnp.dot(xn, wu_ref[...], preferred_element_type=F32)
    hid = (_silu(g) * u).astype(BF16)
    o_ref[...] += jnp.dot(hid, wd_ref[...], preferred_element_type=F32)

    @pl.when(f == pl.num_programs(1) - 1)
    def _epilogue():
        def rows_fn(rows):
            y = _rms(o_ref[rows, :], gpost_ref[...])
            o_ref[rows, :] = x_ref[rows, :] + MACARON_WEIGHT * y
        _for_row_chunks(bm, rows_fn)


def _ffn(x, g_pre, wg, wu, wd, g_post, *, bm, bf):
    S, D = x.shape
    d_ff = wg.shape[1]
    return pl.pallas_call(
        _ffn_kernel,
        name="ffn",
        grid=(S // bm, d_ff // bf),
        in_specs=[
            pl.BlockSpec((bm, D), lambda i, f: (i, 0)),
            pl.BlockSpec((1, D), lambda i, f: (0, 0)),
            pl.BlockSpec((D, bf), lambda i, f: (0, f)),
            pl.BlockSpec((D, bf), lambda i, f: (0, f)),
            pl.BlockSpec((bf, D), lambda i, f: (f, 0)),
            pl.BlockSpec((1, D), lambda i, f: (0, 0)),
        ],
        out_specs=pl.BlockSpec((bm, D), lambda i, f: (i, 0)),
        out_shape=jax.ShapeDtypeStruct((S, D), F32),
        scratch_shapes=[pltpu.VMEM((bm, D), BF16)],
        compiler_params=_params("parallel", "arbitrary"),
    )(x, g_pre, wg, wu, wd, g_post)


def _inproj_kernel(h_ref, g_ref, w_ref, ws_ref, scale_ref, o_ref, os_ref, a_ref):
    n = pl.program_id(1)

    @pl.when(n == 0)
    def _prologue():
        def rows_fn(rows):
            a_ref[rows, :] = _rms(h_ref[rows, :], g_ref[...]).astype(BF16)
        _for_row_chunks(h_ref.shape[0], rows_fn)
        os_ref[...] = jnp.dot(a_ref[...], ws_ref[...], preferred_element_type=F32)

    acc = jnp.dot(a_ref[...], w_ref[...], preferred_element_type=F32)
    o_ref[...] = (acc * scale_ref[...]).astype(BF16)


def _inproj(h, g, w_main, w_small, col_scale, *, bm, bn):
    S, D = h.shape
    n_cols = w_main.shape[1]
    return pl.pallas_call(
        _inproj_kernel,
        name="inproj",
        grid=(S // bm, n_cols // bn),
        in_specs=[
            pl.BlockSpec((bm, D), lambda i, n: (i, 0)),
            pl.BlockSpec((1, D), lambda i, n: (0, 0)),
            pl.BlockSpec((D, bn), lambda i, n: (0, n)),
            pl.BlockSpec((D, LANES), lambda i, n: (0, 0)),
            pl.BlockSpec((1, bn), lambda i, n: (0, n)),
        ],
        out_specs=[
            pl.BlockSpec((bm, bn), lambda i, n: (i, n)),
            pl.BlockSpec((bm, LANES), lambda i, n: (i, 0)),
        ],
        out_shape=[
            jax.ShapeDtypeStruct((S, n_cols), BF16),
            jax.ShapeDtypeStruct((S, LANES), F32),
        ],
        scratch_shapes=[pltpu.VMEM((bm, D), BF16)],
        compiler_params=_params("parallel", "arbitrary"),
    )(h, g, w_main, w_small, col_scale)


def _fox_gate_kernel(small_ref, bias_ref, c_ref, carry_ref):
    t = pl.program_id(0)
    T = small_ref.shape[0]

    @pl.when(t == 0)
    def _init():
        carry_ref[...] = jnp.zeros_like(carry_ref)

    log_f = _log_sigmoid(small_ref[...] + bias_ref[...])
    row = lax.broadcasted_iota(jnp.int32, (T, T), 0)
    col = lax.broadcasted_iota(jnp.int32, (T, T), 1)
    tri = jnp.where(row >= col, 1.0, 0.0).astype(F32)
    c = jnp.dot(tri, log_f, precision=HIGHEST, preferred_element_type=F32) + carry_ref[...]
    c_ref[...] = c * LOG2_E
    carry_ref[...] = c[T - 1:T, :]


def _fox_gate(small, bias, *, bt):
    S = small.shape[0]
    return pl.pallas_call(
        _fox_gate_kernel,
        name="fox_gate",
        grid=(S // bt,),
        in_specs=[
            pl.BlockSpec((bt, LANES), lambda t: (t, 0)),
            pl.BlockSpec((1, LANES), lambda t: (0, 0)),
        ],
        out_specs=pl.BlockSpec((bt, LANES), lambda t: (t, 0)),
        out_shape=jax.ShapeDtypeStruct((S, LANES), F32),
        scratch_shapes=[pltpu.VMEM((1, LANES), F32)],
        compiler_params=_params("arbitrary"),
    )(small, bias)


FOX_PAIR = 2
FOX_ROWS = 16


def _split3(c):
    hi = c.astype(BF16).astype(F32)
    mid = (c - hi).astype(BF16).astype(F32)
    lo = (c - hi - mid).astype(BF16).astype(F32)
    return hi, mid, lo


def _head_column(block, head):
    lane = lax.broadcasted_iota(jnp.int32, block.shape, 1)
    return jnp.sum(jnp.where(lane == head, block, 0.0), axis=1, keepdims=True)


def _fox_kernel(q_ref, k_ref, v_ref, c_ref, o_ref, kaug_ref, vaug_ref, qaug_ref, s_ref, p_ref,
                m_ref, alpha_ref, acc_ref, *, bk):
    hp = pl.program_id(0)
    i = pl.program_id(1)
    bq = q_ref.shape[0]
    S = k_ref.shape[0]
    dh = FOX_HEAD_DIM
    R = FOX_ROWS

    @pl.when(i == 0)
    def _build_k_side():
        rows_per = 128

        def body(r, carry):
            rows = pl.ds(pl.multiple_of(r * rows_per, rows_per), rows_per)
            cblk = c_ref[rows, :]
            lane = lax.broadcasted_iota(jnp.int32, (rows_per, LANES), 1)
            for hh in range(FOX_PAIR):
                hi, mid, lo = _split3(_head_column(cblk, hp * FOX_PAIR + hh))
                extra = jnp.where(lane < 3, 1.0,
                                  jnp.where(lane == 3, -hi,
                                            jnp.where(lane == 4, -mid,
                                                      jnp.where(lane == 5, -lo, 0.0))))
                kaug_ref[hh, rows, 0:dh] = k_ref[rows, hh * dh:(hh + 1) * dh]
                kaug_ref[hh, rows, dh:2 * dh] = extra.astype(BF16)
                vaug_ref[hh, rows, 0:dh] = v_ref[rows, hh * dh:(hh + 1) * dh]
                vaug_ref[hh, rows, dh:2 * dh] = jnp.ones((rows_per, dh), BF16)
            return carry
        lax.fori_loop(0, S // rows_per, body, 0)

    cq_blk = c_ref[pl.ds(pl.multiple_of(i * bq, bq), bq), :]
    lane_q = lax.broadcasted_iota(jnp.int32, (bq, LANES), 1)
    for hh in range(FOX_PAIR):
        hi, mid, lo = _split3(_head_column(cq_blk, hp * FOX_PAIR + hh))
        extra = jnp.where(lane_q == 0, hi,
                          jnp.where(lane_q == 1, mid,
                                    jnp.where(lane_q == 2, lo,
                                              jnp.where(lane_q < 6, 1.0, 0.0))))
        qaug_ref[hh, :, 0:dh] = q_ref[:, hh * dh:(hh + 1) * dh]
        qaug_ref[hh, :, dh:2 * dh] = extra.astype(BF16)
        m_ref[hh] = jnp.full((bq, LANES), -jnp.inf, F32)
        acc_ref[hh] = jnp.zeros((bq, 2 * dh), F32)

    def chunk(k0, width, masked):
        ks = pl.ds(pl.multiple_of(k0, bq), width)
        groups = [slice(g * R, (g + 1) * R) for g in range(bq // R)]
        for hh in range(FOX_PAIR):
            s_ref[hh, :, 0:width] = lax.dot_general(
                qaug_ref[hh], kaug_ref[hh, ks, :], (((1,), (1,)), ((), ())),
                preferred_element_type=F32)
        if masked:
            diff = (lax.broadcasted_iota(jnp.int32, (R, width), 1)
                    - lax.broadcasted_iota(jnp.int32, (R, width), 0))
            limit = i * bq - k0
        for hh in range(FOX_PAIR):
            for g, rows in enumerate(groups):
                s = s_ref[hh, rows, 0:width]
                if masked:
                    s = jnp.where(diff <= limit + g * R, s, -jnp.inf)
                    s_ref[hh, rows, 0:width] = s
                m_old = m_ref[hh, rows, :]
                m_new = jnp.maximum(m_old, jnp.broadcast_to(
                    jnp.max(s, axis=1, keepdims=True), (R, LANES)))
                alpha_ref[hh, rows, :] = jnp.exp2(m_old - m_new)
                m_ref[hh, rows, :] = m_new
        for hh in range(FOX_PAIR):
            for rows in groups:
                m = pltpu.repeat(m_ref[hh, rows, :], width // LANES, axis=1)
                p_ref[hh, rows, 0:width] = jnp.exp2(s_ref[hh, rows, 0:width] - m).astype(BF16)
        for hh in range(FOX_PAIR):
            alpha = pltpu.repeat(alpha_ref[hh], 2, axis=1)
            acc_ref[hh] = alpha * acc_ref[hh] + jnp.dot(
                p_ref[hh, :, 0:width], vaug_ref[hh, ks, :], preferred_element_type=F32)

    n_full = (i * bq) // bk

    def full_body(j, carry):
        chunk(j * bk, bk, False)
        return carry
    lax.fori_loop(0, n_full, full_body, 0)

    tail_start = n_full * bk
    tail_blocks = (i + 1) - n_full * (bk // bq)
    for t in range(1, bk // bq + 1):
        @pl.when(tail_blocks == t)
        def _tail(t=t):
            chunk(tail_start, t * bq, True)

    for hh in range(FOX_PAIR):
        acc = acc_ref[hh]
        o_ref[:, hh * dh:(hh + 1) * dh] = (acc[:, 0:dh] / acc[:, dh:2 * dh]).astype(o_ref.dtype)


def _fox(proj, c2, *, bq, bk):
    S = proj.shape[0]
    dh = FOX_HEAD_DIM
    w = FOX_PAIR * dh
    return pl.pallas_call(
        functools.partial(_fox_kernel, bk=bk),
        name="fox",
        grid=(FOX_HEADS // FOX_PAIR, S // bq),
        in_specs=[
            pl.BlockSpec((bq, w), lambda h, i: (i, OFF_FQ // w + h)),
            pl.BlockSpec((S, w), lambda h, i: (0, OFF_FK // w + h)),
            pl.BlockSpec((S, w), lambda h, i: (0, OFF_FV // w + h)),
            pl.BlockSpec((S, LANES), lambda h, i: (0, 0)),
        ],
        out_specs=pl.BlockSpec((bq, w), lambda h, i: (i, h)),
        out_shape=jax.ShapeDtypeStruct((S, FOX_WIDTH), BF16),
        scratch_shapes=[
            pltpu.VMEM((FOX_PAIR, S, 2 * dh), BF16),
            pltpu.VMEM((FOX_PAIR, S, 2 * dh), BF16),
            pltpu.VMEM((FOX_PAIR, bq, 2 * dh), BF16),
            pltpu.VMEM((FOX_PAIR, bq, bk), F32),
            pltpu.VMEM((FOX_PAIR, bq, bk), BF16),
            pltpu.VMEM((FOX_PAIR, bq, LANES), F32),
            pltpu.VMEM((FOX_PAIR, bq, LANES), F32),
            pltpu.VMEM((FOX_PAIR, bq, 2 * dh), F32),
        ],
        compiler_params=_params("parallel", "arbitrary"),
    )(proj, proj, proj, c2)


def _gla_kernel(q_ref, k_ref, v_ref, gr_ref, small_ref, wgate_ref, bgate_ref,
                gnorm_ref, o_ref, st_ref):
    t = pl.program_id(1)
    T = q_ref.shape[0]
    C = GLA_CHUNK

    @pl.when(t == 0)
    def _init():
        st_ref[...] = jnp.zeros_like(st_ref)

    gate = jnp.dot(small_ref[...], wgate_ref[...], precision=HIGHEST,
                   preferred_element_type=F32) + bgate_ref[...]
    log_a = _log_sigmoid(gate) * (1.0 / GLA_TAU)

    row = lax.broadcasted_iota(jnp.int32, (T, T), 0)
    col = lax.broadcasted_iota(jnp.int32, (T, T), 1)
    same_chunk = (row // C) == (col // C)
    causal = jnp.logical_and(same_chunk, row >= col)
    b = jnp.dot(jnp.where(causal, 1.0, 0.0).astype(F32), log_a, precision=HIGHEST,
                preferred_element_type=F32)
    b_tot = jnp.dot(jnp.where(same_chunk, 1.0, 0.0).astype(F32), log_a, precision=HIGHEST,
                    preferred_element_type=F32)

    q = q_ref[...].astype(F32)
    k = k_ref[...].astype(F32)
    v = v_ref[...]
    q_dec = (q * jnp.exp(b)).astype(BF16)
    k_inv = (k * jnp.exp(-b)).astype(BF16)
    k_end = (k * jnp.exp(b_tot - b)).astype(BF16)
    a = lax.dot_general(q_dec, k_inv, (((1,), (1,)), ((), ())), preferred_element_type=F32)
    a = jnp.where(causal, a, 0.0).astype(BF16)
    o_intra = jnp.dot(a, v, preferred_element_type=F32)

    for ci in range(T // C):
        lo, hi = ci * C, (ci + 1) * C
        st = st_ref[...]
        o_c = o_intra[lo:hi, :] + lax.dot_general(
            q_dec[lo:hi, :], st.astype(BF16), (((1,), (1,)), ((), ())),
            preferred_element_type=F32)
        decay = jnp.exp(b_tot[lo:lo + 1, :])
        st_ref[...] = st * decay + lax.dot_general(
            v[lo:hi, :], k_end[lo:hi, :], (((0,), (0,)), ((), ())),
            preferred_element_type=F32)
        y = _rms(o_c, gnorm_ref[...])
        o_ref[lo:hi, :] = (y * _silu(gr_ref[lo:hi, :].astype(F32))).astype(o_ref.dtype)


def _gla(proj, small, wgate_pad, bgate, gnorm, *, bt):
    S = proj.shape[0]
    dk, dv = GLA_HEAD_K, GLA_HEAD_V
    return pl.pallas_call(
        _gla_kernel,
        name="gla",
        grid=(GLA_HEADS, S // bt),
        in_specs=[
            pl.BlockSpec((bt, dk), lambda h, t: (t, OFF_GQ // dk + h)),
            pl.BlockSpec((bt, dk), lambda h, t: (t, OFF_GK // dk + h)),
            pl.BlockSpec((bt, dv), lambda h, t: (t, OFF_GV // dv + h)),
            pl.BlockSpec((bt, dv), lambda h, t: (t, OFF_GR // dv + h)),
            pl.BlockSpec((bt, LANES), lambda h, t: (t, 0)),
            pl.BlockSpec((LANES, dk), lambda h, t: (0, h)),
            pl.BlockSpec((1, dk), lambda h, t: (0, h)),
            pl.BlockSpec((1, dv), lambda h, t: (0, 0)),
        ],
        out_specs=pl.BlockSpec((bt, dv), lambda h, t: (t, h)),
        out_shape=jax.ShapeDtypeStruct((S, GLA_WIDTH), BF16),
        scratch_shapes=[pltpu.VMEM((dv, dk), F32)],
        compiler_params=_params("parallel", "arbitrary"),
    )(proj, proj, proj, proj, small, wgate_pad, bgate, gnorm)


def _outproj_kernel(fox_ref, gla_ref, wt_ref, wb_ref, h_ref, g_ref, o_ref):
    n = pl.program_id(1)
    bn = wt_ref.shape[1]
    cols = pl.ds(pl.multiple_of(n * bn, bn), bn)
    o_ref[:, cols] = (jnp.dot(fox_ref[...], wt_ref[...], preferred_element_type=F32)
                      + jnp.dot(gla_ref[...], wb_ref[...], preferred_element_type=F32))

    @pl.when(n == pl.num_programs(1) - 1)
    def _epilogue():
        def rows_fn(rows):
            o_ref[rows, :] = h_ref[rows, :] + _rms(o_ref[rows, :], g_ref[...])
        _for_row_chunks(h_ref.shape[0], rows_fn)


def _outproj(o_fox, o_gla, w_o, h, g, *, bm, bn):
    S, D = h.shape
    half = o_fox.shape[1]
    return pl.pallas_call(
        _outproj_kernel,
        name="outproj",
        grid=(S // bm, D // bn),
        in_specs=[
            pl.BlockSpec((bm, half), lambda i, n: (i, 0)),
            pl.BlockSpec((bm, half), lambda i, n: (i, 0)),
            pl.BlockSpec((half, bn), lambda i, n: (0, n)),
            pl.BlockSpec((half, bn), lambda i, n: (1, n)),
            pl.BlockSpec((bm, D), lambda i, n: (i, 0)),
            pl.BlockSpec((1, D), lambda i, n: (0, 0)),
        ],
        out_specs=pl.BlockSpec((bm, D), lambda i, n: (i, 0)),
        out_shape=jax.ShapeDtypeStruct((S, D), F32),
        compiler_params=_params("parallel", "arbitrary"),
    )(o_fox, o_gla, w_o, w_o, h, g)


def _ple_kernel(h_ref, p_ref, wp_ref, g_ref, wg_ref, o_ref, hb_ref, e_ref):
    n = pl.program_id(1)
    bn = wg_ref.shape[1]

    @pl.when(n == 0)
    def _prologue():
        e_ref[...] = jnp.dot(p_ref[...].astype(BF16), wp_ref[...], preferred_element_type=F32)

        def rows_fn(rows):
            hb_ref[rows, :] = h_ref[rows, :].astype(BF16)
            e_ref[rows, :] = _rms(e_ref[rows, :], g_ref[...])
        _for_row_chunks(h_ref.shape[0], rows_fn)

    cols = pl.ds(pl.multiple_of(n * bn, bn), bn)
    gate = jax.nn.sigmoid(jnp.dot(hb_ref[...], wg_ref[...], preferred_element_type=F32))
    o_ref[...] = h_ref[:, cols] + e_ref[:, cols] * gate


def _ple(h, p, w_proj, g, w_gate, *, bm, bn):
    S, D = h.shape
    dp = p.shape[1]
    return pl.pallas_call(
        _ple_kernel,
        name="ple",
        grid=(S // bm, D // bn),
        in_specs=[
            pl.BlockSpec((bm, D), lambda i, n: (i, 0)),
            pl.BlockSpec((bm, dp), lambda i, n: (i, 0)),
            pl.BlockSpec((dp, D), lambda i, n: (0, 0)),
            pl.BlockSpec((1, D), lambda i, n: (0, 0)),
            pl.BlockSpec((D, bn), lambda i, n: (0, n)),
        ],
        out_specs=pl.BlockSpec((bm, bn), lambda i, n: (i, n)),
        out_shape=jax.ShapeDtypeStruct((S, D), F32),
        scratch_shapes=[pltpu.VMEM((bm, D), BF16), pltpu.VMEM((bm, D), F32)],
        compiler_params=_params("parallel", "arbitrary"),
    )(h, p, w_proj, g, w_gate)


def _tile(n, preferred):
    t = min(n, preferred)
    while n % t:
        t //= 2
    return t


def _split_w_in(w_in):
    D = w_in.shape[0]
    o = 0
    parts = {}
    for name, width in (("fq", FOX_WIDTH), ("fk", FOX_WIDTH), ("fv", FOX_WIDTH), ("ff", FOX_HEADS),
                        ("gq", GLA_KEY_WIDTH), ("gk", GLA_KEY_WIDTH), ("gv", GLA_WIDTH),
                        ("gr", GLA_WIDTH), ("glr", GLA_GATE_RANK)):
        parts[name] = w_in[:, o:o + width]
        o += width
    main = jnp.concatenate([parts[n] for n in ("fq", "fk", "fv", "gv", "gr", "gq", "gk")],
                           axis=1).astype(BF16)
    pad = jnp.zeros((D, LANES - FOX_HEADS - GLA_GATE_RANK), w_in.dtype)
    small = jnp.concatenate([parts["ff"], parts["glr"], pad], axis=1).astype(BF16)
    return main, small


def kernel(x, p, ffn1_norm_pre, ffn1_w_gate, ffn1_w_up, ffn1_w_down, ffn1_norm_post, mix_norm_pre, w_in, fox_b_f, gla_w_gate, gla_b_gate, gla_norm_g, w_o, mix_norm_post, ffn2_norm_pre, ffn2_w_gate, ffn2_w_up, ffn2_w_down, ffn2_norm_post, ple_w_proj, ple_norm, ple_w_gate):
    B, S, D = x.shape
    assert B == 1, "the attention kernels treat the row axis as one sequence"
    depth = w_in.shape[0]
    h = x.reshape(S, D)

    bm = _tile(S, 512)
    bf = _tile(ffn1_w_gate.shape[-1], 256)
    bq = _tile(S, 512)
    bk = max(bq, _tile(S, 1024))
    bt = _tile(S, 512)

    col_scale = jnp.ones((1, MAIN_COLS), F32)
    col_scale = col_scale.at[:, OFF_FQ:OFF_FQ + FOX_WIDTH].set(FOX_HEAD_DIM ** -0.5 * LOG2_E)
    col_scale = col_scale.at[:, OFF_GQ:OFF_GQ + GLA_KEY_WIDTH].set(GLA_HEAD_K ** -0.5)

    def row(v):
        return v.reshape(1, -1).astype(F32)

    for i in range(depth):
        h = _ffn(h, row(ffn1_norm_pre[i]), ffn1_w_gate[i].astype(BF16), ffn1_w_up[i].astype(BF16),
                 ffn1_w_down[i].astype(BF16), row(ffn1_norm_post[i]), bm=bm, bf=bf)

        w_main, w_small = _split_w_in(w_in[i])
        proj, small = _inproj(h, row(mix_norm_pre[i]), w_main, w_small, col_scale, bm=bm, bn=1024)

        bias_f = jnp.zeros((1, LANES), F32).at[0, :FOX_HEADS].set(fox_b_f[i])
        c2 = _fox_gate(small, bias_f, bt=bt)
        o_fox = _fox(proj, c2, bq=bq, bk=bk)

        wgate_pad = jnp.zeros((LANES, GLA_KEY_WIDTH), F32).at[
            FOX_HEADS:FOX_HEADS + GLA_GATE_RANK, :].set(gla_w_gate[i])
        o_gla = _gla(proj, small, wgate_pad, row(gla_b_gate[i]), row(gla_norm_g[i]), bt=bt)

        h = _outproj(o_fox, o_gla, w_o[i].astype(BF16), h, row(mix_norm_post[i]), bm=bm, bn=512)

        h = _ffn(h, row(ffn2_norm_pre[i]), ffn2_w_gate[i].astype(BF16), ffn2_w_up[i].astype(BF16),
                 ffn2_w_down[i].astype(BF16), row(ffn2_norm_post[i]), bm=bm, bf=bf)

        h = _ple(h, p[i].reshape(S, -1), ple_w_proj[i].astype(BF16), row(ple_norm[i]),
                 ple_w_gate[i].astype(BF16), bm=bm, bn=1024)
    return h.reshape(B, S, D)
```

```python
import functools

import jax
import jax.numpy as jnp
from jax import lax
from jax.experimental import pallas as pl
from jax.experimental.pallas import tpu as pltpu

F32 = jnp.float32
BF16 = jnp.bfloat16
HIGHEST = lax.Precision.HIGHEST

EPS = 1e-6
LOG2_E = 1.4426950408889634
MACARON_WEIGHT = 0.5
FOX_HEAD_DIM = 128
FOX_HEADS = 16
FOX_WIDTH = FOX_HEADS * FOX_HEAD_DIM
GLA_HEADS = 4
GLA_WIDTH = 2048
GLA_HEAD_V = GLA_WIDTH // GLA_HEADS
GLA_KEY_WIDTH = GLA_WIDTH // 2
GLA_HEAD_K = GLA_KEY_WIDTH // GLA_HEADS
GLA_GATE_RANK = 16
GLA_TAU = 16.0
GLA_CHUNK = 64

LANES = 128
MXU_COLS = 256
VMEM_LIMIT_BYTES = 56 * 1024 * 1024

MAIN_COLS = 3 * FOX_WIDTH + 2 * GLA_WIDTH + 2 * GLA_KEY_WIDTH
OFF_FQ = 0
OFF_FK = FOX_WIDTH
OFF_FV = 2 * FOX_WIDTH
OFF_GQ = 3 * FOX_WIDTH
OFF_GK = OFF_GQ + GLA_KEY_WIDTH
OFF_GV = OFF_GK + GLA_KEY_WIDTH
OFF_GR = OFF_GV + GLA_WIDTH

ROW_CHUNK = 32


def _params(*semantics):
    return pltpu.CompilerParams(dimension_semantics=semantics,
                                vmem_limit_bytes=VMEM_LIMIT_BYTES)


def _rms(x, g):
    ms = jnp.mean(x * x, axis=-1, keepdims=True)
    return x * lax.rsqrt(ms + EPS) * g


def _log_sigmoid(x):
    return jnp.minimum(x, 0.0) - jnp.log1p(jnp.exp(-jnp.abs(x)))


def _silu(x):
    return x * jax.nn.sigmoid(x)


def _sub_tiles(n_cols):
    return [slice(c, c + MXU_COLS) for c in range(0, n_cols, MXU_COLS)]


def _for_row_chunks(n_rows, fn):
    def body(r, carry):
        fn(pl.ds(pl.multiple_of(r * ROW_CHUNK, ROW_CHUNK), ROW_CHUNK))
        return carry
    lax.fori_loop(0, n_rows // ROW_CHUNK, body, 0)


def _ffn_kernel(x_ref, gpre_ref, wg_ref, wu_ref, wd_ref, gpost_ref, o_ref, xn_ref):
    f = pl.program_id(1)
    bm = x_ref.shape[0]

    @pl.when(f == 0)
    def _prologue():
        def rows_fn(rows):
            xn_ref[rows, :] = _rms(x_ref[rows, :], gpre_ref[...]).astype(BF16)
            o_ref[rows, :] = jnp.zeros((ROW_CHUNK, o_ref.shape[1]), F32)
        _for_row_chunks(bm, rows_fn)

    xn = xn_ref[...]
    g = jnp.dot(xn, wg_ref[...], preferred_element_type=F32)
    u = jnp.dot(xn, wu_ref[...], preferred_element_type=F32)
    hid = (_silu(g) * u).astype(BF16)
    o_ref[...] += jnp.dot(hid, wd_ref[...], preferred_element_type=F32)

    @pl.when(f == pl.num_programs(1) - 1)
    def _epilogue():
        def rows_fn(rows):
            y = _rms(o_ref[rows, :], gpost_ref[...])
            o_ref[rows, :] = x_ref[rows, :] + MACARON_WEIGHT * y
        _for_row_chunks(bm, rows_fn)


def _ffn(x, g_pre, wg, wu, wd, g_post, *, bm, bf):
    S, D = x.shape
    d_ff = wg.shape[1]
    return pl.pallas_call(
        _ffn_kernel,
        name="ffn",
        grid=(S // bm, d_ff // bf),
        in_specs=[
            pl.BlockSpec((bm, D), lambda i, f: (i, 0)),
            pl.BlockSpec((1, D), lambda i, f: (0, 0)),
            pl.BlockSpec((D, bf), lambda i, f: (0, f)),
            pl.BlockSpec((D, bf), lambda i, f: (0, f)),
            pl.BlockSpec((bf, D), lambda i, f: (f, 0)),
            pl.BlockSpec((1, D), lambda i, f: (0, 0)),
        ],
        out_specs=pl.BlockSpec((bm, D), lambda i, f: (i, 0)),
        out_shape=jax.ShapeDtypeStruct((S, D), F32),
        scratch_shapes=[pltpu.VMEM((bm, D), BF16)],
        compiler_params=_params("parallel", "arbitrary"),
    )(x, g_pre, wg, wu, wd, g_post)


def _inproj_kernel(h_ref, g_ref, w_ref, ws_ref, scale_ref, o_ref, os_ref, a_ref):
    n = pl.program_id(1)

    @pl.when(n == 0)
    def _prologue():
        def rows_fn(rows):
            a_ref[rows, :] = _rms(h_ref[rows, :], g_ref[...]).astype(BF16)
        _for_row_chunks(h_ref.shape[0], rows_fn)
        os_ref[...] = jnp.dot(a_ref[...], ws_ref[...], preferred_element_type=F32)

    for cols in _sub_tiles(w_ref.shape[1]):
        acc = jnp.dot(a_ref[...], w_ref[:, cols], preferred_element_type=F32)
        o_ref[:, cols] = (acc * scale_ref[:, cols]).astype(BF16)


def _inproj(h, g, w_main, w_small, col_scale, *, bm, bn):
    S, D = h.shape
    n_cols = w_main.shape[1]
    return pl.pallas_call(
        _inproj_kernel,
        name="inproj",
        grid=(S // bm, n_cols // bn),
        in_specs=[
            pl.BlockSpec((bm, D), lambda i, n: (i, 0)),
            pl.BlockSpec((1, D), lambda i, n: (0, 0)),
            pl.BlockSpec((D, bn), lambda i, n: (0, n)),
            pl.BlockSpec((D, LANES), lambda i, n: (0, 0)),
            pl.BlockSpec((1, bn), lambda i, n: (0, n)),
        ],
        out_specs=[
            pl.BlockSpec((bm, bn), lambda i, n: (i, n)),
            pl.BlockSpec((bm, LANES), lambda i, n: (i, 0)),
        ],
        out_shape=[
            jax.ShapeDtypeStruct((S, n_cols), BF16),
            jax.ShapeDtypeStruct((S, LANES), F32),
        ],
        scratch_shapes=[pltpu.VMEM((bm, D), BF16)],
        compiler_params=_params("parallel", "arbitrary"),
    )(h, g, w_main, w_small, col_scale)


def _fox_gate_kernel(small_ref, bias_ref, c_ref, carry_ref):
    t = pl.program_id(0)
    T = small_ref.shape[0]

    @pl.when(t == 0)
    def _init():
        carry_ref[...] = jnp.zeros_like(carry_ref)

    log_f = _log_sigmoid(small_ref[...] + bias_ref[...])
    row = lax.broadcasted_iota(jnp.int32, (T, T), 0)
    col = lax.broadcasted_iota(jnp.int32, (T, T), 1)
    tri = jnp.where(row >= col, 1.0, 0.0).astype(F32)
    c = jnp.dot(tri, log_f, precision=HIGHEST, preferred_element_type=F32) + carry_ref[...]
    c_ref[...] = c * LOG2_E
    carry_ref[...] = c[T - 1:T, :]


def _fox_gate(small, bias, *, bt):
    S = small.shape[0]
    return pl.pallas_call(
        _fox_gate_kernel,
        name="fox_gate",
        grid=(S // bt,),
        in_specs=[
            pl.BlockSpec((bt, LANES), lambda t: (t, 0)),
            pl.BlockSpec((1, LANES), lambda t: (0, 0)),
        ],
        out_specs=pl.BlockSpec((bt, LANES), lambda t: (t, 0)),
        out_shape=jax.ShapeDtypeStruct((S, LANES), F32),
        scratch_shapes=[pltpu.VMEM((1, LANES), F32)],
        compiler_params=_params("arbitrary"),
    )(small, bias)


FOX_PAIR = 2
FOX_ROWS = 16


def _split3(c):
    hi = c.astype(BF16).astype(F32)
    mid = (c - hi).astype(BF16).astype(F32)
    lo = (c - hi - mid).astype(BF16).astype(F32)
    return hi, mid, lo


def _lane_tile(x, n):
    return jnp.concatenate([x] * n, axis=1)


def _head_column(block, head):
    lane = lax.broadcasted_iota(jnp.int32, block.shape, 1)
    return jnp.sum(jnp.where(lane == head, block, 0.0), axis=1, keepdims=True)


def _fox_kernel(q_ref, k_ref, v_ref, c_ref, o_ref, kaug_ref, vaug_ref, qaug_ref, s_ref, p_ref,
                m_ref, alpha_ref, acc_ref, *, bk):
    hp = pl.program_id(0)
    i = pl.program_id(1)
    bq = q_ref.shape[0]
    S = k_ref.shape[0]
    dh = FOX_HEAD_DIM
    R = FOX_ROWS

    @pl.when(i == 0)
    def _build_kv_side():
        rows_per = 128

        def body(r, carry):
            rows = pl.ds(pl.multiple_of(r * rows_per, rows_per), rows_per)
            cblk = c_ref[rows, :]
            lane = lax.broadcasted_iota(jnp.int32, (rows_per, LANES), 1)
            for hh in range(FOX_PAIR):
                hi, mid, lo = _split3(_head_column(cblk, hp * FOX_PAIR + hh))
                extra = jnp.where(lane < 3, 1.0,
                                  jnp.where(lane == 3, -hi,
                                            jnp.where(lane == 4, -mid,
                                                      jnp.where(lane == 5, -lo, 0.0))))
                kaug_ref[hh, rows, 0:dh] = k_ref[rows, hh * dh:(hh + 1) * dh]
                kaug_ref[hh, rows, dh:2 * dh] = extra.astype(BF16)
                vaug_ref[hh, rows, 0:dh] = v_ref[rows, hh * dh:(hh + 1) * dh]
                vaug_ref[hh, rows, dh:2 * dh] = jnp.ones((rows_per, dh), BF16)
            return carry
        lax.fori_loop(0, S // rows_per, body, 0)

    cq_blk = c_ref[pl.ds(pl.multiple_of(i * bq, bq), bq), :]
    lane_q = lax.broadcasted_iota(jnp.int32, (bq, LANES), 1)
    for hh in range(FOX_PAIR):
        hi, mid, lo = _split3(_head_column(cq_blk, hp * FOX_PAIR + hh))
        extra = jnp.where(lane_q == 0, hi,
                          jnp.where(lane_q == 1, mid,
                                    jnp.where(lane_q == 2, lo,
                                              jnp.where(lane_q < 6, 1.0, 0.0))))
        qaug_ref[hh, :, 0:dh] = q_ref[:, hh * dh:(hh + 1) * dh]
        qaug_ref[hh, :, dh:2 * dh] = extra.astype(BF16)
        m_ref[hh] = jnp.full((bq, LANES), -jnp.inf, F32)
        acc_ref[hh] = jnp.zeros((bq, 2 * dh), F32)

    def chunk(k0, width, masked):
        ks = pl.ds(pl.multiple_of(k0, bq), width)
        groups = [slice(g * R, (g + 1) * R) for g in range(bq // R)]
        for hh in range(FOX_PAIR):
            s_ref[hh, :, 0:width] = lax.dot_general(
                qaug_ref[hh], kaug_ref[hh, ks, :], (((1,), (1,)), ((), ())),
                preferred_element_type=F32)
        if masked:
            diff = (lax.broadcasted_iota(jnp.int32, (R, width), 1)
                    - lax.broadcasted_iota(jnp.int32, (R, width), 0))
            limit = i * bq - k0
        for hh in range(FOX_PAIR):
            for g, rows in enumerate(groups):
                s = s_ref[hh, rows, 0:width]
                if masked:
                    s = jnp.where(diff <= limit + g * R, s, -jnp.inf)
                    s_ref[hh, rows, 0:width] = s
                m_old = m_ref[hh, rows, :]
                m_new = jnp.maximum(m_old, jnp.broadcast_to(
                    jnp.max(s, axis=1, keepdims=True), (R, LANES)))
                alpha_ref[hh, rows, :] = jnp.exp2(m_old - m_new)
                m_ref[hh, rows, :] = m_new
        for hh in range(FOX_PAIR):
            for rows in groups:
                m = _lane_tile(m_ref[hh, rows, :], width // LANES)
                p_ref[hh, rows, 0:width] = jnp.exp2(s_ref[hh, rows, 0:width] - m).astype(BF16)
        for hh in range(FOX_PAIR):
            alpha = _lane_tile(alpha_ref[hh], 2)
            acc_ref[hh] = alpha * acc_ref[hh] + jnp.dot(
                p_ref[hh, :, 0:width], vaug_ref[hh, ks, :], preferred_element_type=F32)

    n_full = (i * bq) // bk

    def full_body(j, carry):
        chunk(j * bk, bk, False)
        return carry
    lax.fori_loop(0, n_full, full_body, 0)

    tail_start = n_full * bk
    tail_blocks = (i + 1) - n_full * (bk // bq)
    for t in range(1, bk // bq + 1):
        @pl.when(tail_blocks == t)
        def _tail(t=t):
            chunk(tail_start, t * bq, True)

    for hh in range(FOX_PAIR):
        acc = acc_ref[hh]
        o_ref[:, hh * dh:(hh + 1) * dh] = (acc[:, 0:dh] / acc[:, dh:2 * dh]).astype(o_ref.dtype)


def _fox(proj, c2, *, bq, bk):
    S = proj.shape[0]
    dh = FOX_HEAD_DIM
    w = FOX_PAIR * dh
    return pl.pallas_call(
        functools.partial(_fox_kernel, bk=bk),
        name="fox",
        grid=(FOX_HEADS // FOX_PAIR, S // bq),
        in_specs=[
            pl.BlockSpec((bq, w), lambda h, i: (i, OFF_FQ // w + h)),
            pl.BlockSpec((S, w), lambda h, i: (0, OFF_FK // w + h)),
            pl.BlockSpec((S, w), lambda h, i: (0, OFF_FV // w + h)),
            pl.BlockSpec((S, LANES), lambda h, i: (0, 0)),
        ],
        out_specs=pl.BlockSpec((bq, w), lambda h, i: (i, h)),
        out_shape=jax.ShapeDtypeStruct((S, FOX_WIDTH), BF16),
        scratch_shapes=[
            pltpu.VMEM((FOX_PAIR, S, 2 * dh), BF16),
            pltpu.VMEM((FOX_PAIR, S, 2 * dh), BF16),
            pltpu.VMEM((FOX_PAIR, bq, 2 * dh), BF16),
            pltpu.VMEM((FOX_PAIR, bq, bk), F32),
            pltpu.VMEM((FOX_PAIR, bq, bk), BF16),
            pltpu.VMEM((FOX_PAIR, bq, LANES), F32),
            pltpu.VMEM((FOX_PAIR, bq, LANES), F32),
            pltpu.VMEM((FOX_PAIR, bq, 2 * dh), F32),
        ],
        compiler_params=_params("parallel", "arbitrary"),
    )(proj, proj, proj, c2)


def _gla_kernel(q_ref, k_ref, v_ref, gr_ref, small_ref, wgate_ref, bgate_ref,
                gnorm_ref, tri_ref, o_ref, st_ref):
    t = pl.program_id(1)
    T = q_ref.shape[0]
    C = GLA_CHUNK

    @pl.when(t == 0)
    def _init():
        st_ref[...] = jnp.zeros_like(st_ref)

    dk = q_ref.shape[1]
    gate = jnp.dot(small_ref[...].astype(BF16), wgate_ref[...],
                   preferred_element_type=F32) + bgate_ref[...]
    log_a = _log_sigmoid(gate) * (1.0 / GLA_TAU)

    pieces = jnp.concatenate([x.astype(BF16) for x in _split3(log_a)], axis=1)
    sums = jnp.dot(tri_ref[...].astype(BF16), pieces, preferred_element_type=F32)
    b = sums[:, 0:dk] + sums[:, dk:2 * dk] + sums[:, 2 * dk:3 * dk]
    b_tot = jnp.concatenate(
        [jnp.broadcast_to(b[c * C + C - 1:c * C + C, :], (C, dk)) for c in range(T // C)], axis=0)
    causal = tri_ref[...] > 0.0

    q = q_ref[...].astype(F32)
    k = k_ref[...].astype(F32)
    v = v_ref[...]
    q_dec = (q * jnp.exp(b)).astype(BF16)
    k_inv = (k * jnp.exp(-b)).astype(BF16)
    k_end = (k * jnp.exp(b_tot - b)).astype(BF16)
    a = lax.dot_general(q_dec, k_inv, (((1,), (1,)), ((), ())), preferred_element_type=F32)
    a = jnp.where(causal, a, 0.0).astype(BF16)
    o_intra = jnp.dot(a, v, preferred_element_type=F32)

    for ci in range(T // C):
        lo, hi = ci * C, (ci + 1) * C
        st = st_ref[...]
        o_c = o_intra[lo:hi, :] + lax.dot_general(
            q_dec[lo:hi, :], st.astype(BF16), (((1,), (1,)), ((), ())),
            preferred_element_type=F32)
        decay = jnp.exp(b_tot[lo:lo + 1, :])
        st_ref[...] = st * decay + lax.dot_general(
            v[lo:hi, :], k_end[lo:hi, :], (((0,), (0,)), ((), ())),
            preferred_element_type=F32)
        y = _rms(o_c, gnorm_ref[...])
        o_ref[lo:hi, :] = (y * _silu(gr_ref[lo:hi, :].astype(F32))).astype(o_ref.dtype)


def _chunk_causal_mask(n):
    r = jnp.arange(n)[:, None]
    c = jnp.arange(n)[None, :]
    return ((r // GLA_CHUNK == c // GLA_CHUNK) & (r >= c)).astype(F32)


def _gla(proj, small, wgate_pad, bgate, gnorm, *, bt):
    S = proj.shape[0]
    dk, dv = GLA_HEAD_K, GLA_HEAD_V
    return pl.pallas_call(
        _gla_kernel,
        name="gla",
        grid=(GLA_HEADS, S // bt),
        in_specs=[
            pl.BlockSpec((bt, dk), lambda h, t: (t, OFF_GQ // dk + h)),
            pl.BlockSpec((bt, dk), lambda h, t: (t, OFF_GK // dk + h)),
            pl.BlockSpec((bt, dv), lambda h, t: (t, OFF_GV // dv + h)),
            pl.BlockSpec((bt, dv), lambda h, t: (t, OFF_GR // dv + h)),
            pl.BlockSpec((bt, LANES), lambda h, t: (t, 0)),
            pl.BlockSpec((LANES, dk), lambda h, t: (0, h)),
            pl.BlockSpec((1, dk), lambda h, t: (0, h)),
            pl.BlockSpec((1, dv), lambda h, t: (0, 0)),
            pl.BlockSpec((bt, bt), lambda h, t: (0, 0)),
        ],
        out_specs=pl.BlockSpec((bt, dv), lambda h, t: (t, h)),
        out_shape=jax.ShapeDtypeStruct((S, GLA_WIDTH), BF16),
        scratch_shapes=[pltpu.VMEM((dv, dk), F32)],
        compiler_params=_params("parallel", "arbitrary"),
    )(proj, proj, proj, proj, small, wgate_pad, bgate, gnorm, _chunk_causal_mask(bt))


def _outproj_kernel(fox_ref, gla_ref, wt_ref, wb_ref, h_ref, g_ref, o_ref):
    n = pl.program_id(1)
    bn = wt_ref.shape[1]
    for sub in _sub_tiles(bn):
        cols = pl.ds(pl.multiple_of(n * bn + sub.start, MXU_COLS), MXU_COLS)
        o_ref[:, cols] = (jnp.dot(fox_ref[...], wt_ref[:, sub], preferred_element_type=F32)
                          + jnp.dot(gla_ref[...], wb_ref[:, sub], preferred_element_type=F32))

    @pl.when(n == pl.num_programs(1) - 1)
    def _epilogue():
        def rows_fn(rows):
            o_ref[rows, :] = h_ref[rows, :] + _rms(o_ref[rows, :], g_ref[...])
        _for_row_chunks(h_ref.shape[0], rows_fn)


def _outproj(o_fox, o_gla, w_o, h, g, *, bm, bn):
    S, D = h.shape
    half = o_fox.shape[1]
    return pl.pallas_call(
        _outproj_kernel,
        name="outproj",
        grid=(S // bm, D // bn),
        in_specs=[
            pl.BlockSpec((bm, half), lambda i, n: (i, 0)),
            pl.BlockSpec((bm, half), lambda i, n: (i, 0)),
            pl.BlockSpec((half, bn), lambda i, n: (0, n)),
            pl.BlockSpec((half, bn), lambda i, n: (1, n)),
            pl.BlockSpec((bm, D), lambda i, n: (i, 0)),
            pl.BlockSpec((1, D), lambda i, n: (0, 0)),
        ],
        out_specs=pl.BlockSpec((bm, D), lambda i, n: (i, 0)),
        out_shape=jax.ShapeDtypeStruct((S, D), F32),
        compiler_params=_params("parallel", "arbitrary"),
    )(o_fox, o_gla, w_o, w_o, h, g)


def _ple_kernel(h_ref, p_ref, wp_ref, g_ref, wg_ref, o_ref, hb_ref, e_ref):
    n = pl.program_id(1)
    bn = wg_ref.shape[1]

    @pl.when(n == 0)
    def _prologue():
        e_ref[...] = jnp.dot(p_ref[...].astype(BF16), wp_ref[...], preferred_element_type=F32)

        def rows_fn(rows):
            hb_ref[rows, :] = h_ref[rows, :].astype(BF16)
            e_ref[rows, :] = _rms(e_ref[rows, :], g_ref[...])
        _for_row_chunks(h_ref.shape[0], rows_fn)

    for sub in _sub_tiles(bn):
        cols = pl.ds(pl.multiple_of(n * bn + sub.start, MXU_COLS), MXU_COLS)
        gate = jax.nn.sigmoid(jnp.dot(hb_ref[...], wg_ref[:, sub], preferred_element_type=F32))
        o_ref[:, sub] = h_ref[:, cols] + e_ref[:, cols] * gate


def _ple(h, p, w_proj, g, w_gate, *, bm, bn):
    S, D = h.shape
    dp = p.shape[1]
    return pl.pallas_call(
        _ple_kernel,
        name="ple",
        grid=(S // bm, D // bn),
        in_specs=[
            pl.BlockSpec((bm, D), lambda i, n: (i, 0)),
            pl.BlockSpec((bm, dp), lambda i, n: (i, 0)),
            pl.BlockSpec((dp, D), lambda i, n: (0, 0)),
            pl.BlockSpec((1, D), lambda i, n: (0, 0)),
            pl.BlockSpec((D, bn), lambda i, n: (0, n)),
        ],
        out_specs=pl.BlockSpec((bm, bn), lambda i, n: (i, n)),
        out_shape=jax.ShapeDtypeStruct((S, D), F32),
        scratch_shapes=[pltpu.VMEM((bm, D), BF16), pltpu.VMEM((bm, D), F32)],
        compiler_params=_params("parallel", "arbitrary"),
    )(h, p, w_proj, g, w_gate)


def _tile(n, preferred):
    t = min(n, preferred)
    while n % t:
        t //= 2
    return t


def _split_w_in(w_in):
    D = w_in.shape[0]
    ff_lo = 3 * FOX_WIDTH
    ff_hi = ff_lo + FOX_HEADS
    glr_lo = ff_hi + 2 * GLA_KEY_WIDTH + 2 * GLA_WIDTH
    glr_hi = glr_lo + GLA_GATE_RANK
    main = jnp.concatenate([w_in[:, :ff_lo].astype(BF16), w_in[:, ff_hi:glr_lo].astype(BF16)], axis=1)
    pad = jnp.zeros((D, LANES - FOX_HEADS - GLA_GATE_RANK), BF16)
    small = jnp.concatenate([w_in[:, ff_lo:ff_hi].astype(BF16), w_in[:, glr_lo:glr_hi].astype(BF16), pad],
                            axis=1)
    return main, small


def kernel(x, p, ffn1_norm_pre, ffn1_w_gate, ffn1_w_up, ffn1_w_down, ffn1_norm_post, mix_norm_pre, w_in, fox_b_f, gla_w_gate, gla_b_gate, gla_norm_g, w_o, mix_norm_post, ffn2_norm_pre, ffn2_w_gate, ffn2_w_up, ffn2_w_down, ffn2_norm_post, ple_w_proj, ple_norm, ple_w_gate):
    B, S, D = x.shape
    assert B == 1, "the attention kernels treat the row axis as one sequence"
    depth = w_in.shape[0]
    h = x.reshape(S, D)

    bm = _tile(S, 512)
    bf = _tile(ffn1_w_gate.shape[-1], 256)
    bq = _tile(S, 512)
    bk = max(bq, _tile(S, 1024))
    bt = _tile(S, 512)

    col_scale = jnp.ones((1, MAIN_COLS), F32)
    col_scale = col_scale.at[:, OFF_FQ:OFF_FQ + FOX_WIDTH].set(FOX_HEAD_DIM ** -0.5 * LOG2_E)
    col_scale = col_scale.at[:, OFF_GQ:OFF_GQ + GLA_KEY_WIDTH].set(GLA_HEAD_K ** -0.5)

    def row(v):
        return v.reshape(1, -1).astype(F32)

    for i in range(depth):
        h = _ffn(h, row(ffn1_norm_pre[i]), ffn1_w_gate[i].astype(BF16), ffn1_w_up[i].astype(BF16),
                 ffn1_w_down[i].astype(BF16), row(ffn1_norm_post[i]), bm=bm, bf=bf)

        w_main, w_small = _split_w_in(w_in[i])
        proj, small = _inproj(h, row(mix_norm_pre[i]), w_main, w_small, col_scale, bm=bm, bn=1024)

        bias_f = jnp.zeros((1, LANES), F32).at[0, :FOX_HEADS].set(fox_b_f[i])
        c2 = _fox_gate(small, bias_f, bt=bt)
        o_fox = _fox(proj, c2, bq=bq, bk=bk)

        wgate_pad = jnp.zeros((LANES, GLA_KEY_WIDTH), F32).at[
            FOX_HEADS:FOX_HEADS + GLA_GATE_RANK, :].set(gla_w_gate[i]).astype(BF16)
        o_gla = _gla(proj, small, wgate_pad, row(gla_b_gate[i]), row(gla_norm_g[i]), bt=bt)

        h = _outproj(o_fox, o_gla, w_o[i].astype(BF16), h, row(mix_norm_post[i]), bm=bm, bn=512)

        h = _ffn(h, row(ffn2_norm_pre[i]), ffn2_w_gate[i].astype(BF16), ffn2_w_up[i].astype(BF16),
                 ffn2_w_down[i].astype(BF16), row(ffn2_norm_post[i]), bm=bm, bf=bf)

        h = _ple(h, p[i].reshape(S, -1), ple_w_proj[i].astype(BF16), row(ple_norm[i]),
                 ple_w_gate[i].astype(BF16), bm=bm, bn=1024)
    return h.reshape(B, S, D)
```

```python
import functools

import jax
import jax.numpy as jnp
from jax import lax
from jax.experimental import pallas as pl
from jax.experimental.pallas import tpu as pltpu

F32 = jnp.float32
BF16 = jnp.bfloat16
HIGHEST = lax.Precision.HIGHEST

EPS = 1e-6
LOG2_E = 1.4426950408889634
MACARON_WEIGHT = 0.5
FOX_HEAD_DIM = 128
FOX_HEADS = 16
FOX_WIDTH = FOX_HEADS * FOX_HEAD_DIM
GLA_HEADS = 4
GLA_WIDTH = 2048
GLA_HEAD_V = GLA_WIDTH // GLA_HEADS
GLA_KEY_WIDTH = GLA_WIDTH // 2
GLA_HEAD_K = GLA_KEY_WIDTH // GLA_HEADS
GLA_GATE_RANK = 16
GLA_TAU = 16.0
GLA_CHUNK = 64

LANES = 128
MXU_COLS = 256
VMEM_LIMIT_BYTES = 56 * 1024 * 1024

MAIN_COLS = 3 * FOX_WIDTH + 2 * GLA_WIDTH + 2 * GLA_KEY_WIDTH
OFF_FQ = 0
OFF_FK = FOX_WIDTH
OFF_FV = 2 * FOX_WIDTH
OFF_GQ = 3 * FOX_WIDTH
OFF_GK = OFF_GQ + GLA_KEY_WIDTH
OFF_GV = OFF_GK + GLA_KEY_WIDTH
OFF_GR = OFF_GV + GLA_WIDTH

ROW_CHUNK = 32


def _params(*semantics):
    return pltpu.CompilerParams(dimension_semantics=semantics,
                                vmem_limit_bytes=VMEM_LIMIT_BYTES)


def _rms(x, g):
    ms = jnp.mean(x * x, axis=-1, keepdims=True)
    return x * lax.rsqrt(ms + EPS) * g


def _log_sigmoid(x):
    return jnp.minimum(x, 0.0) - jnp.log1p(jnp.exp(-jnp.abs(x)))


def _silu(x):
    return x * jax.nn.sigmoid(x)


def _sub_tiles(n_cols):
    return [slice(c, c + MXU_COLS) for c in range(0, n_cols, MXU_COLS)]


def _for_row_chunks(n_rows, fn):
    def body(r, carry):
        fn(pl.ds(pl.multiple_of(r * ROW_CHUNK, ROW_CHUNK), ROW_CHUNK))
        return carry
    lax.fori_loop(0, n_rows // ROW_CHUNK, body, 0)


def _ffn_prologue(x_ref, gpre_ref, xn_ref, o_ref):
    def rows_fn(rows):
        xn_ref[rows, :] = _rms(x_ref[rows, :], gpre_ref[...]).astype(BF16)
        o_ref[rows, :] = jnp.zeros((ROW_CHUNK, o_ref.shape[1]), F32)
    _for_row_chunks(x_ref.shape[0], rows_fn)


def _ffn_epilogue(x_ref, gpost_ref, o_ref):
    def rows_fn(rows):
        y = _rms(o_ref[rows, :], gpost_ref[...])
        o_ref[rows, :] = x_ref[rows, :] + MACARON_WEIGHT * y
    _for_row_chunks(x_ref.shape[0], rows_fn)


def _ffn_kernel(x_ref, gpre_ref, wg_ref, wu_ref, wd_ref, gpost_ref, prev_ref, o_ref, xn_ref):
    del prev_ref
    f = pl.program_id(1)

    @pl.when(f == 0)
    def _():
        _ffn_prologue(x_ref, gpre_ref, xn_ref, o_ref)

    xn = xn_ref[...]
    g = jnp.dot(xn, wg_ref[...], preferred_element_type=F32)
    u = jnp.dot(xn, wu_ref[...], preferred_element_type=F32)
    hid = (_silu(g) * u).astype(BF16)
    o_ref[...] += jnp.dot(hid, wd_ref[...], preferred_element_type=F32)

    @pl.when(f == pl.num_programs(1) - 1)
    def _():
        _ffn_epilogue(x_ref, gpost_ref, o_ref)


def _ffn_first_kernel(x_ref, gpre_ref, wg_ref, wu_ref, wd_ref, gpost_ref,
                      o_ref, wg16_ref, wu16_ref, wd16_ref, xn_ref):
    f = pl.program_id(0)
    bf = wg_ref.shape[1]

    @pl.when(f == 0)
    def _():
        _ffn_prologue(x_ref, gpre_ref, xn_ref, o_ref)

    wg = wg_ref[...].astype(BF16)
    wu = wu_ref[...].astype(BF16)
    wd = wd_ref[...].astype(BF16)
    wg16_ref[...] = wg
    wu16_ref[...] = wu
    wd16_ref[...] = wd
    xn = xn_ref[...]
    g = jnp.dot(xn, wg, preferred_element_type=F32)
    u = jnp.dot(xn, wu, preferred_element_type=F32)
    hid = (_silu(g) * u).astype(BF16)
    o_ref[...] += jnp.dot(hid, wd, preferred_element_type=F32)

    @pl.when(f == pl.num_programs(0) - 1)
    def _():
        _ffn_epilogue(x_ref, gpost_ref, o_ref)


def _ffn(x, g_pre, wg, wu, wd, g_post, *, bm, bf):
    S, D = x.shape
    d_ff = wg.shape[1]
    bf1 = MXU_COLS // 2
    once = dict(pipeline_mode=pl.Buffered(1))
    h, wg16, wu16, wd16 = pl.pallas_call(
        _ffn_first_kernel,
        name="ffn_first",
        grid=(d_ff // bf1,),
        in_specs=[
            pl.BlockSpec((bm, D), lambda f: (0, 0), **once),
            pl.BlockSpec((1, D), lambda f: (0, 0)),
            pl.BlockSpec((D, bf1), lambda f: (0, f)),
            pl.BlockSpec((D, bf1), lambda f: (0, f)),
            pl.BlockSpec((bf1, D), lambda f: (f, 0)),
            pl.BlockSpec((1, D), lambda f: (0, 0)),
        ],
        out_specs=[
            pl.BlockSpec((bm, D), lambda f: (0, 0)),
            pl.BlockSpec((D, bf1), lambda f: (0, f)),
            pl.BlockSpec((D, bf1), lambda f: (0, f)),
            pl.BlockSpec((bf1, D), lambda f: (f, 0)),
        ],
        out_shape=[
            jax.ShapeDtypeStruct((S, D), F32),
            jax.ShapeDtypeStruct((D, d_ff), BF16),
            jax.ShapeDtypeStruct((D, d_ff), BF16),
            jax.ShapeDtypeStruct((d_ff, D), BF16),
        ],
        scratch_shapes=[pltpu.VMEM((bm, D), BF16)],
        compiler_params=_params("arbitrary"),
    )(x, g_pre, wg, wu, wd, g_post)
    if S == bm:
        return h
    return pl.pallas_call(
        _ffn_kernel,
        name="ffn",
        grid=(S // bm - 1, d_ff // bf),
        in_specs=[
            pl.BlockSpec((bm, D), lambda i, f: (i + 1, 0)),
            pl.BlockSpec((1, D), lambda i, f: (0, 0)),
            pl.BlockSpec((D, bf), lambda i, f: (0, f)),
            pl.BlockSpec((D, bf), lambda i, f: (0, f)),
            pl.BlockSpec((bf, D), lambda i, f: (f, 0)),
            pl.BlockSpec((1, D), lambda i, f: (0, 0)),
            pl.BlockSpec(memory_space=pl.ANY),
        ],
        out_specs=pl.BlockSpec((bm, D), lambda i, f: (i + 1, 0)),
        out_shape=jax.ShapeDtypeStruct((S, D), F32),
        input_output_aliases={6: 0},
        scratch_shapes=[pltpu.VMEM((bm, D), BF16)],
        compiler_params=_params("parallel", "arbitrary"),
    )(x, g_pre, wg16, wu16, wd16, g_post, h)


def _inproj_kernel(h_ref, g_ref, w1_ref, w2_ref, ws_ref, scale_ref, o_ref, os_ref, a_ref, *, n1):
    n = pl.program_id(1)

    @pl.when(n == 0)
    def _prologue():
        def rows_fn(rows):
            a_ref[rows, :] = _rms(h_ref[rows, :], g_ref[...]).astype(BF16)
        _for_row_chunks(h_ref.shape[0], rows_fn)
        os_ref[...] = jnp.dot(a_ref[...], ws_ref[...], preferred_element_type=F32)

    def tile(w_ref):
        for cols in _sub_tiles(w_ref.shape[1]):
            acc = jnp.dot(a_ref[...], w_ref[:, cols], preferred_element_type=F32)
            o_ref[:, cols] = (acc * scale_ref[:, cols]).astype(BF16)

    @pl.when(n < n1)
    def _():
        tile(w1_ref)

    @pl.when(n >= n1)
    def _():
        tile(w2_ref)


def _inproj(h, g, w_all, w_rest, w_small, col_scale, *, bm, bn):
    S, D = h.shape
    n_cols = col_scale.shape[1]
    n1 = (n_cols - w_rest.shape[1]) // bn
    return pl.pallas_call(
        functools.partial(_inproj_kernel, n1=n1),
        name="inproj",
        grid=(S // bm, n_cols // bn),
        in_specs=[
            pl.BlockSpec((bm, D), lambda i, n: (i, 0)),
            pl.BlockSpec((1, D), lambda i, n: (0, 0)),
            pl.BlockSpec((D, bn), lambda i, n: (0, jnp.minimum(n, n1 - 1))),
            pl.BlockSpec((D, bn), lambda i, n: (0, jnp.maximum(n - n1, 0))),
            pl.BlockSpec((D, LANES), lambda i, n: (0, 0)),
            pl.BlockSpec((1, bn), lambda i, n: (0, n)),
        ],
        out_specs=[
            pl.BlockSpec((bm, bn), lambda i, n: (i, n)),
            pl.BlockSpec((bm, LANES), lambda i, n: (i, 0)),
        ],
        out_shape=[
            jax.ShapeDtypeStruct((S, n_cols), BF16),
            jax.ShapeDtypeStruct((S, LANES), F32),
        ],
        scratch_shapes=[pltpu.VMEM((bm, D), BF16)],
        compiler_params=_params("parallel", "arbitrary"),
    )(h, g, w_all, w_rest, w_small, col_scale)


def _fox_gate_kernel(small_ref, bias_ref, c_ref, carry_ref):
    t = pl.program_id(0)
    T = small_ref.shape[0]

    @pl.when(t == 0)
    def _init():
        carry_ref[...] = jnp.zeros_like(carry_ref)

    log_f = _log_sigmoid(small_ref[...] + bias_ref[...])
    row = lax.broadcasted_iota(jnp.int32, (T, T), 0)
    col = lax.broadcasted_iota(jnp.int32, (T, T), 1)
    tri = jnp.where(row >= col, 1.0, 0.0).astype(F32)
    c = jnp.dot(tri, log_f, precision=HIGHEST, preferred_element_type=F32) + carry_ref[...]
    c_ref[...] = c * LOG2_E
    carry_ref[...] = c[T - 1:T, :]


def _fox_gate(small, bias, *, bt):
    S = small.shape[0]
    return pl.pallas_call(
        _fox_gate_kernel,
        name="fox_gate",
        grid=(S // bt,),
        in_specs=[
            pl.BlockSpec((bt, LANES), lambda t: (t, 0)),
            pl.BlockSpec((1, LANES), lambda t: (0, 0)),
        ],
        out_specs=pl.BlockSpec((bt, LANES), lambda t: (t, 0)),
        out_shape=jax.ShapeDtypeStruct((S, LANES), F32),
        scratch_shapes=[pltpu.VMEM((1, LANES), F32)],
        compiler_params=_params("arbitrary"),
    )(small, bias)


FOX_PAIR = 2
FOX_ROWS = 16


def _split3(c):
    hi = c.astype(BF16).astype(F32)
    mid = (c - hi).astype(BF16).astype(F32)
    lo = (c - hi - mid).astype(BF16).astype(F32)
    return hi, mid, lo


def _lane_tile(x, n):
    return jnp.concatenate([x] * n, axis=1)


def _head_column(block, head):
    lane = lax.broadcasted_iota(jnp.int32, block.shape, 1)
    return jnp.sum(jnp.where(lane == head, block, 0.0), axis=1, keepdims=True)


def _fox_kernel(q_ref, k_ref, v_ref, c_ref, o_ref, kaug_ref, vaug_ref, qaug_ref, s_ref, p_ref,
                m_ref, alpha_ref, acc_ref, *, bk):
    hp = pl.program_id(0)
    i = pl.program_id(1)
    bq = q_ref.shape[0]
    S = k_ref.shape[0]
    dh = FOX_HEAD_DIM
    R = FOX_ROWS

    @pl.when(i == 0)
    def _build_kv_side():
        rows_per = 128

        def body(r, carry):
            rows = pl.ds(pl.multiple_of(r * rows_per, rows_per), rows_per)
            cblk = c_ref[rows, :]
            lane = lax.broadcasted_iota(jnp.int32, (rows_per, LANES), 1)
            for hh in range(FOX_PAIR):
                hi, mid, lo = _split3(_head_column(cblk, hp * FOX_PAIR + hh))
                extra = jnp.where(lane < 3, 1.0,
                                  jnp.where(lane == 3, -hi,
                                            jnp.where(lane == 4, -mid,
                                                      jnp.where(lane == 5, -lo, 0.0))))
                kaug_ref[hh, rows, 0:dh] = k_ref[rows, hh * dh:(hh + 1) * dh]
                kaug_ref[hh, rows, dh:2 * dh] = extra.astype(BF16)
                vaug_ref[hh, rows, 0:dh] = v_ref[rows, hh * dh:(hh + 1) * dh]
                vaug_ref[hh, rows, dh:2 * dh] = jnp.ones((rows_per, dh), BF16)
            return carry
        lax.fori_loop(0, S // rows_per, body, 0)

    cq_blk = c_ref[pl.ds(pl.multiple_of(i * bq, bq), bq), :]
    lane_q = lax.broadcasted_iota(jnp.int32, (bq, LANES), 1)
    for hh in range(FOX_PAIR):
        hi, mid, lo = _split3(_head_column(cq_blk, hp * FOX_PAIR + hh))
        extra = jnp.where(lane_q == 0, hi,
                          jnp.where(lane_q == 1, mid,
                                    jnp.where(lane_q == 2, lo,
                                              jnp.where(lane_q < 6, 1.0, 0.0))))
        qaug_ref[hh, :, 0:dh] = q_ref[:, hh * dh:(hh + 1) * dh]
        qaug_ref[hh, :, dh:2 * dh] = extra.astype(BF16)
        m_ref[hh] = jnp.full((bq, LANES), -jnp.inf, F32)
        acc_ref[hh] = jnp.zeros((bq, 2 * dh), F32)

    def chunk(k0, width, masked):
        ks = pl.ds(pl.multiple_of(k0, bq), width)
        groups = [slice(g * R, (g + 1) * R) for g in range(bq // R)]
        for hh in range(FOX_PAIR):
            s_ref[hh, :, 0:width] = lax.dot_general(
                qaug_ref[hh], kaug_ref[hh, ks, :], (((1,), (1,)), ((), ())),
                preferred_element_type=F32)
        if masked:
            diff = (lax.broadcasted_iota(jnp.int32, (R, width), 1)
                    - lax.broadcasted_iota(jnp.int32, (R, width), 0))
            limit = i * bq - k0
        for hh in range(FOX_PAIR):
            for g, rows in enumerate(groups):
                s = s_ref[hh, rows, 0:width]
                if masked:
                    s = jnp.where(diff <= limit + g * R, s, -jnp.inf)
                    s_ref[hh, rows, 0:width] = s
                m_old = m_ref[hh, rows, :]
                m_new = jnp.maximum(m_old, jnp.broadcast_to(
                    jnp.max(s, axis=1, keepdims=True), (R, LANES)))
                alpha_ref[hh, rows, :] = jnp.exp2(m_old - m_new)
                m_ref[hh, rows, :] = m_new
        for hh in range(FOX_PAIR):
            for rows in groups:
                m = _lane_tile(m_ref[hh, rows, :], width // LANES)
                p_ref[hh, rows, 0:width] = jnp.exp2(s_ref[hh, rows, 0:width] - m).astype(BF16)
        for hh in range(FOX_PAIR):
            alpha = _lane_tile(alpha_ref[hh], 2)
            acc_ref[hh] = alpha * acc_ref[hh] + jnp.dot(
                p_ref[hh, :, 0:width], vaug_ref[hh, ks, :], preferred_element_type=F32)

    n_full = (i * bq) // bk

    def full_body(j, carry):
        chunk(j * bk, bk, False)
        return carry
    lax.fori_loop(0, n_full, full_body, 0)

    tail_start = n_full * bk
    tail_blocks = (i + 1) - n_full * (bk // bq)
    for t in range(1, bk // bq + 1):
        @pl.when(tail_blocks == t)
        def _tail(t=t):
            chunk(tail_start, t * bq, True)

    for hh in range(FOX_PAIR):
        acc = acc_ref[hh]
        o_ref[:, hh * dh:(hh + 1) * dh] = (acc[:, 0:dh] / acc[:, dh:2 * dh]).astype(o_ref.dtype)


def _fox(proj, c2, *, bq, bk):
    S = proj.shape[0]
    dh = FOX_HEAD_DIM
    w = FOX_PAIR * dh
    return pl.pallas_call(
        functools.partial(_fox_kernel, bk=bk),
        name="fox",
        grid=(FOX_HEADS // FOX_PAIR, S // bq),
        in_specs=[
            pl.BlockSpec((bq, w), lambda h, i: (i, OFF_FQ // w + h)),
            pl.BlockSpec((S, w), lambda h, i: (0, OFF_FK // w + h)),
            pl.BlockSpec((S, w), lambda h, i: (0, OFF_FV // w + h)),
            pl.BlockSpec((S, LANES), lambda h, i: (0, 0)),
        ],
        out_specs=pl.BlockSpec((bq, w), lambda h, i: (i, h)),
        out_shape=jax.ShapeDtypeStruct((S, FOX_WIDTH), BF16),
        scratch_shapes=[
            pltpu.VMEM((FOX_PAIR, S, 2 * dh), BF16),
            pltpu.VMEM((FOX_PAIR, S, 2 * dh), BF16),
            pltpu.VMEM((FOX_PAIR, bq, 2 * dh), BF16),
            pltpu.VMEM((FOX_PAIR, bq, bk), F32),
            pltpu.VMEM((FOX_PAIR, bq, bk), BF16),
            pltpu.VMEM((FOX_PAIR, bq, LANES), F32),
            pltpu.VMEM((FOX_PAIR, bq, LANES), F32),
            pltpu.VMEM((FOX_PAIR, bq, 2 * dh), F32),
        ],
        compiler_params=_params("parallel", "arbitrary"),
    )(proj, proj, proj, c2)


def _gla_kernel(q_ref, k_ref, v_ref, gr_ref, small_ref, wgate_ref, bgate_ref,
                gnorm_ref, tri_ref, o_ref, st_ref):
    t = pl.program_id(1)
    T = q_ref.shape[0]
    C = GLA_CHUNK

    @pl.when(t == 0)
    def _init():
        st_ref[...] = jnp.zeros_like(st_ref)

    dk = q_ref.shape[1]
    gate = jnp.dot(small_ref[...].astype(BF16), wgate_ref[...],
                   preferred_element_type=F32) + bgate_ref[...]
    log_a = _log_sigmoid(gate) * (1.0 / GLA_TAU)

    pieces = jnp.concatenate([x.astype(BF16) for x in _split3(log_a)], axis=1)
    sums = jnp.dot(tri_ref[...].astype(BF16), pieces, preferred_element_type=F32)
    b = sums[:, 0:dk] + sums[:, dk:2 * dk] + sums[:, 2 * dk:3 * dk]
    b_tot = jnp.concatenate(
        [jnp.broadcast_to(b[c * C + C - 1:c * C + C, :], (C, dk)) for c in range(T // C)], axis=0)
    causal = tri_ref[...] > 0.0

    q = q_ref[...].astype(F32)
    k = k_ref[...].astype(F32)
    v = v_ref[...]
    q_dec = (q * jnp.exp(b)).astype(BF16)
    k_inv = (k * jnp.exp(-b)).astype(BF16)
    k_end = (k * jnp.exp(b_tot - b)).astype(BF16)
    a = lax.dot_general(q_dec, k_inv, (((1,), (1,)), ((), ())), preferred_element_type=F32)
    a = jnp.where(causal, a, 0.0).astype(BF16)
    o_intra = jnp.dot(a, v, preferred_element_type=F32)

    for ci in range(T // C):
        lo, hi = ci * C, (ci + 1) * C
        st = st_ref[...]
        o_c = o_intra[lo:hi, :] + lax.dot_general(
            q_dec[lo:hi, :], st.astype(BF16), (((1,), (1,)), ((), ())),
            preferred_element_type=F32)
        decay = jnp.exp(b_tot[lo:lo + 1, :])
        st_ref[...] = st * decay + lax.dot_general(
            v[lo:hi, :], k_end[lo:hi, :], (((0,), (0,)), ((), ())),
            preferred_element_type=F32)
        y = _rms(o_c, gnorm_ref[...])
        o_ref[lo:hi, :] = (y * _silu(gr_ref[lo:hi, :].astype(F32))).astype(o_ref.dtype)


def _chunk_causal_mask(n):
    r = jnp.arange(n)[:, None]
    c = jnp.arange(n)[None, :]
    return ((r // GLA_CHUNK == c // GLA_CHUNK) & (r >= c)).astype(F32)


def _gla(proj, small, wgate_pad, bgate, gnorm, *, bt):
    S = proj.shape[0]
    dk, dv = GLA_HEAD_K, GLA_HEAD_V
    return pl.pallas_call(
        _gla_kernel,
        name="gla",
        grid=(GLA_HEADS, S // bt),
        in_specs=[
            pl.BlockSpec((bt, dk), lambda h, t: (t, OFF_GQ // dk + h)),
            pl.BlockSpec((bt, dk), lambda h, t: (t, OFF_GK // dk + h)),
            pl.BlockSpec((bt, dv), lambda h, t: (t, OFF_GV // dv + h)),
            pl.BlockSpec((bt, dv), lambda h, t: (t, OFF_GR // dv + h)),
            pl.BlockSpec((bt, LANES), lambda h, t: (t, 0)),
            pl.BlockSpec((LANES, dk), lambda h, t: (0, h)),
            pl.BlockSpec((1, dk), lambda h, t: (0, h)),
            pl.BlockSpec((1, dv), lambda h, t: (0, 0)),
            pl.BlockSpec((bt, bt), lambda h, t: (0, 0)),
        ],
        out_specs=pl.BlockSpec((bt, dv), lambda h, t: (t, h)),
        out_shape=jax.ShapeDtypeStruct((S, GLA_WIDTH), BF16),
        scratch_shapes=[pltpu.VMEM((dv, dk), F32)],
        compiler_params=_params("parallel", "arbitrary"),
    )(proj, proj, proj, proj, small, wgate_pad, bgate, gnorm, _chunk_causal_mask(bt))


def _outproj_kernel(fox_ref, gla_ref, wt_ref, wb_ref, h_ref, g_ref, o_ref):
    n = pl.program_id(1)
    bn = wt_ref.shape[1]
    for sub in _sub_tiles(bn):
        cols = pl.ds(pl.multiple_of(n * bn + sub.start, MXU_COLS), MXU_COLS)
        o_ref[:, cols] = (jnp.dot(fox_ref[...], wt_ref[:, sub], preferred_element_type=F32)
                          + jnp.dot(gla_ref[...], wb_ref[:, sub], preferred_element_type=F32))

    @pl.when(n == pl.num_programs(1) - 1)
    def _epilogue():
        def rows_fn(rows):
            o_ref[rows, :] = h_ref[rows, :] + _rms(o_ref[rows, :], g_ref[...])
        _for_row_chunks(h_ref.shape[0], rows_fn)


def _outproj(o_fox, o_gla, w_o, h, g, *, bm, bn):
    S, D = h.shape
    half = o_fox.shape[1]
    return pl.pallas_call(
        _outproj_kernel,
        name="outproj",
        grid=(S // bm, D // bn),
        in_specs=[
            pl.BlockSpec((bm, half), lambda i, n: (i, 0)),
            pl.BlockSpec((bm, half), lambda i, n: (i, 0)),
            pl.BlockSpec((half, bn), lambda i, n: (0, n)),
            pl.BlockSpec((half, bn), lambda i, n: (1, n)),
            pl.BlockSpec((bm, D), lambda i, n: (i, 0)),
            pl.BlockSpec((1, D), lambda i, n: (0, 0)),
        ],
        out_specs=pl.BlockSpec((bm, D), lambda i, n: (i, 0)),
        out_shape=jax.ShapeDtypeStruct((S, D), F32),
        compiler_params=_params("parallel", "arbitrary"),
    )(o_fox, o_gla, w_o, w_o, h, g)


def _ple_kernel(h_ref, p_ref, wp_ref, g_ref, wg_ref, o_ref, hb_ref, e_ref):
    n = pl.program_id(1)
    bn = wg_ref.shape[1]

    @pl.when(n == 0)
    def _prologue():
        e_ref[...] = jnp.dot(p_ref[...].astype(BF16), wp_ref[...], preferred_element_type=F32)

        def rows_fn(rows):
            hb_ref[rows, :] = h_ref[rows, :].astype(BF16)
            e_ref[rows, :] = _rms(e_ref[rows, :], g_ref[...])
        _for_row_chunks(h_ref.shape[0], rows_fn)

    for sub in _sub_tiles(bn):
        cols = pl.ds(pl.multiple_of(n * bn + sub.start, MXU_COLS), MXU_COLS)
        gate = jax.nn.sigmoid(jnp.dot(hb_ref[...], wg_ref[:, sub], preferred_element_type=F32))
        o_ref[:, sub] = h_ref[:, cols] + e_ref[:, cols] * gate


def _ple(h, p, w_proj, g, w_gate, *, bm, bn):
    S, D = h.shape
    dp = p.shape[1]
    return pl.pallas_call(
        _ple_kernel,
        name="ple",
        grid=(S // bm, D // bn),
        in_specs=[
            pl.BlockSpec((bm, D), lambda i, n: (i, 0)),
            pl.BlockSpec((bm, dp), lambda i, n: (i, 0)),
            pl.BlockSpec((dp, D), lambda i, n: (0, 0)),
            pl.BlockSpec((1, D), lambda i, n: (0, 0)),
            pl.BlockSpec((D, bn), lambda i, n: (0, n)),
        ],
        out_specs=pl.BlockSpec((bm, bn), lambda i, n: (i, n)),
        out_shape=jax.ShapeDtypeStruct((S, D), F32),
        scratch_shapes=[pltpu.VMEM((bm, D), BF16), pltpu.VMEM((bm, D), F32)],
        compiler_params=_params("parallel", "arbitrary"),
    )(h, p, w_proj, g, w_gate)


def _tile(n, preferred):
    t = min(n, preferred)
    while n % t:
        t //= 2
    return t


def _split_w_in(w_in):
    D = w_in.shape[0]
    ff_lo = 3 * FOX_WIDTH
    ff_hi = ff_lo + FOX_HEADS
    glr_lo = ff_hi + 2 * GLA_KEY_WIDTH + 2 * GLA_WIDTH
    glr_hi = glr_lo + GLA_GATE_RANK
    w_all = w_in.astype(BF16)
    rest = w_all[:, ff_hi:glr_lo]
    pad = jnp.zeros((D, LANES - FOX_HEADS - GLA_GATE_RANK), BF16)
    small = jnp.concatenate([w_all[:, ff_lo:ff_hi], w_all[:, glr_lo:glr_hi], pad], axis=1)
    return w_all, rest, small


def kernel(x, p, ffn1_norm_pre, ffn1_w_gate, ffn1_w_up, ffn1_w_down, ffn1_norm_post, mix_norm_pre, w_in, fox_b_f, gla_w_gate, gla_b_gate, gla_norm_g, w_o, mix_norm_post, ffn2_norm_pre, ffn2_w_gate, ffn2_w_up, ffn2_w_down, ffn2_norm_post, ple_w_proj, ple_norm, ple_w_gate):
    B, S, D = x.shape
    assert B == 1, "the attention kernels treat the row axis as one sequence"
    depth = w_in.shape[0]
    h = x.reshape(S, D)

    bm = _tile(S, 512)
    bf = _tile(ffn1_w_gate.shape[-1], 256)
    bq = _tile(S, 512)
    bk = max(bq, _tile(S, 1024))
    bt = _tile(S, 512)

    col_scale = jnp.ones((1, MAIN_COLS), F32)
    col_scale = col_scale.at[:, OFF_FQ:OFF_FQ + FOX_WIDTH].set(FOX_HEAD_DIM ** -0.5 * LOG2_E)
    col_scale = col_scale.at[:, OFF_GQ:OFF_GQ + GLA_KEY_WIDTH].set(GLA_HEAD_K ** -0.5)

    def row(v):
        return v.reshape(1, -1).astype(F32)

    for i in range(depth):
        h = _ffn(h, row(ffn1_norm_pre[i]), ffn1_w_gate[i], ffn1_w_up[i], ffn1_w_down[i],
                 row(ffn1_norm_post[i]), bm=bm, bf=bf)

        w_all, w_rest, w_small = _split_w_in(w_in[i])
        proj, small = _inproj(h, row(mix_norm_pre[i]), w_all, w_rest, w_small, col_scale, bm=bm, bn=768)

        bias_f = jnp.zeros((1, LANES), F32).at[0, :FOX_HEADS].set(fox_b_f[i])
        c2 = _fox_gate(small, bias_f, bt=bt)
        o_fox = _fox(proj, c2, bq=bq, bk=bk)

        wgate_pad = jnp.zeros((LANES, GLA_KEY_WIDTH), F32).at[
            FOX_HEADS:FOX_HEADS + GLA_GATE_RANK, :].set(gla_w_gate[i]).astype(BF16)
        o_gla = _gla(proj, small, wgate_pad, row(gla_b_gate[i]), row(gla_norm_g[i]), bt=bt)

        h = _outproj(o_fox, o_gla, w_o[i].astype(BF16), h, row(mix_norm_post[i]), bm=bm, bn=512)

        h = _ffn(h, row(ffn2_norm_pre[i]), ffn2_w_gate[i], ffn2_w_up[i], ffn2_w_down[i],
                 row(ffn2_norm_post[i]), bm=bm, bf=bf)

        h = _ple(h, p[i].reshape(S, -1), ple_w_proj[i].astype(BF16), row(ple_norm[i]),
                 ple_w_gate[i].astype(BF16), bm=bm, bn=1024)
    return h.reshape(B, S, D)
```

```python
import functools

import jax
import jax.numpy as jnp
from jax import lax
from jax.experimental import pallas as pl
from jax.experimental.pallas import tpu as pltpu

F32 = jnp.float32
BF16 = jnp.bfloat16
HIGHEST = lax.Precision.HIGHEST

EPS = 1e-6
LOG2_E = 1.4426950408889634
MACARON_WEIGHT = 0.5
FOX_HEAD_DIM = 128
FOX_HEADS = 16
FOX_WIDTH = FOX_HEADS * FOX_HEAD_DIM
GLA_HEADS = 4
GLA_WIDTH = 2048
GLA_HEAD_V = GLA_WIDTH // GLA_HEADS
GLA_KEY_WIDTH = GLA_WIDTH // 2
GLA_HEAD_K = GLA_KEY_WIDTH // GLA_HEADS
GLA_GATE_RANK = 16
GLA_TAU = 16.0
GLA_CHUNK = 64

LANES = 128
MXU_COLS = 256
VMEM_LIMIT_BYTES = 56 * 1024 * 1024

MAIN_COLS = 3 * FOX_WIDTH + 2 * GLA_WIDTH + 2 * GLA_KEY_WIDTH
OFF_FQ = 0
OFF_FK = FOX_WIDTH
OFF_FV = 2 * FOX_WIDTH
OFF_GQ = 3 * FOX_WIDTH
OFF_GK = OFF_GQ + GLA_KEY_WIDTH
OFF_GV = OFF_GK + GLA_KEY_WIDTH
OFF_GR = OFF_GV + GLA_WIDTH

ROW_CHUNK = 32


def _params(*semantics):
    return pltpu.CompilerParams(dimension_semantics=semantics,
                                vmem_limit_bytes=VMEM_LIMIT_BYTES)


def _rms(x, g):
    ms = jnp.mean(x * x, axis=-1, keepdims=True)
    return x * lax.rsqrt(ms + EPS) * g


def _log_sigmoid(x):
    return jnp.minimum(x, 0.0) - jnp.log1p(jnp.exp(-jnp.abs(x)))


def _silu(x):
    return x * jax.nn.sigmoid(x)


def _sub_tiles(n_cols):
    return [slice(c, c + MXU_COLS) for c in range(0, n_cols, MXU_COLS)]


def _for_row_chunks(n_rows, fn):
    def body(r, carry):
        fn(pl.ds(pl.multiple_of(r * ROW_CHUNK, ROW_CHUNK), ROW_CHUNK))
        return carry
    lax.fori_loop(0, n_rows // ROW_CHUNK, body, 0)


def _ffn_prologue(x_ref, gpre_ref, xn_ref, o_ref):
    def rows_fn(rows):
        xn_ref[rows, :] = _rms(x_ref[rows, :], gpre_ref[...]).astype(BF16)
        o_ref[rows, :] = jnp.zeros((ROW_CHUNK, o_ref.shape[1]), F32)
    _for_row_chunks(x_ref.shape[0], rows_fn)


def _ffn_epilogue(x_ref, gpost_ref, o_ref):
    def rows_fn(rows):
        y = _rms(o_ref[rows, :], gpost_ref[...])
        o_ref[rows, :] = x_ref[rows, :] + MACARON_WEIGHT * y
    _for_row_chunks(x_ref.shape[0], rows_fn)


def _ffn_kernel(x_ref, gpre_ref, wg_ref, wu_ref, wd_ref, gpost_ref, prev_ref, o_ref, xn_ref):
    del prev_ref
    f = pl.program_id(1)

    @pl.when(f == 0)
    def _():
        _ffn_prologue(x_ref, gpre_ref, xn_ref, o_ref)

    xn = xn_ref[...]
    g = jnp.dot(xn, wg_ref[...], preferred_element_type=F32)
    u = jnp.dot(xn, wu_ref[...], preferred_element_type=F32)
    hid = (_silu(g) * u).astype(BF16)
    o_ref[...] += jnp.dot(hid, wd_ref[...], preferred_element_type=F32)

    @pl.when(f == pl.num_programs(1) - 1)
    def _():
        _ffn_epilogue(x_ref, gpost_ref, o_ref)


def _ffn_first_kernel(x_ref, gpre_ref, wg_ref, wu_ref, wd_ref, gpost_ref,
                      o_ref, wg16_ref, wu16_ref, wd16_ref, xn_ref, g_ref, u_ref):
    f = pl.program_id(0)
    k = pl.program_id(1)
    dk = wg_ref.shape[0]

    @pl.when(jnp.logical_and(f == 0, k == 0))
    def _():
        _ffn_prologue(x_ref, gpre_ref, xn_ref, o_ref)

    wg = wg_ref[...].astype(BF16)
    wu = wu_ref[...].astype(BF16)
    wg16_ref[...] = wg
    wu16_ref[...] = wu
    xk = xn_ref[:, pl.ds(pl.multiple_of(k * dk, dk), dk)]
    g = jnp.dot(xk, wg, preferred_element_type=F32)
    u = jnp.dot(xk, wu, preferred_element_type=F32)

    @pl.when(k == 0)
    def _():
        g_ref[...] = g
        u_ref[...] = u

    @pl.when(k == 1)
    def _():
        wd = wd_ref[...].astype(BF16)
        wd16_ref[...] = wd
        hid = (_silu(g_ref[...] + g) * (u_ref[...] + u)).astype(BF16)
        o_ref[...] += jnp.dot(hid, wd, preferred_element_type=F32)

    @pl.when(jnp.logical_and(f == pl.num_programs(0) - 1, k == 1))
    def _():
        _ffn_epilogue(x_ref, gpost_ref, o_ref)


def _ffn(x, g_pre, wg, wu, wd, g_post, *, bm, bf):
    S, D = x.shape
    d_ff = wg.shape[1]
    once = dict(pipeline_mode=pl.Buffered(1))
    h, wg16, wu16, wd16 = pl.pallas_call(
        _ffn_first_kernel,
        name="ffn_first",
        grid=(d_ff // bf, 2),
        in_specs=[
            pl.BlockSpec((bm, D), lambda f, k: (0, 0), **once),
            pl.BlockSpec((1, D), lambda f, k: (0, 0)),
            pl.BlockSpec((D // 2, bf), lambda f, k: (k, f)),
            pl.BlockSpec((D // 2, bf), lambda f, k: (k, f)),
            pl.BlockSpec((bf, D), lambda f, k: (f, 0)),
            pl.BlockSpec((1, D), lambda f, k: (0, 0)),
        ],
        out_specs=[
            pl.BlockSpec((bm, D), lambda f, k: (0, 0)),
            pl.BlockSpec((D // 2, bf), lambda f, k: (k, f)),
            pl.BlockSpec((D // 2, bf), lambda f, k: (k, f)),
            pl.BlockSpec((bf, D), lambda f, k: (f, 0)),
        ],
        out_shape=[
            jax.ShapeDtypeStruct((S, D), F32),
            jax.ShapeDtypeStruct((D, d_ff), BF16),
            jax.ShapeDtypeStruct((D, d_ff), BF16),
            jax.ShapeDtypeStruct((d_ff, D), BF16),
        ],
        scratch_shapes=[pltpu.VMEM((bm, D), BF16), pltpu.VMEM((bm, bf), F32), pltpu.VMEM((bm, bf), F32)],
        compiler_params=_params("arbitrary", "arbitrary"),
    )(x, g_pre, wg, wu, wd, g_post)
    if S == bm:
        return h
    return pl.pallas_call(
        _ffn_kernel,
        name="ffn",
        grid=(S // bm - 1, d_ff // bf),
        in_specs=[
            pl.BlockSpec((bm, D), lambda i, f: (i + 1, 0)),
            pl.BlockSpec((1, D), lambda i, f: (0, 0)),
            pl.BlockSpec((D, bf), lambda i, f: (0, f)),
            pl.BlockSpec((D, bf), lambda i, f: (0, f)),
            pl.BlockSpec((bf, D), lambda i, f: (f, 0)),
            pl.BlockSpec((1, D), lambda i, f: (0, 0)),
            pl.BlockSpec(memory_space=pl.ANY),
        ],
        out_specs=pl.BlockSpec((bm, D), lambda i, f: (i + 1, 0)),
        out_shape=jax.ShapeDtypeStruct((S, D), F32),
        input_output_aliases={6: 0},
        scratch_shapes=[pltpu.VMEM((bm, D), BF16)],
        compiler_params=_params("parallel", "arbitrary"),
    )(x, g_pre, wg16, wu16, wd16, g_post, h)


def _inproj_kernel(h_ref, g_ref, w1_ref, w2_ref, ws_ref, scale_ref, o_ref, os_ref, a_ref, *, n1):
    n = pl.program_id(1)

    @pl.when(n == 0)
    def _prologue():
        def rows_fn(rows):
            a_ref[rows, :] = _rms(h_ref[rows, :], g_ref[...]).astype(BF16)
        _for_row_chunks(h_ref.shape[0], rows_fn)
        os_ref[...] = jnp.dot(a_ref[...], ws_ref[...], preferred_element_type=F32)

    def tile(w_ref):
        for cols in _sub_tiles(w_ref.shape[1]):
            acc = jnp.dot(a_ref[...], w_ref[:, cols], preferred_element_type=F32)
            o_ref[:, cols] = (acc * scale_ref[:, cols]).astype(BF16)

    @pl.when(n < n1)
    def _():
        tile(w1_ref)

    @pl.when(n >= n1)
    def _():
        tile(w2_ref)


def _inproj(h, g, w_all, w_rest, w_small, col_scale, *, bm, bn):
    S, D = h.shape
    n_cols = col_scale.shape[1]
    n1 = (n_cols - w_rest.shape[1]) // bn
    return pl.pallas_call(
        functools.partial(_inproj_kernel, n1=n1),
        name="inproj",
        grid=(S // bm, n_cols // bn),
        in_specs=[
            pl.BlockSpec((bm, D), lambda i, n: (i, 0), pipeline_mode=pl.Buffered(1)),
            pl.BlockSpec((1, D), lambda i, n: (0, 0)),
            pl.BlockSpec((D, bn), lambda i, n: (0, jnp.minimum(n, n1 - 1))),
            pl.BlockSpec((D, bn), lambda i, n: (0, jnp.maximum(n - n1, 0))),
            pl.BlockSpec((D, LANES), lambda i, n: (0, 0)),
            pl.BlockSpec((1, bn), lambda i, n: (0, n)),
        ],
        out_specs=[
            pl.BlockSpec((bm, bn), lambda i, n: (i, n)),
            pl.BlockSpec((bm, LANES), lambda i, n: (i, 0)),
        ],
        out_shape=[
            jax.ShapeDtypeStruct((S, n_cols), BF16),
            jax.ShapeDtypeStruct((S, LANES), F32),
        ],
        scratch_shapes=[pltpu.VMEM((bm, D), BF16)],
        compiler_params=_params("parallel", "arbitrary"),
    )(h, g, w_all, w_rest, w_small, col_scale)


def _fox_gate_kernel(small_ref, bias_ref, c_ref, carry_ref):
    t = pl.program_id(0)
    T = small_ref.shape[0]

    @pl.when(t == 0)
    def _init():
        carry_ref[...] = jnp.zeros_like(carry_ref)

    log_f = _log_sigmoid(small_ref[...] + bias_ref[...])
    row = lax.broadcasted_iota(jnp.int32, (T, T), 0)
    col = lax.broadcasted_iota(jnp.int32, (T, T), 1)
    tri = jnp.where(row >= col, 1.0, 0.0).astype(F32)
    c = jnp.dot(tri, log_f, precision=HIGHEST, preferred_element_type=F32) + carry_ref[...]
    c_ref[...] = c * LOG2_E
    carry_ref[...] = c[T - 1:T, :]


def _fox_gate(small, bias, *, bt):
    S = small.shape[0]
    return pl.pallas_call(
        _fox_gate_kernel,
        name="fox_gate",
        grid=(S // bt,),
        in_specs=[
            pl.BlockSpec((bt, LANES), lambda t: (t, 0)),
            pl.BlockSpec((1, LANES), lambda t: (0, 0)),
        ],
        out_specs=pl.BlockSpec((bt, LANES), lambda t: (t, 0)),
        out_shape=jax.ShapeDtypeStruct((S, LANES), F32),
        scratch_shapes=[pltpu.VMEM((1, LANES), F32)],
        compiler_params=_params("arbitrary"),
    )(small, bias)


FOX_PAIR = 2
FOX_ROWS = 16


def _split3(c):
    hi = c.astype(BF16).astype(F32)
    mid = (c - hi).astype(BF16).astype(F32)
    lo = (c - hi - mid).astype(BF16).astype(F32)
    return hi, mid, lo


def _lane_tile(x, n):
    return jnp.concatenate([x] * n, axis=1)


def _head_column(block, head):
    lane = lax.broadcasted_iota(jnp.int32, block.shape, 1)
    return jnp.sum(jnp.where(lane == head, block, 0.0), axis=1, keepdims=True)


def _fox_kernel(q_ref, k_ref, v_ref, c_ref, o_ref, kaug_ref, vaug_ref, qaug_ref, s_ref, p_ref,
                m_ref, alpha_ref, acc_ref, *, bk):
    hp = pl.program_id(0)
    i = pl.program_id(1)
    bq = q_ref.shape[0]
    S = k_ref.shape[0]
    dh = FOX_HEAD_DIM
    R = FOX_ROWS

    @pl.when(i == 0)
    def _build_kv_side():
        rows_per = 128

        def body(r, carry):
            rows = pl.ds(pl.multiple_of(r * rows_per, rows_per), rows_per)
            cblk = c_ref[rows, :]
            lane = lax.broadcasted_iota(jnp.int32, (rows_per, LANES), 1)
            for hh in range(FOX_PAIR):
                hi, mid, lo = _split3(_head_column(cblk, hp * FOX_PAIR + hh))
                extra = jnp.where(lane < 3, 1.0,
                                  jnp.where(lane == 3, -hi,
                                            jnp.where(lane == 4, -mid,
                                                      jnp.where(lane == 5, -lo, 0.0))))
                kaug_ref[hh, rows, 0:dh] = k_ref[rows, hh * dh:(hh + 1) * dh]
                kaug_ref[hh, rows, dh:2 * dh] = extra.astype(BF16)
                vaug_ref[hh, rows, 0:dh] = v_ref[rows, hh * dh:(hh + 1) * dh]
                vaug_ref[hh, rows, dh:2 * dh] = jnp.ones((rows_per, dh), BF16)
            return carry
        lax.fori_loop(0, S // rows_per, body, 0)

    cq_blk = c_ref[pl.ds(pl.multiple_of(i * bq, bq), bq), :]
    lane_q = lax.broadcasted_iota(jnp.int32, (bq, LANES), 1)
    for hh in range(FOX_PAIR):
        hi, mid, lo = _split3(_head_column(cq_blk, hp * FOX_PAIR + hh))
        extra = jnp.where(lane_q == 0, hi,
                          jnp.where(lane_q == 1, mid,
                                    jnp.where(lane_q == 2, lo,
                                              jnp.where(lane_q < 6, 1.0, 0.0))))
        qaug_ref[hh, :, 0:dh] = q_ref[:, hh * dh:(hh + 1) * dh]
        qaug_ref[hh, :, dh:2 * dh] = extra.astype(BF16)
        m_ref[hh] = jnp.full((bq, LANES), -jnp.inf, F32)
        acc_ref[hh] = jnp.zeros((bq, 2 * dh), F32)

    def chunk(k0, width, masked):
        ks = pl.ds(pl.multiple_of(k0, bq), width)
        groups = [slice(g * R, (g + 1) * R) for g in range(bq // R)]
        for hh in range(FOX_PAIR):
            s_ref[hh, :, 0:width] = lax.dot_general(
                qaug_ref[hh], kaug_ref[hh, ks, :], (((1,), (1,)), ((), ())),
                preferred_element_type=F32)
        if masked:
            diff = (lax.broadcasted_iota(jnp.int32, (R, width), 1)
                    - lax.broadcasted_iota(jnp.int32, (R, width), 0))
            limit = i * bq - k0
        for hh in range(FOX_PAIR):
            for g, rows in enumerate(groups):
                s = s_ref[hh, rows, 0:width]
                if masked:
                    s = jnp.where(diff <= limit + g * R, s, -jnp.inf)
                    s_ref[hh, rows, 0:width] = s
                m_old = m_ref[hh, rows, :]
                m_new = jnp.maximum(m_old, jnp.broadcast_to(
                    jnp.max(s, axis=1, keepdims=True), (R, LANES)))
                alpha_ref[hh, rows, :] = jnp.exp2(m_old - m_new)
                m_ref[hh, rows, :] = m_new
        for hh in range(FOX_PAIR):
            for rows in groups:
                m = _lane_tile(m_ref[hh, rows, :], width // LANES)
                p_ref[hh, rows, 0:width] = jnp.exp2(s_ref[hh, rows, 0:width] - m).astype(BF16)
        for hh in range(FOX_PAIR):
            alpha = _lane_tile(alpha_ref[hh], 2)
            acc_ref[hh] = alpha * acc_ref[hh] + jnp.dot(
                p_ref[hh, :, 0:width], vaug_ref[hh, ks, :], preferred_element_type=F32)

    n_full = (i * bq) // bk

    def full_body(j, carry):
        chunk(j * bk, bk, False)
        return carry
    lax.fori_loop(0, n_full, full_body, 0)

    tail_start = n_full * bk
    tail_blocks = (i + 1) - n_full * (bk // bq)
    for t in range(1, bk // bq + 1):
        @pl.when(tail_blocks == t)
        def _tail(t=t):
            chunk(tail_start, t * bq, True)

    for hh in range(FOX_PAIR):
        acc = acc_ref[hh]
        o_ref[:, hh * dh:(hh + 1) * dh] = (acc[:, 0:dh] / acc[:, dh:2 * dh]).astype(o_ref.dtype)


def _fox(proj, c2, *, bq, bk):
    S = proj.shape[0]
    dh = FOX_HEAD_DIM
    w = FOX_PAIR * dh
    return pl.pallas_call(
        functools.partial(_fox_kernel, bk=bk),
        name="fox",
        grid=(FOX_HEADS // FOX_PAIR, S // bq),
        in_specs=[
            pl.BlockSpec((bq, w), lambda h, i: (i, OFF_FQ // w + h)),
            pl.BlockSpec((S, w), lambda h, i: (0, OFF_FK // w + h)),
            pl.BlockSpec((S, w), lambda h, i: (0, OFF_FV // w + h)),
            pl.BlockSpec((S, LANES), lambda h, i: (0, 0)),
        ],
        out_specs=pl.BlockSpec((bq, w), lambda h, i: (i, h)),
        out_shape=jax.ShapeDtypeStruct((S, FOX_WIDTH), BF16),
        scratch_shapes=[
            pltpu.VMEM((FOX_PAIR, S, 2 * dh), BF16),
            pltpu.VMEM((FOX_PAIR, S, 2 * dh), BF16),
            pltpu.VMEM((FOX_PAIR, bq, 2 * dh), BF16),
            pltpu.VMEM((FOX_PAIR, bq, bk), F32),
            pltpu.VMEM((FOX_PAIR, bq, bk), BF16),
            pltpu.VMEM((FOX_PAIR, bq, LANES), F32),
            pltpu.VMEM((FOX_PAIR, bq, LANES), F32),
            pltpu.VMEM((FOX_PAIR, bq, 2 * dh), F32),
        ],
        compiler_params=_params("parallel", "arbitrary"),
    )(proj, proj, proj, c2)


def _gla_kernel(q_ref, k_ref, v_ref, gr_ref, small_ref, wgate_ref, bgate_ref,
                gnorm_ref, tri_ref, o_ref, st_ref):
    t = pl.program_id(1)
    T = q_ref.shape[0]
    C = GLA_CHUNK

    @pl.when(t == 0)
    def _init():
        st_ref[...] = jnp.zeros_like(st_ref)

    dk = q_ref.shape[1]
    gate = jnp.dot(small_ref[...].astype(BF16), wgate_ref[...],
                   preferred_element_type=F32) + bgate_ref[...]
    log_a = _log_sigmoid(gate) * (1.0 / GLA_TAU)

    pieces = jnp.concatenate([x.astype(BF16) for x in _split3(log_a)], axis=1)
    sums = jnp.dot(tri_ref[...].astype(BF16), pieces, preferred_element_type=F32)
    b = sums[:, 0:dk] + sums[:, dk:2 * dk] + sums[:, 2 * dk:3 * dk]
    b_tot = jnp.concatenate(
        [jnp.broadcast_to(b[c * C + C - 1:c * C + C, :], (C, dk)) for c in range(T // C)], axis=0)
    causal = tri_ref[...] > 0.0

    q = q_ref[...].astype(F32)
    k = k_ref[...].astype(F32)
    v = v_ref[...]
    q_dec = (q * jnp.exp(b)).astype(BF16)
    k_inv = (k * jnp.exp(-b)).astype(BF16)
    k_end = (k * jnp.exp(b_tot - b)).astype(BF16)
    a = lax.dot_general(q_dec, k_inv, (((1,), (1,)), ((), ())), preferred_element_type=F32)
    a = jnp.where(causal, a, 0.0).astype(BF16)
    o_intra = jnp.dot(a, v, preferred_element_type=F32)

    for ci in range(T // C):
        lo, hi = ci * C, (ci + 1) * C
        st = st_ref[...]
        o_c = o_intra[lo:hi, :] + lax.dot_general(
            q_dec[lo:hi, :], st.astype(BF16), (((1,), (1,)), ((), ())),
            preferred_element_type=F32)
        decay = jnp.exp(b_tot[lo:lo + 1, :])
        st_ref[...] = st * decay + lax.dot_general(
            v[lo:hi, :], k_end[lo:hi, :], (((0,), (0,)), ((), ())),
            preferred_element_type=F32)
        y = _rms(o_c, gnorm_ref[...])
        o_ref[lo:hi, :] = (y * _silu(gr_ref[lo:hi, :].astype(F32))).astype(o_ref.dtype)


def _chunk_causal_mask(n):
    r = jnp.arange(n)[:, None]
    c = jnp.arange(n)[None, :]
    return ((r // GLA_CHUNK == c // GLA_CHUNK) & (r >= c)).astype(F32)


def _gla(proj, small, wgate_pad, bgate, gnorm, *, bt):
    S = proj.shape[0]
    dk, dv = GLA_HEAD_K, GLA_HEAD_V
    return pl.pallas_call(
        _gla_kernel,
        name="gla",
        grid=(GLA_HEADS, S // bt),
        in_specs=[
            pl.BlockSpec((bt, dk), lambda h, t: (t, OFF_GQ // dk + h)),
            pl.BlockSpec((bt, dk), lambda h, t: (t, OFF_GK // dk + h)),
            pl.BlockSpec((bt, dv), lambda h, t: (t, OFF_GV // dv + h)),
            pl.BlockSpec((bt, dv), lambda h, t: (t, OFF_GR // dv + h)),
            pl.BlockSpec((bt, LANES), lambda h, t: (t, 0)),
            pl.BlockSpec((LANES, dk), lambda h, t: (0, h)),
            pl.BlockSpec((1, dk), lambda h, t: (0, h)),
            pl.BlockSpec((1, dv), lambda h, t: (0, 0)),
            pl.BlockSpec((bt, bt), lambda h, t: (0, 0)),
        ],
        out_specs=pl.BlockSpec((bt, dv), lambda h, t: (t, h)),
        out_shape=jax.ShapeDtypeStruct((S, GLA_WIDTH), BF16),
        scratch_shapes=[pltpu.VMEM((dv, dk), F32)],
        compiler_params=_params("parallel", "arbitrary"),
    )(proj, proj, proj, proj, small, wgate_pad, bgate, gnorm, _chunk_causal_mask(bt))


def _outproj_kernel(fox_ref, gla_ref, wt_ref, wb_ref, h_ref, g_ref, o_ref):
    n = pl.program_id(1)
    bn = wt_ref.shape[1]
    for sub in _sub_tiles(bn):
        cols = pl.ds(pl.multiple_of(n * bn + sub.start, MXU_COLS), MXU_COLS)
        o_ref[:, cols] = (jnp.dot(fox_ref[...], wt_ref[:, sub], preferred_element_type=F32)
                          + jnp.dot(gla_ref[...], wb_ref[:, sub], preferred_element_type=F32))

    @pl.when(n == pl.num_programs(1) - 1)
    def _epilogue():
        def rows_fn(rows):
            o_ref[rows, :] = h_ref[rows, :] + _rms(o_ref[rows, :], g_ref[...])
        _for_row_chunks(h_ref.shape[0], rows_fn)


def _outproj(o_fox, o_gla, w_o, h, g, *, bm, bn):
    S, D = h.shape
    half = o_fox.shape[1]
    return pl.pallas_call(
        _outproj_kernel,
        name="outproj",
        grid=(S // bm, D // bn),
        in_specs=[
            pl.BlockSpec((bm, half), lambda i, n: (i, 0)),
            pl.BlockSpec((bm, half), lambda i, n: (i, 0)),
            pl.BlockSpec((half, bn), lambda i, n: (0, n)),
            pl.BlockSpec((half, bn), lambda i, n: (1, n)),
            pl.BlockSpec((bm, D), lambda i, n: (i, 0)),
            pl.BlockSpec((1, D), lambda i, n: (0, 0)),
        ],
        out_specs=pl.BlockSpec((bm, D), lambda i, n: (i, 0)),
        out_shape=jax.ShapeDtypeStruct((S, D), F32),
        compiler_params=_params("parallel", "arbitrary"),
    )(o_fox, o_gla, w_o, w_o, h, g)


def _ple_kernel(h_ref, p_ref, wp_ref, g_ref, wg_ref, o_ref, hb_ref, e_ref):
    n = pl.program_id(1)
    bn = wg_ref.shape[1]

    @pl.when(n == 0)
    def _prologue():
        e_ref[...] = jnp.dot(p_ref[...].astype(BF16), wp_ref[...], preferred_element_type=F32)

        def rows_fn(rows):
            hb_ref[rows, :] = h_ref[rows, :].astype(BF16)
            e_ref[rows, :] = _rms(e_ref[rows, :], g_ref[...])
        _for_row_chunks(h_ref.shape[0], rows_fn)

    for sub in _sub_tiles(bn):
        cols = pl.ds(pl.multiple_of(n * bn + sub.start, MXU_COLS), MXU_COLS)
        gate = jax.nn.sigmoid(jnp.dot(hb_ref[...], wg_ref[:, sub], preferred_element_type=F32))
        o_ref[:, sub] = h_ref[:, cols] + e_ref[:, cols] * gate


def _ple(h, p, w_proj, g, w_gate, *, bm, bn):
    S, D = h.shape
    dp = p.shape[1]
    return pl.pallas_call(
        _ple_kernel,
        name="ple",
        grid=(S // bm, D // bn),
        in_specs=[
            pl.BlockSpec((bm, D), lambda i, n: (i, 0)),
            pl.BlockSpec((bm, dp), lambda i, n: (i, 0)),
            pl.BlockSpec((dp, D), lambda i, n: (0, 0)),
            pl.BlockSpec((1, D), lambda i, n: (0, 0)),
            pl.BlockSpec((D, bn), lambda i, n: (0, n)),
        ],
        out_specs=pl.BlockSpec((bm, bn), lambda i, n: (i, n)),
        out_shape=jax.ShapeDtypeStruct((S, D), F32),
        scratch_shapes=[pltpu.VMEM((bm, D), BF16), pltpu.VMEM((bm, D), F32)],
        compiler_params=_params("parallel", "arbitrary"),
    )(h, p, w_proj, g, w_gate)


def _tile(n, preferred):
    t = min(n, preferred)
    while n % t:
        t //= 2
    return t


def _split_w_in(w_in):
    D = w_in.shape[0]
    ff_lo = 3 * FOX_WIDTH
    ff_hi = ff_lo + FOX_HEADS
    glr_lo = ff_hi + 2 * GLA_KEY_WIDTH + 2 * GLA_WIDTH
    glr_hi = glr_lo + GLA_GATE_RANK
    w_all = w_in.astype(BF16)
    rest = w_all[:, ff_hi:glr_lo]
    pad = jnp.zeros((D, LANES - FOX_HEADS - GLA_GATE_RANK), BF16)
    small = jnp.concatenate([w_all[:, ff_lo:ff_hi], w_all[:, glr_lo:glr_hi], pad], axis=1)
    return w_all, rest, small


def kernel(x, p, ffn1_norm_pre, ffn1_w_gate, ffn1_w_up, ffn1_w_down, ffn1_norm_post, mix_norm_pre, w_in, fox_b_f, gla_w_gate, gla_b_gate, gla_norm_g, w_o, mix_norm_post, ffn2_norm_pre, ffn2_w_gate, ffn2_w_up, ffn2_w_down, ffn2_norm_post, ple_w_proj, ple_norm, ple_w_gate):
    B, S, D = x.shape
    assert B == 1, "the attention kernels treat the row axis as one sequence"
    depth = w_in.shape[0]
    h = x.reshape(S, D)

    bm = _tile(S, 512)
    bf = _tile(ffn1_w_gate.shape[-1], 256)
    bq = _tile(S, 512)
    bk = max(bq, _tile(S, 1024))
    bt = _tile(S, 512)

    col_scale = jnp.ones((1, MAIN_COLS), F32)
    col_scale = col_scale.at[:, OFF_FQ:OFF_FQ + FOX_WIDTH].set(FOX_HEAD_DIM ** -0.5 * LOG2_E)
    col_scale = col_scale.at[:, OFF_GQ:OFF_GQ + GLA_KEY_WIDTH].set(GLA_HEAD_K ** -0.5)

    def row(v):
        return v.reshape(1, -1).astype(F32)

    for i in range(depth):
        h = _ffn(h, row(ffn1_norm_pre[i]), ffn1_w_gate[i], ffn1_w_up[i], ffn1_w_down[i],
                 row(ffn1_norm_post[i]), bm=bm, bf=bf)

        w_all, w_rest, w_small = _split_w_in(w_in[i])
        proj, small = _inproj(h, row(mix_norm_pre[i]), w_all, w_rest, w_small, col_scale,
                              bm=_tile(S, 1024), bn=768)

        bias_f = jnp.zeros((1, LANES), F32).at[0, :FOX_HEADS].set(fox_b_f[i])
        c2 = _fox_gate(small, bias_f, bt=bt)
        o_fox = _fox(proj, c2, bq=bq, bk=bk)

        wgate_pad = jnp.zeros((LANES, GLA_KEY_WIDTH), F32).at[
            FOX_HEADS:FOX_HEADS + GLA_GATE_RANK, :].set(gla_w_gate[i]).astype(BF16)
        o_gla = _gla(proj, small, wgate_pad, row(gla_b_gate[i]), row(gla_norm_g[i]), bt=bt)

        h = _outproj(o_fox, o_gla, w_o[i].astype(BF16), h, row(mix_norm_post[i]), bm=bm, bn=512)

        h = _ffn(h, row(ffn2_norm_pre[i]), ffn2_w_gate[i], ffn2_w_up[i], ffn2_w_down[i],
                 row(ffn2_norm_post[i]), bm=bm, bf=bf)

        h = _ple(h, p[i].reshape(S, -1), ple_w_proj[i].astype(BF16), row(ple_norm[i]),
                 ple_w_gate[i].astype(BF16), bm=bm, bn=1024)
    return h.reshape(B, S, D)
```

```python
import functools

import jax
import jax.numpy as jnp
from jax import lax
from jax.experimental import pallas as pl
from jax.experimental.pallas import tpu as pltpu

F32 = jnp.float32
BF16 = jnp.bfloat16
HIGHEST = lax.Precision.HIGHEST

EPS = 1e-6
LOG2_E = 1.4426950408889634
MACARON_WEIGHT = 0.5
FOX_HEAD_DIM = 128
FOX_HEADS = 16
FOX_WIDTH = FOX_HEADS * FOX_HEAD_DIM
GLA_HEADS = 4
GLA_WIDTH = 2048
GLA_HEAD_V = GLA_WIDTH // GLA_HEADS
GLA_KEY_WIDTH = GLA_WIDTH // 2
GLA_HEAD_K = GLA_KEY_WIDTH // GLA_HEADS
GLA_GATE_RANK = 16
GLA_TAU = 16.0
GLA_CHUNK = 64

LANES = 128
MXU_COLS = 256
VMEM_LIMIT_BYTES = 56 * 1024 * 1024

MAIN_COLS = 3 * FOX_WIDTH + 2 * GLA_WIDTH + 2 * GLA_KEY_WIDTH
OFF_FQ = 0
OFF_FK = FOX_WIDTH
OFF_FV = 2 * FOX_WIDTH
OFF_GQ = 3 * FOX_WIDTH
OFF_GK = OFF_GQ + GLA_KEY_WIDTH
OFF_GV = OFF_GK + GLA_KEY_WIDTH
OFF_GR = OFF_GV + GLA_WIDTH

ROW_CHUNK = 32
ROW_BLOCK = 128


def _params(*semantics):
    return pltpu.CompilerParams(dimension_semantics=semantics,
                                vmem_limit_bytes=VMEM_LIMIT_BYTES)


def _rms(x, g):
    ms = jnp.mean(x * x, axis=-1, keepdims=True)
    return x * lax.rsqrt(ms + EPS) * g


def _log_sigmoid(x):
    return jnp.minimum(x, 0.0) - jnp.log1p(jnp.exp(-jnp.abs(x)))


def _silu(x):
    return x * jax.nn.sigmoid(x)


def _sub_tiles(n_cols):
    return [slice(c, c + MXU_COLS) for c in range(0, n_cols, MXU_COLS)]


def _rms_rows(src_ref, g_ref, emit):
    def body(b, carry):
        chunks = [pl.ds(pl.multiple_of(b * ROW_BLOCK + c * ROW_CHUNK, ROW_CHUNK), ROW_CHUNK)
                  for c in range(ROW_BLOCK // ROW_CHUNK)]
        scales = []
        for rows in chunks:
            x = src_ref[rows, :]
            scales.append(lax.rsqrt(jnp.mean(x * x, axis=-1, keepdims=True) + EPS))
        for rows, scale in zip(chunks, scales):
            emit(rows, src_ref[rows, :] * scale * g_ref[...])
        return carry
    lax.fori_loop(0, src_ref.shape[0] // ROW_BLOCK, body, 0)


def _ffn_prologue(x_ref, gpre_ref, xn_ref, o_ref):
    def emit(rows, y):
        xn_ref[rows, :] = y.astype(BF16)
        o_ref[rows, :] = jnp.zeros((ROW_CHUNK, o_ref.shape[1]), F32)
    _rms_rows(x_ref, gpre_ref, emit)


def _ffn_epilogue(x_ref, gpost_ref, o_ref):
    def emit(rows, y):
        o_ref[rows, :] = x_ref[rows, :] + MACARON_WEIGHT * y
    _rms_rows(o_ref, gpost_ref, emit)


FFN_COPY_STEPS = 4


def _ffn_kernel(x_ref, gpre_ref, wg_ref, wu_ref, wd_ref, gpost_ref, first_ref, o_ref, xn_ref, *, nf):
    s = pl.program_id(0)
    rows_c = first_ref.shape[0]

    @pl.when(s < FFN_COPY_STEPS)
    def _():
        o_ref[pl.ds(pl.multiple_of(s * rows_c, rows_c), rows_c), :] = first_ref[...]

    @pl.when(s >= FFN_COPY_STEPS)
    def _():
        f = (s - FFN_COPY_STEPS) % nf

        @pl.when(f == 0)
        def _():
            _ffn_prologue(x_ref, gpre_ref, xn_ref, o_ref)

        xn = xn_ref[...]
        g = jnp.dot(xn, wg_ref[...], preferred_element_type=F32)
        u = jnp.dot(xn, wu_ref[...], preferred_element_type=F32)
        hid = (_silu(g) * u).astype(BF16)
        o_ref[...] += jnp.dot(hid, wd_ref[...], preferred_element_type=F32)

        @pl.when(f == nf - 1)
        def _():
            _ffn_epilogue(x_ref, gpost_ref, o_ref)


def _ffn_first_kernel(x_ref, gpre_ref, wg_ref, wu_ref, wd_ref, gpost_ref,
                      o_ref, wg16_ref, wu16_ref, wd16_ref, xn_ref, g_ref, u_ref):
    f = pl.program_id(0)
    k = pl.program_id(1)
    dk = wg_ref.shape[0]

    @pl.when(jnp.logical_and(f == 0, k == 0))
    def _():
        _ffn_prologue(x_ref, gpre_ref, xn_ref, o_ref)

    wg = wg_ref[...].astype(BF16)
    wu = wu_ref[...].astype(BF16)
    wg16_ref[...] = wg
    wu16_ref[...] = wu
    xk = xn_ref[:, pl.ds(pl.multiple_of(k * dk, dk), dk)]
    g = jnp.dot(xk, wg, preferred_element_type=F32)
    u = jnp.dot(xk, wu, preferred_element_type=F32)

    @pl.when(k == 0)
    def _():
        g_ref[...] = g
        u_ref[...] = u

    @pl.when(k == 1)
    def _():
        wd = wd_ref[...].astype(BF16)
        wd16_ref[...] = wd
        hid = (_silu(g_ref[...] + g) * (u_ref[...] + u)).astype(BF16)
        o_ref[...] += jnp.dot(hid, wd, preferred_element_type=F32)

    @pl.when(jnp.logical_and(f == pl.num_programs(0) - 1, k == 1))
    def _():
        _ffn_epilogue(x_ref, gpost_ref, o_ref)


def _ffn(x, g_pre, wg, wu, wd, g_post, *, bm, bf):
    S, D = x.shape
    d_ff = wg.shape[1]
    once = dict(pipeline_mode=pl.Buffered(1))
    h, wg16, wu16, wd16 = pl.pallas_call(
        _ffn_first_kernel,
        name="ffn_first",
        grid=(d_ff // bf, 2),
        in_specs=[
            pl.BlockSpec((bm, D), lambda f, k: (0, 0), **once),
            pl.BlockSpec((1, D), lambda f, k: (0, 0)),
            pl.BlockSpec((D // 2, bf), lambda f, k: (k, f)),
            pl.BlockSpec((D // 2, bf), lambda f, k: (k, f)),
            pl.BlockSpec((bf, D), lambda f, k: (f, 0)),
            pl.BlockSpec((1, D), lambda f, k: (0, 0)),
        ],
        out_specs=[
            pl.BlockSpec((bm, D), lambda f, k: (0, 0)),
            pl.BlockSpec((D // 2, bf), lambda f, k: (k, f)),
            pl.BlockSpec((D // 2, bf), lambda f, k: (k, f)),
            pl.BlockSpec((bf, D), lambda f, k: (f, 0)),
        ],
        out_shape=[
            jax.ShapeDtypeStruct((bm, D), F32),
            jax.ShapeDtypeStruct((D, d_ff), BF16),
            jax.ShapeDtypeStruct((D, d_ff), BF16),
            jax.ShapeDtypeStruct((d_ff, D), BF16),
        ],
        scratch_shapes=[pltpu.VMEM((bm, D), BF16), pltpu.VMEM((bm, bf), F32), pltpu.VMEM((bm, bf), F32)],
        compiler_params=_params("arbitrary", "arbitrary"),
    )(x, g_pre, wg, wu, wd, g_post)
    if S == bm:
        return h
    nf = d_ff // bf
    c = FFN_COPY_STEPS

    def row_tile(s):
        return jnp.where(s < c, 0, 1 + (s - c) // nf)

    def ff_tile(s):
        return jnp.where(s < c, 0, (s - c) % nf)

    return pl.pallas_call(
        functools.partial(_ffn_kernel, nf=nf),
        name="ffn",
        grid=(c + (S // bm - 1) * nf,),
        in_specs=[
            pl.BlockSpec((bm, D), lambda s: (row_tile(s), 0)),
            pl.BlockSpec((1, D), lambda s: (0, 0)),
            pl.BlockSpec((D, bf), lambda s: (0, ff_tile(s))),
            pl.BlockSpec((D, bf), lambda s: (0, ff_tile(s))),
            pl.BlockSpec((bf, D), lambda s: (ff_tile(s), 0)),
            pl.BlockSpec((1, D), lambda s: (0, 0)),
            pl.BlockSpec((bm // c, D), lambda s: (jnp.minimum(s, c - 1), 0)),
        ],
        out_specs=pl.BlockSpec((bm, D), lambda s: (row_tile(s), 0)),
        out_shape=jax.ShapeDtypeStruct((S, D), F32),
        scratch_shapes=[pltpu.VMEM((bm, D), BF16)],
        compiler_params=_params("arbitrary"),
    )(x, g_pre, wg16, wu16, wd16, g_post, h)


def _inproj_kernel(h_ref, g_ref, w1_ref, w2_ref, ws_ref, scale_ref, o_ref, os_ref, a_ref, *, n1):
    n = pl.program_id(1)

    @pl.when(n == 0)
    def _prologue():
        def emit(rows, y):
            a_ref[rows, :] = y.astype(BF16)
        _rms_rows(h_ref, g_ref, emit)
        os_ref[...] = jnp.dot(a_ref[...], ws_ref[...], preferred_element_type=F32)

    def tile(w_ref):
        for cols in _sub_tiles(w_ref.shape[1]):
            acc = jnp.dot(a_ref[...], w_ref[:, cols], preferred_element_type=F32)
            o_ref[:, cols] = (acc * scale_ref[:, cols]).astype(BF16)

    @pl.when(n < n1)
    def _():
        tile(w1_ref)

    @pl.when(n >= n1)
    def _():
        tile(w2_ref)


def _inproj(h, g, w_all, w_rest, w_small, col_scale, *, bm, bn):
    S, D = h.shape
    n_cols = col_scale.shape[1]
    n1 = (n_cols - w_rest.shape[1]) // bn
    return pl.pallas_call(
        functools.partial(_inproj_kernel, n1=n1),
        name="inproj",
        grid=(S // bm, n_cols // bn),
        in_specs=[
            pl.BlockSpec((bm, D), lambda i, n: (i, 0), pipeline_mode=pl.Buffered(1)),
            pl.BlockSpec((1, D), lambda i, n: (0, 0)),
            pl.BlockSpec((D, bn), lambda i, n: (0, jnp.minimum(n, n1 - 1))),
            pl.BlockSpec((D, bn), lambda i, n: (0, jnp.maximum(n - n1, 0))),
            pl.BlockSpec((D, LANES), lambda i, n: (0, 0)),
            pl.BlockSpec((1, bn), lambda i, n: (0, n)),
        ],
        out_specs=[
            pl.BlockSpec((bm, bn), lambda i, n: (i, n)),
            pl.BlockSpec((bm, LANES), lambda i, n: (i, 0)),
        ],
        out_shape=[
            jax.ShapeDtypeStruct((S, n_cols), BF16),
            jax.ShapeDtypeStruct((S, LANES), F32),
        ],
        scratch_shapes=[pltpu.VMEM((bm, D), BF16)],
        compiler_params=_params("parallel", "arbitrary"),
    )(h, g, w_all, w_rest, w_small, col_scale)


def _fox_gate_kernel(small_ref, bias_ref, c_ref, carry_ref):
    t = pl.program_id(0)
    T = small_ref.shape[0]

    @pl.when(t == 0)
    def _init():
        carry_ref[...] = jnp.zeros_like(carry_ref)

    log_f = _log_sigmoid(small_ref[...] + bias_ref[...])
    row = lax.broadcasted_iota(jnp.int32, (T, T), 0)
    col = lax.broadcasted_iota(jnp.int32, (T, T), 1)
    tri = jnp.where(row >= col, 1.0, 0.0).astype(F32)
    c = jnp.dot(tri, log_f, precision=HIGHEST, preferred_element_type=F32) + carry_ref[...]
    c_ref[...] = c * LOG2_E
    carry_ref[...] = c[T - 1:T, :]


def _fox_gate(small, bias, *, bt):
    S = small.shape[0]
    return pl.pallas_call(
        _fox_gate_kernel,
        name="fox_gate",
        grid=(S // bt,),
        in_specs=[
            pl.BlockSpec((bt, LANES), lambda t: (t, 0)),
            pl.BlockSpec((1, LANES), lambda t: (0, 0)),
        ],
        out_specs=pl.BlockSpec((bt, LANES), lambda t: (t, 0)),
        out_shape=jax.ShapeDtypeStruct((S, LANES), F32),
        scratch_shapes=[pltpu.VMEM((1, LANES), F32)],
        compiler_params=_params("arbitrary"),
    )(small, bias)


FOX_PAIR = 2
FOX_ROWS = 16


def _split3(c):
    hi = c.astype(BF16).astype(F32)
    mid = (c - hi).astype(BF16).astype(F32)
    lo = (c - hi - mid).astype(BF16).astype(F32)
    return hi, mid, lo


def _lane_tile(x, n):
    return jnp.concatenate([x] * n, axis=1)


def _head_column(block, head):
    lane = lax.broadcasted_iota(jnp.int32, block.shape, 1)
    return jnp.sum(jnp.where(lane == head, block, 0.0), axis=1, keepdims=True)


def _fox_kernel(q_ref, k_ref, v_ref, c_ref, o_ref, kaug_ref, vaug_ref, qaug_ref, s_ref, p_ref,
                m_ref, alpha_ref, acc_ref, *, bk):
    hp = pl.program_id(0)
    i = pl.program_id(1)
    bq = q_ref.shape[0]
    S = k_ref.shape[0]
    dh = FOX_HEAD_DIM
    R = FOX_ROWS

    @pl.when(i == 0)
    def _build_kv_side():
        rows_per = 128

        def body(r, carry):
            rows = pl.ds(pl.multiple_of(r * rows_per, rows_per), rows_per)
            cblk = c_ref[rows, :]
            lane = lax.broadcasted_iota(jnp.int32, (rows_per, LANES), 1)
            for hh in range(FOX_PAIR):
                hi, mid, lo = _split3(_head_column(cblk, hp * FOX_PAIR + hh))
                extra = jnp.where(lane < 3, 1.0,
                                  jnp.where(lane == 3, -hi,
                                            jnp.where(lane == 4, -mid,
                                                      jnp.where(lane == 5, -lo, 0.0))))
                kaug_ref[hh, rows, 0:dh] = k_ref[rows, hh * dh:(hh + 1) * dh]
                kaug_ref[hh, rows, dh:2 * dh] = extra.astype(BF16)
                vaug_ref[hh, rows, 0:dh] = v_ref[rows, hh * dh:(hh + 1) * dh]
                vaug_ref[hh, rows, dh:2 * dh] = jnp.ones((rows_per, dh), BF16)
            return carry
        lax.fori_loop(0, S // rows_per, body, 0)

    cq_blk = c_ref[pl.ds(pl.multiple_of(i * bq, bq), bq), :]
    lane_q = lax.broadcasted_iota(jnp.int32, (bq, LANES), 1)
    for hh in range(FOX_PAIR):
        hi, mid, lo = _split3(_head_column(cq_blk, hp * FOX_PAIR + hh))
        extra = jnp.where(lane_q == 0, hi,
                          jnp.where(lane_q == 1, mid,
                                    jnp.where(lane_q == 2, lo,
                                              jnp.where(lane_q < 6, 1.0, 0.0))))
        qaug_ref[hh, :, 0:dh] = q_ref[:, hh * dh:(hh + 1) * dh]
        qaug_ref[hh, :, dh:2 * dh] = extra.astype(BF16)
        m_ref[hh] = jnp.full((bq, LANES), -jnp.inf, F32)
        acc_ref[hh] = jnp.zeros((bq, 2 * dh), F32)

    groups = [slice(g * R, (g + 1) * R) for g in range(bq // R)]

    def chunk(k0, width, masked):
        ks = pl.ds(pl.multiple_of(k0, bq), width)
        for hh in range(FOX_PAIR):
            s_ref[hh, :, 0:width] = lax.dot_general(
                qaug_ref[hh], kaug_ref[hh, ks, :], (((1,), (1,)), ((), ())),
                preferred_element_type=F32)
        if masked:
            diff = (lax.broadcasted_iota(jnp.int32, (R, width), 1)
                    - lax.broadcasted_iota(jnp.int32, (R, width), 0))
            limit = i * bq - k0
        for hh in range(FOX_PAIR):
            for g, rows in enumerate(groups):
                s = s_ref[hh, rows, 0:width]
                if masked:
                    s = jnp.where(diff <= limit + g * R, s, -jnp.inf)
                    s_ref[hh, rows, 0:width] = s
                m_old = m_ref[hh, rows, :]
                m_new = jnp.maximum(m_old, jnp.broadcast_to(
                    jnp.max(s, axis=1, keepdims=True), (R, LANES)))
                alpha_ref[hh, rows, :] = jnp.exp2(m_old - m_new)
                m_ref[hh, rows, :] = m_new
        for hh in range(FOX_PAIR):
            for rows in groups:
                m = _lane_tile(m_ref[hh, rows, :], width // LANES)
                p_ref[hh, rows, 0:width] = jnp.exp2(s_ref[hh, rows, 0:width] - m).astype(BF16)
        for hh in range(FOX_PAIR):
            alpha = _lane_tile(alpha_ref[hh], 2)
            acc_ref[hh] = alpha * acc_ref[hh] + jnp.dot(
                p_ref[hh, :, 0:width], vaug_ref[hh, ks, :], preferred_element_type=F32)

    n_full = (i * bq) // bk

    def full_body(j, carry):
        chunk(j * bk, bk, False)
        return carry
    lax.fori_loop(0, n_full, full_body, 0)

    tail_start = n_full * bk
    tail_blocks = (i + 1) - n_full * (bk // bq)
    for t in range(1, bk // bq + 1):
        @pl.when(tail_blocks == t)
        def _tail(t=t):
            chunk(tail_start, t * bq, True)

    for hh in range(FOX_PAIR):
        acc = acc_ref[hh]
        o_ref[:, hh * dh:(hh + 1) * dh] = (acc[:, 0:dh] / acc[:, dh:2 * dh]).astype(o_ref.dtype)


def _fox(proj, c2, *, bq, bk):
    S = proj.shape[0]
    dh = FOX_HEAD_DIM
    w = FOX_PAIR * dh
    return pl.pallas_call(
        functools.partial(_fox_kernel, bk=bk),
        name="fox",
        grid=(FOX_HEADS // FOX_PAIR, S // bq),
        in_specs=[
            pl.BlockSpec((bq, w), lambda h, i: (i, OFF_FQ // w + h)),
            pl.BlockSpec((S, w), lambda h, i: (0, OFF_FK // w + h)),
            pl.BlockSpec((S, w), lambda h, i: (0, OFF_FV // w + h)),
            pl.BlockSpec((S, LANES), lambda h, i: (0, 0)),
        ],
        out_specs=pl.BlockSpec((bq, w), lambda h, i: (i, h)),
        out_shape=jax.ShapeDtypeStruct((S, FOX_WIDTH), BF16),
        scratch_shapes=[
            pltpu.VMEM((FOX_PAIR, S, 2 * dh), BF16),
            pltpu.VMEM((FOX_PAIR, S, 2 * dh), BF16),
            pltpu.VMEM((FOX_PAIR, bq, 2 * dh), BF16),
            pltpu.VMEM((FOX_PAIR, bq, bk), F32),
            pltpu.VMEM((FOX_PAIR, bq, bk), BF16),
            pltpu.VMEM((FOX_PAIR, bq, LANES), F32),
            pltpu.VMEM((FOX_PAIR, bq, LANES), F32),
            pltpu.VMEM((FOX_PAIR, bq, 2 * dh), F32),
        ],
        compiler_params=_params("parallel", "arbitrary"),
    )(proj, proj, proj, c2)


def _gla_kernel(q_ref, k_ref, v_ref, gr_ref, small_ref, wgate_ref, bgate_ref,
                gnorm_ref, tri_ref, o_ref, st_ref):
    t = pl.program_id(1)
    T = q_ref.shape[0]
    C = GLA_CHUNK

    @pl.when(t == 0)
    def _init():
        st_ref[...] = jnp.zeros_like(st_ref)

    dk = q_ref.shape[1]
    gate = jnp.dot(small_ref[...].astype(BF16), wgate_ref[...],
                   preferred_element_type=F32) + bgate_ref[...]
    log_a = _log_sigmoid(gate) * (1.0 / GLA_TAU)

    pieces = jnp.concatenate([x.astype(BF16) for x in _split3(log_a)], axis=1)
    sums = jnp.dot(tri_ref[...].astype(BF16), pieces, preferred_element_type=F32)
    b = sums[:, 0:dk] + sums[:, dk:2 * dk] + sums[:, 2 * dk:3 * dk]
    b_tot = jnp.concatenate(
        [jnp.broadcast_to(b[c * C + C - 1:c * C + C, :], (C, dk)) for c in range(T // C)], axis=0)
    causal = tri_ref[...] > 0.0

    q = q_ref[...].astype(F32)
    k = k_ref[...].astype(F32)
    v = v_ref[...]
    q_dec = (q * jnp.exp(b)).astype(BF16)
    k_inv = (k * jnp.exp(-b)).astype(BF16)
    k_end = (k * jnp.exp(b_tot - b)).astype(BF16)
    a = lax.dot_general(q_dec, k_inv, (((1,), (1,)), ((), ())), preferred_element_type=F32)
    a = jnp.where(causal, a, 0.0).astype(BF16)
    o_intra = jnp.dot(a, v, preferred_element_type=F32)

    for ci in range(T // C):
        lo, hi = ci * C, (ci + 1) * C
        st = st_ref[...]
        o_c = o_intra[lo:hi, :] + lax.dot_general(
            q_dec[lo:hi, :], st.astype(BF16), (((1,), (1,)), ((), ())),
            preferred_element_type=F32)
        decay = jnp.exp(b_tot[lo:lo + 1, :])
        st_ref[...] = st * decay + lax.dot_general(
            v[lo:hi, :], k_end[lo:hi, :], (((0,), (0,)), ((), ())),
            preferred_element_type=F32)
        y = _rms(o_c, gnorm_ref[...])
        o_ref[lo:hi, :] = (y * _silu(gr_ref[lo:hi, :].astype(F32))).astype(o_ref.dtype)


def _chunk_causal_mask(n):
    r = jnp.arange(n)[:, None]
    c = jnp.arange(n)[None, :]
    return ((r // GLA_CHUNK == c // GLA_CHUNK) & (r >= c)).astype(F32)


def _gla(proj, small, wgate_pad, bgate, gnorm, *, bt):
    S = proj.shape[0]
    dk, dv = GLA_HEAD_K, GLA_HEAD_V
    return pl.pallas_call(
        _gla_kernel,
        name="gla",
        grid=(GLA_HEADS, S // bt),
        in_specs=[
            pl.BlockSpec((bt, dk), lambda h, t: (t, OFF_GQ // dk + h)),
            pl.BlockSpec((bt, dk), lambda h, t: (t, OFF_GK // dk + h)),
            pl.BlockSpec((bt, dv), lambda h, t: (t, OFF_GV // dv + h)),
            pl.BlockSpec((bt, dv), lambda h, t: (t, OFF_GR // dv + h)),
            pl.BlockSpec((bt, LANES), lambda h, t: (t, 0)),
            pl.BlockSpec((LANES, dk), lambda h, t: (0, h)),
            pl.BlockSpec((1, dk), lambda h, t: (0, h)),
            pl.BlockSpec((1, dv), lambda h, t: (0, 0)),
            pl.BlockSpec((bt, bt), lambda h, t: (0, 0)),
        ],
        out_specs=pl.BlockSpec((bt, dv), lambda h, t: (t, h)),
        out_shape=jax.ShapeDtypeStruct((S, GLA_WIDTH), BF16),
        scratch_shapes=[pltpu.VMEM((dv, dk), F32)],
        compiler_params=_params("parallel", "arbitrary"),
    )(proj, proj, proj, proj, small, wgate_pad, bgate, gnorm, _chunk_causal_mask(bt))


def _outproj_kernel(fox_ref, gla_ref, wt_ref, wb_ref, h_ref, g_ref, o_ref):
    n = pl.program_id(1)
    bn = wt_ref.shape[1]
    for sub in _sub_tiles(bn):
        cols = pl.ds(pl.multiple_of(n * bn + sub.start, MXU_COLS), MXU_COLS)
        o_ref[:, cols] = (jnp.dot(fox_ref[...], wt_ref[:, sub], preferred_element_type=F32)
                          + jnp.dot(gla_ref[...], wb_ref[:, sub], preferred_element_type=F32))

    @pl.when(n == pl.num_programs(1) - 1)
    def _epilogue():
        def emit(rows, y):
            o_ref[rows, :] = h_ref[rows, :] + y
        _rms_rows(o_ref, g_ref, emit)


def _outproj(o_fox, o_gla, w_o, h, g, *, bm, bn):
    S, D = h.shape
    half = o_fox.shape[1]
    return pl.pallas_call(
        _outproj_kernel,
        name="outproj",
        grid=(S // bm, D // bn),
        in_specs=[
            pl.BlockSpec((bm, half), lambda i, n: (i, 0)),
            pl.BlockSpec((bm, half), lambda i, n: (i, 0)),
            pl.BlockSpec((half, bn), lambda i, n: (0, n)),
            pl.BlockSpec((half, bn), lambda i, n: (1, n)),
            pl.BlockSpec((bm, D), lambda i, n: (i, 0)),
            pl.BlockSpec((1, D), lambda i, n: (0, 0)),
        ],
        out_specs=pl.BlockSpec((bm, D), lambda i, n: (i, 0)),
        out_shape=jax.ShapeDtypeStruct((S, D), F32),
        compiler_params=_params("parallel", "arbitrary"),
    )(o_fox, o_gla, w_o, w_o, h, g)


def _ple_kernel(h_ref, p_ref, wp_ref, g_ref, wg_ref, o_ref, hb_ref, e_ref):
    n = pl.program_id(1)
    bn = wg_ref.shape[1]

    @pl.when(n == 0)
    def _prologue():
        e_ref[...] = jnp.dot(p_ref[...].astype(BF16), wp_ref[...], preferred_element_type=F32)

        def emit(rows, y):
            hb_ref[rows, :] = h_ref[rows, :].astype(BF16)
            e_ref[rows, :] = y
        _rms_rows(e_ref, g_ref, emit)

    for sub in _sub_tiles(bn):
        cols = pl.ds(pl.multiple_of(n * bn + sub.start, MXU_COLS), MXU_COLS)
        gate = jax.nn.sigmoid(jnp.dot(hb_ref[...], wg_ref[:, sub], preferred_element_type=F32))
        o_ref[:, sub] = h_ref[:, cols] + e_ref[:, cols] * gate


def _ple(h, p, w_proj, g, w_gate, *, bm, bn):
    S, D = h.shape
    dp = p.shape[1]
    return pl.pallas_call(
        _ple_kernel,
        name="ple",
        grid=(S // bm, D // bn),
        in_specs=[
            pl.BlockSpec((bm, D), lambda i, n: (i, 0)),
            pl.BlockSpec((bm, dp), lambda i, n: (i, 0)),
            pl.BlockSpec((dp, D), lambda i, n: (0, 0)),
            pl.BlockSpec((1, D), lambda i, n: (0, 0)),
            pl.BlockSpec((D, bn), lambda i, n: (0, n)),
        ],
        out_specs=pl.BlockSpec((bm, bn), lambda i, n: (i, n)),
        out_shape=jax.ShapeDtypeStruct((S, D), F32),
        scratch_shapes=[pltpu.VMEM((bm, D), BF16), pltpu.VMEM((bm, D), F32)],
        compiler_params=_params("parallel", "arbitrary"),
    )(h, p, w_proj, g, w_gate)


def _tile(n, preferred):
    t = min(n, preferred)
    while n % t:
        t //= 2
    return t


def _split_w_in(w_in):
    D = w_in.shape[0]
    ff_lo = 3 * FOX_WIDTH
    ff_hi = ff_lo + FOX_HEADS
    glr_lo = ff_hi + 2 * GLA_KEY_WIDTH + 2 * GLA_WIDTH
    glr_hi = glr_lo + GLA_GATE_RANK
    w_all = w_in.astype(BF16)
    rest = w_all[:, ff_hi:glr_lo]
    pad = jnp.zeros((D, LANES - FOX_HEADS - GLA_GATE_RANK), BF16)
    small = jnp.concatenate([w_all[:, ff_lo:ff_hi], w_all[:, glr_lo:glr_hi], pad], axis=1)
    return w_all, rest, small


def kernel(x, p, ffn1_norm_pre, ffn1_w_gate, ffn1_w_up, ffn1_w_down, ffn1_norm_post, mix_norm_pre, w_in, fox_b_f, gla_w_gate, gla_b_gate, gla_norm_g, w_o, mix_norm_post, ffn2_norm_pre, ffn2_w_gate, ffn2_w_up, ffn2_w_down, ffn2_norm_post, ple_w_proj, ple_norm, ple_w_gate):
    B, S, D = x.shape
    assert B == 1, "the attention kernels treat the row axis as one sequence"
    depth = w_in.shape[0]
    h = x.reshape(S, D)

    bm = _tile(S, 512)
    bf = _tile(ffn1_w_gate.shape[-1], 256)
    bq = _tile(S, 512)
    bk = max(bq, _tile(S, 1024))
    bt = _tile(S, 512)

    col_scale = jnp.ones((1, MAIN_COLS), F32)
    col_scale = col_scale.at[:, OFF_FQ:OFF_FQ + FOX_WIDTH].set(FOX_HEAD_DIM ** -0.5 * LOG2_E)
    col_scale = col_scale.at[:, OFF_GQ:OFF_GQ + GLA_KEY_WIDTH].set(GLA_HEAD_K ** -0.5)

    def row(v):
        return v.reshape(1, -1).astype(F32)

    for i in range(depth):
        h = _ffn(h, row(ffn1_norm_pre[i]), ffn1_w_gate[i], ffn1_w_up[i], ffn1_w_down[i],
                 row(ffn1_norm_post[i]), bm=bm, bf=bf)

        w_all, w_rest, w_small = _split_w_in(w_in[i])
        proj, small = _inproj(h, row(mix_norm_pre[i]), w_all, w_rest, w_small, col_scale,
                              bm=_tile(S, 1024), bn=768)

        bias_f = jnp.zeros((1, LANES), F32).at[0, :FOX_HEADS].set(fox_b_f[i])
        c2 = _fox_gate(small, bias_f, bt=bt)
        o_fox = _fox(proj, c2, bq=bq, bk=bk)

        wgate_pad = jnp.zeros((LANES, GLA_KEY_WIDTH), F32).at[
            FOX_HEADS:FOX_HEADS + GLA_GATE_RANK, :].set(gla_w_gate[i]).astype(BF16)
        o_gla = _gla(proj, small, wgate_pad, row(gla_b_gate[i]), row(gla_norm_g[i]), bt=bt)

        h = _outproj(o_fox, o_gla, w_o[i].astype(BF16), h, row(mix_norm_post[i]), bm=bm, bn=512)

        h = _ffn(h, row(ffn2_norm_pre[i]), ffn2_w_gate[i], ffn2_w_up[i], ffn2_w_down[i],
                 row(ffn2_norm_post[i]), bm=bm, bf=bf)

        h = _ple(h, p[i].reshape(S, -1), ple_w_proj[i].astype(BF16), row(ple_norm[i]),
                 ple_w_gate[i].astype(BF16), bm=bm, bn=1024)
    return h.reshape(B, S, D)
```

```python
import functools

import jax
import jax.numpy as jnp
import numpy as np
from jax import lax
from jax.experimental import pallas as pl
from jax.experimental.pallas import tpu as pltpu

F32 = jnp.float32
BF16 = jnp.bfloat16
HIGHEST = lax.Precision.HIGHEST

EPS = 1e-6
LOG2_E = 1.4426950408889634
MACARON_WEIGHT = 0.5
FOX_HEAD_DIM = 128
FOX_HEADS = 16
FOX_WIDTH = FOX_HEADS * FOX_HEAD_DIM
GLA_HEADS = 4
GLA_WIDTH = 2048
GLA_HEAD_V = GLA_WIDTH // GLA_HEADS
GLA_KEY_WIDTH = GLA_WIDTH // 2
GLA_HEAD_K = GLA_KEY_WIDTH // GLA_HEADS
GLA_GATE_RANK = 16
GLA_TAU = 16.0
GLA_CHUNK = 64

LANES = 128
MXU_COLS = 256
VMEM_LIMIT_BYTES = 56 * 1024 * 1024

MAIN_COLS = 3 * FOX_WIDTH + 2 * GLA_WIDTH + 2 * GLA_KEY_WIDTH
OFF_FQ = 0
OFF_FK = FOX_WIDTH
OFF_FV = 2 * FOX_WIDTH
OFF_GQ = 3 * FOX_WIDTH
OFF_GK = OFF_GQ + GLA_KEY_WIDTH
OFF_GV = OFF_GK + GLA_KEY_WIDTH
OFF_GR = OFF_GV + GLA_WIDTH

ROW_CHUNK = 32
ROW_BLOCK = 128


def _params(*semantics):
    return pltpu.CompilerParams(dimension_semantics=semantics,
                                vmem_limit_bytes=VMEM_LIMIT_BYTES)


def _rms(x, g):
    ms = jnp.mean(x * x, axis=-1, keepdims=True)
    return x * lax.rsqrt(ms + EPS) * g


def _log_sigmoid(x):
    return jnp.minimum(x, 0.0) - jnp.log(1.0 + jnp.exp(-jnp.abs(x)))


def _silu(x):
    return x * jax.nn.sigmoid(x)


def _sub_tiles(n_cols):
    return [slice(c, c + MXU_COLS) for c in range(0, n_cols, MXU_COLS)]


def _rms_rows(src_ref, g_ref, emit):
    def body(b, carry):
        chunks = [pl.ds(pl.multiple_of(b * ROW_BLOCK + c * ROW_CHUNK, ROW_CHUNK), ROW_CHUNK)
                  for c in range(ROW_BLOCK // ROW_CHUNK)]
        scales = []
        for rows in chunks:
            x = src_ref[rows, :]
            scales.append(lax.rsqrt(jnp.mean(x * x, axis=-1, keepdims=True) + EPS))
        for rows, scale in zip(chunks, scales):
            emit(rows, src_ref[rows, :] * scale * g_ref[...])
        return carry
    lax.fori_loop(0, src_ref.shape[0] // ROW_BLOCK, body, 0)


def _ffn_prologue(x_ref, gpre_ref, xn_ref, o_ref):
    def emit(rows, y):
        xn_ref[rows, :] = y.astype(BF16)
        o_ref[rows, :] = jnp.zeros((ROW_CHUNK, o_ref.shape[1]), F32)
    _rms_rows(x_ref, gpre_ref, emit)


def _ffn_epilogue(x_ref, gpost_ref, o_ref):
    def emit(rows, y):
        o_ref[rows, :] = x_ref[rows, :] + MACARON_WEIGHT * y
    _rms_rows(o_ref, gpost_ref, emit)


FFN_COPY_STEPS = 4


def _ffn_kernel(row_tab_ref, ff_tab_ref, x_ref, gpre_ref, wg_ref, wu_ref, wd_ref, gpost_ref,
                first_ref, o_ref, xn_ref, *, nf):
    del row_tab_ref
    s = pl.program_id(0)
    rows_c = first_ref.shape[0]

    @pl.when(s < FFN_COPY_STEPS)
    def _():
        o_ref[pl.ds(pl.multiple_of(s * rows_c, rows_c), rows_c), :] = first_ref[...]

    @pl.when(s >= FFN_COPY_STEPS)
    def _():
        f = ff_tab_ref[s]

        @pl.when(f == 0)
        def _():
            _ffn_prologue(x_ref, gpre_ref, xn_ref, o_ref)

        xn = xn_ref[...]
        g = jnp.dot(xn, wg_ref[...], preferred_element_type=F32)
        u = jnp.dot(xn, wu_ref[...], preferred_element_type=F32)
        hid = (_silu(g) * u).astype(BF16)
        o_ref[...] += jnp.dot(hid, wd_ref[...], preferred_element_type=F32)

        @pl.when(f == nf - 1)
        def _():
            _ffn_epilogue(x_ref, gpost_ref, o_ref)


def _ffn_first_kernel(x_ref, gpre_ref, wg_ref, wu_ref, wd_ref, gpost_ref,
                      o_ref, wg16_ref, wu16_ref, wd16_ref, xn_ref, g_ref, u_ref):
    f = pl.program_id(0)
    k = pl.program_id(1)
    dk = wg_ref.shape[0]

    @pl.when(jnp.logical_and(f == 0, k == 0))
    def _():
        _ffn_prologue(x_ref, gpre_ref, xn_ref, o_ref)

    wg = wg_ref[...].astype(BF16)
    wu = wu_ref[...].astype(BF16)
    wg16_ref[...] = wg
    wu16_ref[...] = wu
    xk = xn_ref[:, pl.ds(pl.multiple_of(k * dk, dk), dk)]
    g = jnp.dot(xk, wg, preferred_element_type=F32)
    u = jnp.dot(xk, wu, preferred_element_type=F32)

    @pl.when(k == 0)
    def _():
        g_ref[...] = g
        u_ref[...] = u

    @pl.when(k == 1)
    def _():
        wd = wd_ref[...].astype(BF16)
        wd16_ref[...] = wd
        hid = (_silu(g_ref[...] + g) * (u_ref[...] + u)).astype(BF16)
        o_ref[...] += jnp.dot(hid, wd, preferred_element_type=F32)

    @pl.when(jnp.logical_and(f == pl.num_programs(0) - 1, k == 1))
    def _():
        _ffn_epilogue(x_ref, gpost_ref, o_ref)


def _ffn(x, g_pre, wg, wu, wd, g_post, *, bm, bf):
    S, D = x.shape
    d_ff = wg.shape[1]
    once = dict(pipeline_mode=pl.Buffered(1))
    h, wg16, wu16, wd16 = pl.pallas_call(
        _ffn_first_kernel,
        name="ffn_first",
        grid=(d_ff // bf, 2),
        in_specs=[
            pl.BlockSpec((bm, D), lambda f, k: (0, 0), **once),
            pl.BlockSpec((1, D), lambda f, k: (0, 0)),
            pl.BlockSpec((D // 2, bf), lambda f, k: (k, f)),
            pl.BlockSpec((D // 2, bf), lambda f, k: (k, f)),
            pl.BlockSpec((bf, D), lambda f, k: (f, 0)),
            pl.BlockSpec((1, D), lambda f, k: (0, 0)),
        ],
        out_specs=[
            pl.BlockSpec((bm, D), lambda f, k: (0, 0)),
            pl.BlockSpec((D // 2, bf), lambda f, k: (k, f)),
            pl.BlockSpec((D // 2, bf), lambda f, k: (k, f)),
            pl.BlockSpec((bf, D), lambda f, k: (f, 0)),
        ],
        out_shape=[
            jax.ShapeDtypeStruct((bm, D), F32),
            jax.ShapeDtypeStruct((D, d_ff), BF16),
            jax.ShapeDtypeStruct((D, d_ff), BF16),
            jax.ShapeDtypeStruct((d_ff, D), BF16),
        ],
        scratch_shapes=[pltpu.VMEM((bm, D), BF16), pltpu.VMEM((bm, bf), F32), pltpu.VMEM((bm, bf), F32)],
        compiler_params=_params("arbitrary", "arbitrary"),
    )(x, g_pre, wg, wu, wd, g_post)
    if S == bm:
        return h
    nf = d_ff // bf
    c = FFN_COPY_STEPS

    steps = np.arange(c + (S // bm - 1) * nf)
    row_tab = np.where(steps < c, 0, 1 + (steps - c) // nf).astype(np.int32)
    ff_tab = np.where(steps < c, 0, (steps - c) % nf).astype(np.int32)
    grid_spec = pltpu.PrefetchScalarGridSpec(
        num_scalar_prefetch=2,
        grid=(steps.size,),
        in_specs=[
            pl.BlockSpec((bm, D), lambda s, rt, ft: (rt[s], 0)),
            pl.BlockSpec((1, D), lambda s, rt, ft: (0, 0)),
            pl.BlockSpec((D, bf), lambda s, rt, ft: (0, ft[s])),
            pl.BlockSpec((D, bf), lambda s, rt, ft: (0, ft[s])),
            pl.BlockSpec((bf, D), lambda s, rt, ft: (ft[s], 0)),
            pl.BlockSpec((1, D), lambda s, rt, ft: (0, 0)),
            pl.BlockSpec((bm // c, D), lambda s, rt, ft: (jnp.minimum(s, c - 1), 0)),
        ],
        out_specs=pl.BlockSpec((bm, D), lambda s, rt, ft: (rt[s], 0)),
        scratch_shapes=[pltpu.VMEM((bm, D), BF16)],
    )
    return pl.pallas_call(
        functools.partial(_ffn_kernel, nf=nf),
        name="ffn",
        grid_spec=grid_spec,
        out_shape=jax.ShapeDtypeStruct((S, D), F32),
        compiler_params=_params("arbitrary"),
    )(jnp.asarray(row_tab), jnp.asarray(ff_tab), x, g_pre, wg16, wu16, wd16, g_post, h)


def _inproj_kernel(h_ref, g_ref, w1_ref, w2_ref, ws_ref, scale_ref, o_ref, os_ref, a_ref, *, n1):
    n = pl.program_id(1)

    @pl.when(n == 0)
    def _prologue():
        def emit(rows, y):
            a_ref[rows, :] = y.astype(BF16)
        _rms_rows(h_ref, g_ref, emit)
        os_ref[...] = jnp.dot(a_ref[...], ws_ref[...], preferred_element_type=F32)

    def tile(w_ref):
        for cols in _sub_tiles(w_ref.shape[1]):
            acc = jnp.dot(a_ref[...], w_ref[:, cols], preferred_element_type=F32)
            o_ref[:, cols] = (acc * scale_ref[:, cols]).astype(BF16)

    @pl.when(n < n1)
    def _():
        tile(w1_ref)

    @pl.when(n >= n1)
    def _():
        tile(w2_ref)


def _inproj(h, g, w_all, w_rest, w_small, col_scale, *, bm, bn):
    S, D = h.shape
    n_cols = col_scale.shape[1]
    n1 = (n_cols - w_rest.shape[1]) // bn
    return pl.pallas_call(
        functools.partial(_inproj_kernel, n1=n1),
        name="inproj",
        grid=(S // bm, n_cols // bn),
        in_specs=[
            pl.BlockSpec((bm, D), lambda i, n: (i, 0), pipeline_mode=pl.Buffered(1)),
            pl.BlockSpec((1, D), lambda i, n: (0, 0)),
            pl.BlockSpec((D, bn), lambda i, n: (0, jnp.minimum(n, n1 - 1))),
            pl.BlockSpec((D, bn), lambda i, n: (0, jnp.maximum(n - n1, 0))),
            pl.BlockSpec((D, LANES), lambda i, n: (0, 0)),
            pl.BlockSpec((1, bn), lambda i, n: (0, n)),
        ],
        out_specs=[
            pl.BlockSpec((bm, bn), lambda i, n: (i, n)),
            pl.BlockSpec((bm, LANES), lambda i, n: (i, 0)),
        ],
        out_shape=[
            jax.ShapeDtypeStruct((S, n_cols), BF16),
            jax.ShapeDtypeStruct((S, LANES), F32),
        ],
        scratch_shapes=[pltpu.VMEM((bm, D), BF16)],
        compiler_params=_params("parallel", "arbitrary"),
    )(h, g, w_all, w_rest, w_small, col_scale)


def _fox_gate_kernel(small_ref, bias_ref, c_ref, carry_ref):
    t = pl.program_id(0)
    T = small_ref.shape[0]

    @pl.when(t == 0)
    def _init():
        carry_ref[...] = jnp.zeros_like(carry_ref)

    log_f = _log_sigmoid(small_ref[...] + bias_ref[...])
    row = lax.broadcasted_iota(jnp.int32, (T, T), 0)
    col = lax.broadcasted_iota(jnp.int32, (T, T), 1)
    tri = jnp.where(row >= col, 1.0, 0.0).astype(F32)
    c = jnp.dot(tri, log_f, precision=HIGHEST, preferred_element_type=F32) + carry_ref[...]
    c_ref[...] = c * LOG2_E
    carry_ref[...] = c[T - 1:T, :]


def _fox_gate(small, bias, *, bt):
    S = small.shape[0]
    return pl.pallas_call(
        _fox_gate_kernel,
        name="fox_gate",
        grid=(S // bt,),
        in_specs=[
            pl.BlockSpec((bt, LANES), lambda t: (t, 0)),
            pl.BlockSpec((1, LANES), lambda t: (0, 0)),
        ],
        out_specs=pl.BlockSpec((bt, LANES), lambda t: (t, 0)),
        out_shape=jax.ShapeDtypeStruct((S, LANES), F32),
        scratch_shapes=[pltpu.VMEM((1, LANES), F32)],
        compiler_params=_params("arbitrary"),
    )(small, bias)


FOX_PAIR = 2
FOX_ROWS = 16


def _split3(c):
    hi = c.astype(BF16).astype(F32)
    mid = (c - hi).astype(BF16).astype(F32)
    lo = (c - hi - mid).astype(BF16).astype(F32)
    return hi, mid, lo


def _lane_tile(x, n):
    return jnp.concatenate([x] * n, axis=1)


def _head_column(block, head):
    lane = lax.broadcasted_iota(jnp.int32, block.shape, 1)
    return jnp.sum(jnp.where(lane == head, block, 0.0), axis=1, keepdims=True)


def _fox_kernel(q_ref, k_ref, v_ref, c_ref, o_ref, kaug_ref, vaug_ref, qaug_ref, s_ref, p_ref,
                m_ref, alpha_ref, acc_ref, *, bk):
    hp = pl.program_id(0)
    i = pl.program_id(1)
    bq = q_ref.shape[0]
    S = k_ref.shape[0]
    dh = FOX_HEAD_DIM
    R = FOX_ROWS

    @pl.when(i == 0)
    def _build_kv_side():
        rows_per = 128

        def body(r, carry):
            rows = pl.ds(pl.multiple_of(r * rows_per, rows_per), rows_per)
            cblk = c_ref[rows, :]
            lane = lax.broadcasted_iota(jnp.int32, (rows_per, LANES), 1)
            for hh in range(FOX_PAIR):
                hi, mid, lo = _split3(_head_column(cblk, hp * FOX_PAIR + hh))
                extra = jnp.where(lane < 3, 1.0,
                                  jnp.where(lane == 3, -hi,
                                            jnp.where(lane == 4, -mid,
                                                      jnp.where(lane == 5, -lo, 0.0))))
                kaug_ref[hh, rows, 0:dh] = k_ref[rows, hh * dh:(hh + 1) * dh]
                kaug_ref[hh, rows, dh:2 * dh] = extra.astype(BF16)
                vaug_ref[hh, rows, 0:dh] = v_ref[rows, hh * dh:(hh + 1) * dh]
                vaug_ref[hh, rows, dh:2 * dh] = jnp.ones((rows_per, dh), BF16)
            return carry
        lax.fori_loop(0, S // rows_per, body, 0, unroll=4)

    cq_blk = c_ref[pl.ds(pl.multiple_of(i * bq, bq), bq), :]
    lane_q = lax.broadcasted_iota(jnp.int32, (bq, LANES), 1)
    for hh in range(FOX_PAIR):
        hi, mid, lo = _split3(_head_column(cq_blk, hp * FOX_PAIR + hh))
        extra = jnp.where(lane_q == 0, hi,
                          jnp.where(lane_q == 1, mid,
                                    jnp.where(lane_q == 2, lo,
                                              jnp.where(lane_q < 6, 1.0, 0.0))))
        qaug_ref[hh, :, 0:dh] = q_ref[:, hh * dh:(hh + 1) * dh]
        qaug_ref[hh, :, dh:2 * dh] = extra.astype(BF16)
        m_ref[hh] = jnp.full((bq, LANES), -jnp.inf, F32)
        acc_ref[hh] = jnp.zeros((bq, 2 * dh), F32)

    groups = [slice(g * R, (g + 1) * R) for g in range(bq // R)]

    def chunk(k0, width, masked):
        ks = pl.ds(pl.multiple_of(k0, bq), width)
        for hh in range(FOX_PAIR):
            s_ref[hh, :, 0:width] = lax.dot_general(
                qaug_ref[hh], kaug_ref[hh, ks, :], (((1,), (1,)), ((), ())),
                preferred_element_type=F32)
        if masked:
            diff = (lax.broadcasted_iota(jnp.int32, (R, width), 1)
                    - lax.broadcasted_iota(jnp.int32, (R, width), 0))
            limit = i * bq - k0
        for hh in range(FOX_PAIR):
            for g, rows in enumerate(groups):
                s = s_ref[hh, rows, 0:width]
                if masked:
                    s = jnp.where(diff <= limit + g * R, s, -jnp.inf)
                    s_ref[hh, rows, 0:width] = s
                m_old = m_ref[hh, rows, :]
                m_new = jnp.maximum(m_old, jnp.broadcast_to(
                    jnp.max(s, axis=1, keepdims=True), (R, LANES)))
                alpha_ref[hh, rows, :] = jnp.exp2(m_old - m_new)
                m_ref[hh, rows, :] = m_new
        for hh in range(FOX_PAIR):
            for rows in groups:
                m = _lane_tile(m_ref[hh, rows, :], width // LANES)
                p_ref[hh, rows, 0:width] = jnp.exp2(s_ref[hh, rows, 0:width] - m).astype(BF16)
        for hh in range(FOX_PAIR):
            alpha = _lane_tile(alpha_ref[hh], 2)
            acc_ref[hh] = alpha * acc_ref[hh] + jnp.dot(
                p_ref[hh, :, 0:width], vaug_ref[hh, ks, :], preferred_element_type=F32)

    n_full = (i * bq) // bk

    def full_body(j, carry):
        chunk(j * bk, bk, False)
        return carry
    lax.fori_loop(0, n_full, full_body, 0)

    tail_start = n_full * bk
    tail_blocks = (i + 1) - n_full * (bk // bq)
    for t in range(1, bk // bq + 1):
        @pl.when(tail_blocks == t)
        def _tail(t=t):
            chunk(tail_start, t * bq, True)

    for hh in range(FOX_PAIR):
        acc = acc_ref[hh]
        o_ref[:, hh * dh:(hh + 1) * dh] = (acc[:, 0:dh] / acc[:, dh:2 * dh]).astype(o_ref.dtype)


def _fox(proj, c2, *, bq, bk):
    S = proj.shape[0]
    dh = FOX_HEAD_DIM
    w = FOX_PAIR * dh
    return pl.pallas_call(
        functools.partial(_fox_kernel, bk=bk),
        name="fox",
        grid=(FOX_HEADS // FOX_PAIR, S // bq),
        in_specs=[
            pl.BlockSpec((bq, w), lambda h, i: (i, OFF_FQ // w + h)),
            pl.BlockSpec((S, w), lambda h, i: (0, OFF_FK // w + h)),
            pl.BlockSpec((S, w), lambda h, i: (0, OFF_FV // w + h)),
            pl.BlockSpec((S, LANES), lambda h, i: (0, 0)),
        ],
        out_specs=pl.BlockSpec((bq, w), lambda h, i: (i, h)),
        out_shape=jax.ShapeDtypeStruct((S, FOX_WIDTH), BF16),
        scratch_shapes=[
            pltpu.VMEM((FOX_PAIR, S, 2 * dh), BF16),
            pltpu.VMEM((FOX_PAIR, S, 2 * dh), BF16),
            pltpu.VMEM((FOX_PAIR, bq, 2 * dh), BF16),
            pltpu.VMEM((FOX_PAIR, bq, bk), F32),
            pltpu.VMEM((FOX_PAIR, bq, bk), BF16),
            pltpu.VMEM((FOX_PAIR, bq, LANES), F32),
            pltpu.VMEM((FOX_PAIR, bq, LANES), F32),
            pltpu.VMEM((FOX_PAIR, bq, 2 * dh), F32),
        ],
        compiler_params=_params("parallel", "arbitrary"),
    )(proj, proj, proj, c2)


def _gla_kernel(q_ref, k_ref, v_ref, gr_ref, small_ref, wgate_ref, bgate_ref,
                gnorm_ref, tri_ref, o_ref, st_ref):
    t = pl.program_id(1)
    T = q_ref.shape[0]
    C = GLA_CHUNK

    @pl.when(t == 0)
    def _init():
        st_ref[...] = jnp.zeros_like(st_ref)

    dk = q_ref.shape[1]
    gate = jnp.dot(small_ref[...].astype(BF16), wgate_ref[...],
                   preferred_element_type=F32) + bgate_ref[...]
    log_a = _log_sigmoid(gate) * (1.0 / GLA_TAU)

    pieces = jnp.concatenate([x.astype(BF16) for x in _split3(log_a)], axis=1)
    sums = jnp.dot(tri_ref[...].astype(BF16), pieces, preferred_element_type=F32)
    b = sums[:, 0:dk] + sums[:, dk:2 * dk] + sums[:, 2 * dk:3 * dk]
    b_tot = jnp.concatenate(
        [jnp.broadcast_to(b[c * C + C - 1:c * C + C, :], (C, dk)) for c in range(T // C)], axis=0)
    causal = tri_ref[...] > 0.0

    q = q_ref[...].astype(F32)
    k = k_ref[...].astype(F32)
    v = v_ref[...]
    q_dec = (q * jnp.exp(b)).astype(BF16)
    k_inv = (k * jnp.exp(-b)).astype(BF16)
    k_end = (k * jnp.exp(b_tot - b)).astype(BF16)
    a = lax.dot_general(q_dec, k_inv, (((1,), (1,)), ((), ())), preferred_element_type=F32)
    a = jnp.where(causal, a, 0.0).astype(BF16)
    o_intra = jnp.dot(a, v, preferred_element_type=F32)

    for ci in range(T // C):
        lo, hi = ci * C, (ci + 1) * C
        st = st_ref[...]
        o_c = o_intra[lo:hi, :] + lax.dot_general(
            q_dec[lo:hi, :], st.astype(BF16), (((1,), (1,)), ((), ())),
            preferred_element_type=F32)
        decay = jnp.exp(b_tot[lo:lo + 1, :])
        st_ref[...] = st * decay + lax.dot_general(
            v[lo:hi, :], k_end[lo:hi, :], (((0,), (0,)), ((), ())),
            preferred_element_type=F32)
        y = _rms(o_c, gnorm_ref[...])
        o_ref[lo:hi, :] = (y * _silu(gr_ref[lo:hi, :].astype(F32))).astype(o_ref.dtype)


def _chunk_causal_mask(n):
    r = jnp.arange(n)[:, None]
    c = jnp.arange(n)[None, :]
    return ((r // GLA_CHUNK == c // GLA_CHUNK) & (r >= c)).astype(F32)


def _gla(proj, small, wgate_pad, bgate, gnorm, *, bt):
    S = proj.shape[0]
    dk, dv = GLA_HEAD_K, GLA_HEAD_V
    return pl.pallas_call(
        _gla_kernel,
        name="gla",
        grid=(GLA_HEADS, S // bt),
        in_specs=[
            pl.BlockSpec((bt, dk), lambda h, t: (t, OFF_GQ // dk + h)),
            pl.BlockSpec((bt, dk), lambda h, t: (t, OFF_GK // dk + h)),
            pl.BlockSpec((bt, dv), lambda h, t: (t, OFF_GV // dv + h)),
            pl.BlockSpec((bt, dv), lambda h, t: (t, OFF_GR // dv + h)),
            pl.BlockSpec((bt, LANES), lambda h, t: (t, 0)),
            pl.BlockSpec((LANES, dk), lambda h, t: (0, h)),
            pl.BlockSpec((1, dk), lambda h, t: (0, h)),
            pl.BlockSpec((1, dv), lambda h, t: (0, 0)),
            pl.BlockSpec((bt, bt), lambda h, t: (0, 0)),
        ],
        out_specs=pl.BlockSpec((bt, dv), lambda h, t: (t, h)),
        out_shape=jax.ShapeDtypeStruct((S, GLA_WIDTH), BF16),
        scratch_shapes=[pltpu.VMEM((dv, dk), F32)],
        compiler_params=_params("parallel", "arbitrary"),
    )(proj, proj, proj, proj, small, wgate_pad, bgate, gnorm, _chunk_causal_mask(bt))


def _outproj_kernel(fox_ref, gla_ref, wt_ref, wb_ref, h_ref, g_ref, o_ref):
    n = pl.program_id(1)
    bn = wt_ref.shape[1]
    for sub in _sub_tiles(bn):
        cols = pl.ds(pl.multiple_of(n * bn + sub.start, MXU_COLS), MXU_COLS)
        o_ref[:, cols] = (jnp.dot(fox_ref[...], wt_ref[:, sub], preferred_element_type=F32)
                          + jnp.dot(gla_ref[...], wb_ref[:, sub], preferred_element_type=F32))

    @pl.when(n == pl.num_programs(1) - 1)
    def _epilogue():
        def emit(rows, y):
            o_ref[rows, :] = h_ref[rows, :] + y
        _rms_rows(o_ref, g_ref, emit)


def _outproj(o_fox, o_gla, w_o, h, g, *, bm, bn):
    S, D = h.shape
    half = o_fox.shape[1]
    return pl.pallas_call(
        _outproj_kernel,
        name="outproj",
        grid=(S // bm, D // bn),
        in_specs=[
            pl.BlockSpec((bm, half), lambda i, n: (i, 0)),
            pl.BlockSpec((bm, half), lambda i, n: (i, 0)),
            pl.BlockSpec((half, bn), lambda i, n: (0, n)),
            pl.BlockSpec((half, bn), lambda i, n: (1, n)),
            pl.BlockSpec((bm, D), lambda i, n: (i, 0)),
            pl.BlockSpec((1, D), lambda i, n: (0, 0)),
        ],
        out_specs=pl.BlockSpec((bm, D), lambda i, n: (i, 0)),
        out_shape=jax.ShapeDtypeStruct((S, D), F32),
        compiler_params=_params("parallel", "arbitrary"),
    )(o_fox, o_gla, w_o, w_o, h, g)


def _ple_kernel(h_ref, p_ref, wp_ref, g_ref, wg_ref, o_ref, hb_ref, e_ref):
    n = pl.program_id(1)
    bn = wg_ref.shape[1]

    @pl.when(n == 0)
    def _prologue():
        e_ref[...] = jnp.dot(p_ref[...].astype(BF16), wp_ref[...], preferred_element_type=F32)

        def emit(rows, y):
            hb_ref[rows, :] = h_ref[rows, :].astype(BF16)
            e_ref[rows, :] = y
        _rms_rows(e_ref, g_ref, emit)

    for sub in _sub_tiles(bn):
        cols = pl.ds(pl.multiple_of(n * bn + sub.start, MXU_COLS), MXU_COLS)
        gate = jax.nn.sigmoid(jnp.dot(hb_ref[...], wg_ref[:, sub], preferred_element_type=F32))
        o_ref[:, sub] = h_ref[:, cols] + e_ref[:, cols] * gate


def _ple(h, p, w_proj, g, w_gate, *, bm, bn):
    S, D = h.shape
    dp = p.shape[1]
    return pl.pallas_call(
        _ple_kernel,
        name="ple",
        grid=(S // bm, D // bn),
        in_specs=[
            pl.BlockSpec((bm, D), lambda i, n: (i, 0)),
            pl.BlockSpec((bm, dp), lambda i, n: (i, 0)),
            pl.BlockSpec((dp, D), lambda i, n: (0, 0)),
            pl.BlockSpec((1, D), lambda i, n: (0, 0)),
            pl.BlockSpec((D, bn), lambda i, n: (0, n)),
        ],
        out_specs=pl.BlockSpec((bm, bn), lambda i, n: (i, n)),
        out_shape=jax.ShapeDtypeStruct((S, D), F32),
        scratch_shapes=[pltpu.VMEM((bm, D), BF16), pltpu.VMEM((bm, D), F32)],
        compiler_params=_params("parallel", "arbitrary"),
    )(h, p, w_proj, g, w_gate)


def _tile(n, preferred):
    t = min(n, preferred)
    while n % t:
        t //= 2
    return t


def _split_w_in(w_in):
    D = w_in.shape[0]
    ff_lo = 3 * FOX_WIDTH
    ff_hi = ff_lo + FOX_HEADS
    glr_lo = ff_hi + 2 * GLA_KEY_WIDTH + 2 * GLA_WIDTH
    glr_hi = glr_lo + GLA_GATE_RANK
    w_all = w_in.astype(BF16)
    rest = w_all[:, ff_hi:glr_lo]
    pad = jnp.zeros((D, LANES - FOX_HEADS - GLA_GATE_RANK), BF16)
    small = jnp.concatenate([w_all[:, ff_lo:ff_hi], w_all[:, glr_lo:glr_hi], pad], axis=1)
    return w_all, rest, small


def kernel(x, p, ffn1_norm_pre, ffn1_w_gate, ffn1_w_up, ffn1_w_down, ffn1_norm_post, mix_norm_pre, w_in, fox_b_f, gla_w_gate, gla_b_gate, gla_norm_g, w_o, mix_norm_post, ffn2_norm_pre, ffn2_w_gate, ffn2_w_up, ffn2_w_down, ffn2_norm_post, ple_w_proj, ple_norm, ple_w_gate):
    B, S, D = x.shape
    assert B == 1, "the attention kernels treat the row axis as one sequence"
    depth = w_in.shape[0]
    h = x.reshape(S, D)

    bm = _tile(S, 512)
    bf = _tile(ffn1_w_gate.shape[-1], 256)
    bq = _tile(S, 512)
    bk = max(bq, _tile(S, 1024))
    bt = _tile(S, 512)

    col_scale = jnp.ones((1, MAIN_COLS), F32)
    col_scale = col_scale.at[:, OFF_FQ:OFF_FQ + FOX_WIDTH].set(FOX_HEAD_DIM ** -0.5 * LOG2_E)
    col_scale = col_scale.at[:, OFF_GQ:OFF_GQ + GLA_KEY_WIDTH].set(GLA_HEAD_K ** -0.5)

    def row(v):
        return v.reshape(1, -1).astype(F32)

    for i in range(depth):
        h = _ffn(h, row(ffn1_norm_pre[i]), ffn1_w_gate[i], ffn1_w_up[i], ffn1_w_down[i],
                 row(ffn1_norm_post[i]), bm=bm, bf=bf)

        w_all, w_rest, w_small = _split_w_in(w_in[i])
        proj, small = _inproj(h, row(mix_norm_pre[i]), w_all, w_rest, w_small, col_scale,
                              bm=_tile(S, 1024), bn=768)

        bias_f = jnp.zeros((1, LANES), F32).at[0, :FOX_HEADS].set(fox_b_f[i])
        c2 = _fox_gate(small, bias_f, bt=bt)
        o_fox = _fox(proj, c2, bq=bq, bk=bk)

        wgate_pad = jnp.zeros((LANES, GLA_KEY_WIDTH), F32).at[
            FOX_HEADS:FOX_HEADS + GLA_GATE_RANK, :].set(gla_w_gate[i]).astype(BF16)
        o_gla = _gla(proj, small, wgate_pad, row(gla_b_gate[i]), row(gla_norm_g[i]), bt=bt)

        h = _outproj(o_fox, o_gla, w_o[i].astype(BF16), h, row(mix_norm_post[i]), bm=bm, bn=512)

        h = _ffn(h, row(ffn2_norm_pre[i]), ffn2_w_gate[i], ffn2_w_up[i], ffn2_w_down[i],
                 row(ffn2_norm_post[i]), bm=bm, bf=bf)

        h = _ple(h, p[i].reshape(S, -1), ple_w_proj[i].astype(BF16), row(ple_norm[i]),
                 ple_w_gate[i].astype(BF16), bm=bm, bn=1024)
    return h.reshape(B, S, D)
```

```python
import functools

import jax
import jax.numpy as jnp
import numpy as np
from jax import lax
from jax.experimental import pallas as pl
from jax.experimental.pallas import tpu as pltpu

F32 = jnp.float32
BF16 = jnp.bfloat16
HIGHEST = lax.Precision.HIGHEST

EPS = 1e-6
LOG2_E = 1.4426950408889634
MACARON_WEIGHT = 0.5
FOX_HEAD_DIM = 128
FOX_HEADS = 16
FOX_WIDTH = FOX_HEADS * FOX_HEAD_DIM
GLA_HEADS = 4
GLA_WIDTH = 2048
GLA_HEAD_V = GLA_WIDTH // GLA_HEADS
GLA_KEY_WIDTH = GLA_WIDTH // 2
GLA_HEAD_K = GLA_KEY_WIDTH // GLA_HEADS
GLA_GATE_RANK = 16
GLA_TAU = 16.0
GLA_CHUNK = 64

LANES = 128
MXU_COLS = 256
VMEM_LIMIT_BYTES = 56 * 1024 * 1024

MAIN_COLS = 3 * FOX_WIDTH + 2 * GLA_WIDTH + 2 * GLA_KEY_WIDTH
OFF_FQ = 0
OFF_FK = FOX_WIDTH
OFF_FV = 2 * FOX_WIDTH
OFF_GQ = 3 * FOX_WIDTH
OFF_GK = OFF_GQ + GLA_KEY_WIDTH
OFF_GV = OFF_GK + GLA_KEY_WIDTH
OFF_GR = OFF_GV + GLA_WIDTH

ROW_CHUNK = 32
ROW_BLOCK = 128


def _params(*semantics):
    return pltpu.CompilerParams(dimension_semantics=semantics,
                                vmem_limit_bytes=VMEM_LIMIT_BYTES)


def _rms(x, g):
    ms = jnp.mean(x * x, axis=-1, keepdims=True)
    return x * lax.rsqrt(ms + EPS) * g


def _log_sigmoid(x):
    return jnp.minimum(x, 0.0) - jnp.log(1.0 + jnp.exp(-jnp.abs(x)))


def _silu(x):
    return x * jax.nn.sigmoid(x)


def _sub_tiles(n_cols):
    return [slice(c, c + MXU_COLS) for c in range(0, n_cols, MXU_COLS)]


def _rms_rows(src_ref, g_ref, emit):
    def body(b, carry):
        chunks = [pl.ds(pl.multiple_of(b * ROW_BLOCK + c * ROW_CHUNK, ROW_CHUNK), ROW_CHUNK)
                  for c in range(ROW_BLOCK // ROW_CHUNK)]
        scales = []
        for rows in chunks:
            x = src_ref[rows, :]
            scales.append(lax.rsqrt(jnp.mean(x * x, axis=-1, keepdims=True) + EPS))
        for rows, scale in zip(chunks, scales):
            emit(rows, src_ref[rows, :] * scale * g_ref[...])
        return carry
    lax.fori_loop(0, src_ref.shape[0] // ROW_BLOCK, body, 0)


def _ffn_prologue(x_ref, gpre_ref, xn_ref, o_ref):
    def emit(rows, y):
        xn_ref[rows, :] = y.astype(BF16)
        o_ref[rows, :] = jnp.zeros((ROW_CHUNK, o_ref.shape[1]), F32)
    _rms_rows(x_ref, gpre_ref, emit)


def _ffn_epilogue(x_ref, gpost_ref, o_ref):
    def emit(rows, y):
        o_ref[rows, :] = x_ref[rows, :] + MACARON_WEIGHT * y
    _rms_rows(o_ref, gpost_ref, emit)


FFN_COPY_STEPS = 4


def _ffn_kernel(row_tab_ref, ff_tab_ref, x_ref, gpre_ref, wg_ref, wu_ref, wd_ref, gpost_ref,
                first_ref, o_ref, xn_ref, *, nf):
    del row_tab_ref
    s = pl.program_id(0)
    rows_c = first_ref.shape[0]

    @pl.when(s < FFN_COPY_STEPS)
    def _():
        o_ref[pl.ds(pl.multiple_of(s * rows_c, rows_c), rows_c), :] = first_ref[...]

    @pl.when(s >= FFN_COPY_STEPS)
    def _():
        f = ff_tab_ref[s]

        @pl.when(f == 0)
        def _():
            _ffn_prologue(x_ref, gpre_ref, xn_ref, o_ref)

        xn = xn_ref[...]
        g = jnp.dot(xn, wg_ref[...], preferred_element_type=F32)
        u = jnp.dot(xn, wu_ref[...], preferred_element_type=F32)
        hid = (_silu(g) * u).astype(BF16)
        o_ref[...] += jnp.dot(hid, wd_ref[...], preferred_element_type=F32)

        @pl.when(f == nf - 1)
        def _():
            _ffn_epilogue(x_ref, gpost_ref, o_ref)


def _ffn_first_kernel(x_ref, gpre_ref, wg_ref, wu_ref, wd_ref, gpost_ref,
                      o_ref, wg16_ref, wu16_ref, wd16_ref, xn_ref, g_ref, u_ref):
    f = pl.program_id(0)
    k = pl.program_id(1)
    dk = wg_ref.shape[0]

    @pl.when(jnp.logical_and(f == 0, k == 0))
    def _():
        _ffn_prologue(x_ref, gpre_ref, xn_ref, o_ref)

    wg = wg_ref[...].astype(BF16)
    wu = wu_ref[...].astype(BF16)
    wg16_ref[...] = wg
    wu16_ref[...] = wu
    xk = xn_ref[:, pl.ds(pl.multiple_of(k * dk, dk), dk)]
    g = jnp.dot(xk, wg, preferred_element_type=F32)
    u = jnp.dot(xk, wu, preferred_element_type=F32)

    @pl.when(k == 0)
    def _():
        g_ref[...] = g
        u_ref[...] = u

    @pl.when(k == 1)
    def _():
        wd = wd_ref[...].astype(BF16)
        wd16_ref[...] = wd
        hid = (_silu(g_ref[...] + g) * (u_ref[...] + u)).astype(BF16)
        o_ref[...] += jnp.dot(hid, wd, preferred_element_type=F32)

    @pl.when(jnp.logical_and(f == pl.num_programs(0) - 1, k == 1))
    def _():
        _ffn_epilogue(x_ref, gpost_ref, o_ref)


def _ffn(x, g_pre, wg, wu, wd, g_post, *, bm, bf):
    S, D = x.shape
    d_ff = wg.shape[1]
    once = dict(pipeline_mode=pl.Buffered(1))
    h, wg16, wu16, wd16 = pl.pallas_call(
        _ffn_first_kernel,
        name="ffn_first",
        grid=(d_ff // bf, 2),
        in_specs=[
            pl.BlockSpec((bm, D), lambda f, k: (0, 0), **once),
            pl.BlockSpec((1, D), lambda f, k: (0, 0)),
            pl.BlockSpec((D // 2, bf), lambda f, k: (k, f)),
            pl.BlockSpec((D // 2, bf), lambda f, k: (k, f)),
            pl.BlockSpec((bf, D), lambda f, k: (f, 0)),
            pl.BlockSpec((1, D), lambda f, k: (0, 0)),
        ],
        out_specs=[
            pl.BlockSpec((bm, D), lambda f, k: (0, 0)),
            pl.BlockSpec((D // 2, bf), lambda f, k: (k, f)),
            pl.BlockSpec((D // 2, bf), lambda f, k: (k, f)),
            pl.BlockSpec((bf, D), lambda f, k: (f, 0)),
        ],
        out_shape=[
            jax.ShapeDtypeStruct((bm, D), F32),
            jax.ShapeDtypeStruct((D, d_ff), BF16),
            jax.ShapeDtypeStruct((D, d_ff), BF16),
            jax.ShapeDtypeStruct((d_ff, D), BF16),
        ],
        scratch_shapes=[pltpu.VMEM((bm, D), BF16), pltpu.VMEM((bm, bf), F32), pltpu.VMEM((bm, bf), F32)],
        compiler_params=_params("arbitrary", "arbitrary"),
    )(x, g_pre, wg, wu, wd, g_post)
    if S == bm:
        return h
    nf = d_ff // bf
    c = FFN_COPY_STEPS

    steps = np.arange(c + (S // bm - 1) * nf)
    row_tab = np.where(steps < c, 0, 1 + (steps - c) // nf).astype(np.int32)
    ff_tab = np.where(steps < c, 0, (steps - c) % nf).astype(np.int32)
    grid_spec = pltpu.PrefetchScalarGridSpec(
        num_scalar_prefetch=2,
        grid=(steps.size,),
        in_specs=[
            pl.BlockSpec((bm, D), lambda s, rt, ft: (rt[s], 0)),
            pl.BlockSpec((1, D), lambda s, rt, ft: (0, 0)),
            pl.BlockSpec((D, bf), lambda s, rt, ft: (0, ft[s])),
            pl.BlockSpec((D, bf), lambda s, rt, ft: (0, ft[s])),
            pl.BlockSpec((bf, D), lambda s, rt, ft: (ft[s], 0)),
            pl.BlockSpec((1, D), lambda s, rt, ft: (0, 0)),
            pl.BlockSpec((bm // c, D), lambda s, rt, ft: (jnp.minimum(s, c - 1), 0)),
        ],
        out_specs=pl.BlockSpec((bm, D), lambda s, rt, ft: (rt[s], 0)),
        scratch_shapes=[pltpu.VMEM((bm, D), BF16)],
    )
    return pl.pallas_call(
        functools.partial(_ffn_kernel, nf=nf),
        name="ffn",
        grid_spec=grid_spec,
        out_shape=jax.ShapeDtypeStruct((S, D), F32),
        compiler_params=_params("arbitrary"),
    )(jnp.asarray(row_tab), jnp.asarray(ff_tab), x, g_pre, wg16, wu16, wd16, g_post, h)


NT_DIMS = (((1,), (1,)), ((), ()))


def _inproj_kernel(h_ref, g_ref, wt_ref, wst_ref, scale_ref, o_ref, os_ref, a_ref):
    n = pl.program_id(1)

    @pl.when(n == 0)
    def _prologue():
        def emit(rows, y):
            a_ref[rows, :] = y.astype(BF16)
        _rms_rows(h_ref, g_ref, emit)
        os_ref[...] = lax.dot_general(a_ref[...], wst_ref[...], NT_DIMS, preferred_element_type=F32)

    for cols in _sub_tiles(wt_ref.shape[0]):
        acc = lax.dot_general(a_ref[...], wt_ref[cols, :], NT_DIMS, preferred_element_type=F32)
        o_ref[:, cols] = (acc * scale_ref[:, cols]).astype(BF16)


def _inproj(h, g, w_t, w_small_t, col_scale, *, bm, bn, n1, rest_row):
    S, D = h.shape
    n_cols = col_scale.shape[1]

    def w_row(i, n):
        return pl.multiple_of(jnp.where(n < n1, n * bn, rest_row + (n - n1) * bn), FOX_HEADS)

    return pl.pallas_call(
        _inproj_kernel,
        name="inproj",
        grid=(S // bm, n_cols // bn),
        in_specs=[
            pl.BlockSpec((bm, D), lambda i, n: (i, 0), pipeline_mode=pl.Buffered(1)),
            pl.BlockSpec((1, D), lambda i, n: (0, 0)),
            pl.BlockSpec((pl.Element(bn), pl.Element(D)), lambda i, n: (w_row(i, n), 0)),
            pl.BlockSpec((LANES, D), lambda i, n: (0, 0)),
            pl.BlockSpec((1, bn), lambda i, n: (0, n)),
        ],
        out_specs=[
            pl.BlockSpec((bm, bn), lambda i, n: (i, n)),
            pl.BlockSpec((bm, LANES), lambda i, n: (i, 0)),
        ],
        out_shape=[
            jax.ShapeDtypeStruct((S, n_cols), BF16),
            jax.ShapeDtypeStruct((S, LANES), F32),
        ],
        scratch_shapes=[pltpu.VMEM((bm, D), BF16)],
        compiler_params=_params("parallel", "arbitrary"),
    )(h, g, w_t, w_small_t, col_scale)


def _fox_gate_kernel(small_ref, bias_ref, c_ref, carry_ref):
    t = pl.program_id(0)
    T = small_ref.shape[0]

    @pl.when(t == 0)
    def _init():
        carry_ref[...] = jnp.zeros_like(carry_ref)

    log_f = _log_sigmoid(small_ref[...] + bias_ref[...])
    row = lax.broadcasted_iota(jnp.int32, (T, T), 0)
    col = lax.broadcasted_iota(jnp.int32, (T, T), 1)
    tri = jnp.where(row >= col, 1.0, 0.0).astype(F32)
    c = jnp.dot(tri, log_f, precision=HIGHEST, preferred_element_type=F32) + carry_ref[...]
    c_ref[...] = c * LOG2_E
    carry_ref[...] = c[T - 1:T, :]


def _fox_gate(small, bias, *, bt):
    S = small.shape[0]
    return pl.pallas_call(
        _fox_gate_kernel,
        name="fox_gate",
        grid=(S // bt,),
        in_specs=[
            pl.BlockSpec((bt, LANES), lambda t: (t, 0)),
            pl.BlockSpec((1, LANES), lambda t: (0, 0)),
        ],
        out_specs=pl.BlockSpec((bt, LANES), lambda t: (t, 0)),
        out_shape=jax.ShapeDtypeStruct((S, LANES), F32),
        scratch_shapes=[pltpu.VMEM((1, LANES), F32)],
        compiler_params=_params("arbitrary"),
    )(small, bias)


FOX_PAIR = 2
FOX_ROWS = 16


def _split3(c):
    hi = c.astype(BF16).astype(F32)
    mid = (c - hi).astype(BF16).astype(F32)
    lo = (c - hi - mid).astype(BF16).astype(F32)
    return hi, mid, lo


def _lane_tile(x, n):
    return jnp.concatenate([x] * n, axis=1)


def _head_column(block, head):
    lane = lax.broadcasted_iota(jnp.int32, block.shape, 1)
    return jnp.sum(jnp.where(lane == head, block, 0.0), axis=1, keepdims=True)


def _fox_kernel(q_ref, k_ref, v_ref, c_ref, o_ref, kaug_ref, vaug_ref, qaug_ref, s_ref, p_ref,
                m_ref, alpha_ref, acc_ref, *, bk):
    hp = pl.program_id(0)
    i = pl.program_id(1)
    bq = q_ref.shape[0]
    S = k_ref.shape[0]
    dh = FOX_HEAD_DIM
    R = FOX_ROWS

    @pl.when(i == 0)
    def _build_kv_side():
        rows_per = 128

        def body(r, carry):
            rows = pl.ds(pl.multiple_of(r * rows_per, rows_per), rows_per)
            cblk = c_ref[rows, :]
            lane = lax.broadcasted_iota(jnp.int32, (rows_per, LANES), 1)
            for hh in range(FOX_PAIR):
                hi, mid, lo = _split3(_head_column(cblk, hp * FOX_PAIR + hh))
                extra = jnp.where(lane < 3, 1.0,
                                  jnp.where(lane == 3, -hi,
                                            jnp.where(lane == 4, -mid,
                                                      jnp.where(lane == 5, -lo, 0.0))))
                kaug_ref[hh, rows, 0:dh] = k_ref[rows, hh * dh:(hh + 1) * dh]
                kaug_ref[hh, rows, dh:2 * dh] = extra.astype(BF16)
                vaug_ref[hh, rows, 0:dh] = v_ref[rows, hh * dh:(hh + 1) * dh]
                vaug_ref[hh, rows, dh:2 * dh] = jnp.ones((rows_per, dh), BF16)
            return carry
        lax.fori_loop(0, S // rows_per, body, 0, unroll=4)

    cq_blk = c_ref[pl.ds(pl.multiple_of(i * bq, bq), bq), :]
    lane_q = lax.broadcasted_iota(jnp.int32, (bq, LANES), 1)
    for hh in range(FOX_PAIR):
        hi, mid, lo = _split3(_head_column(cq_blk, hp * FOX_PAIR + hh))
        extra = jnp.where(lane_q == 0, hi,
                          jnp.where(lane_q == 1, mid,
                                    jnp.where(lane_q == 2, lo,
                                              jnp.where(lane_q < 6, 1.0, 0.0))))
        qaug_ref[hh, :, 0:dh] = q_ref[:, hh * dh:(hh + 1) * dh]
        qaug_ref[hh, :, dh:2 * dh] = extra.astype(BF16)
        m_ref[hh] = jnp.full((bq, LANES), -jnp.inf, F32)
        acc_ref[hh] = jnp.zeros((bq, 2 * dh), F32)

    groups = [slice(g * R, (g + 1) * R) for g in range(bq // R)]

    def chunk(k0, width, masked):
        ks = pl.ds(pl.multiple_of(k0, bq), width)
        for hh in range(FOX_PAIR):
            s_ref[hh, :, 0:width] = lax.dot_general(
                qaug_ref[hh], kaug_ref[hh, ks, :], (((1,), (1,)), ((), ())),
                preferred_element_type=F32)
        if masked:
            diff = (lax.broadcasted_iota(jnp.int32, (R, width), 1)
                    - lax.broadcasted_iota(jnp.int32, (R, width), 0))
            limit = i * bq - k0
        for hh in range(FOX_PAIR):
            for g, rows in enumerate(groups):
                s = s_ref[hh, rows, 0:width]
                if masked:
                    s = jnp.where(diff <= limit + g * R, s, -jnp.inf)
                    s_ref[hh, rows, 0:width] = s
                m_old = m_ref[hh, rows, :]
                m_new = jnp.maximum(m_old, jnp.broadcast_to(
                    jnp.max(s, axis=1, keepdims=True), (R, LANES)))
                alpha_ref[hh, rows, :] = jnp.exp2(m_old - m_new)
                m_ref[hh, rows, :] = m_new
        for hh in range(FOX_PAIR):
            for rows in groups:
                m = _lane_tile(m_ref[hh, rows, :], width // LANES)
                p_ref[hh, rows, 0:width] = jnp.exp2(s_ref[hh, rows, 0:width] - m).astype(BF16)
        for hh in range(FOX_PAIR):
            alpha = _lane_tile(alpha_ref[hh], 2)
            acc_ref[hh] = alpha * acc_ref[hh] + jnp.dot(
                p_ref[hh, :, 0:width], vaug_ref[hh, ks, :], preferred_element_type=F32)

    n_full = (i * bq) // bk

    def full_body(j, carry):
        chunk(j * bk, bk, False)
        return carry
    lax.fori_loop(0, n_full, full_body, 0)

    tail_start = n_full * bk
    tail_blocks = (i + 1) - n_full * (bk // bq)
    for t in range(1, bk // bq + 1):
        @pl.when(tail_blocks == t)
        def _tail(t=t):
            chunk(tail_start, t * bq, True)

    for hh in range(FOX_PAIR):
        acc = acc_ref[hh]
        o_ref[:, hh * dh:(hh + 1) * dh] = (acc[:, 0:dh] / acc[:, dh:2 * dh]).astype(o_ref.dtype)


def _fox(proj, c2, *, bq, bk):
    S = proj.shape[0]
    dh = FOX_HEAD_DIM
    w = FOX_PAIR * dh
    return pl.pallas_call(
        functools.partial(_fox_kernel, bk=bk),
        name="fox",
        grid=(FOX_HEADS // FOX_PAIR, S // bq),
        in_specs=[
            pl.BlockSpec((bq, w), lambda h, i: (i, OFF_FQ // w + h)),
            pl.BlockSpec((S, w), lambda h, i: (0, OFF_FK // w + h)),
            pl.BlockSpec((S, w), lambda h, i: (0, OFF_FV // w + h)),
            pl.BlockSpec((S, LANES), lambda h, i: (0, 0)),
        ],
        out_specs=pl.BlockSpec((bq, w), lambda h, i: (i, h)),
        out_shape=jax.ShapeDtypeStruct((S, FOX_WIDTH), BF16),
        scratch_shapes=[
            pltpu.VMEM((FOX_PAIR, S, 2 * dh), BF16),
            pltpu.VMEM((FOX_PAIR, S, 2 * dh), BF16),
            pltpu.VMEM((FOX_PAIR, bq, 2 * dh), BF16),
            pltpu.VMEM((FOX_PAIR, bq, bk), F32),
            pltpu.VMEM((FOX_PAIR, bq, bk), BF16),
            pltpu.VMEM((FOX_PAIR, bq, LANES), F32),
            pltpu.VMEM((FOX_PAIR, bq, LANES), F32),
            pltpu.VMEM((FOX_PAIR, bq, 2 * dh), F32),
        ],
        compiler_params=_params("parallel", "arbitrary"),
    )(proj, proj, proj, c2)


def _gla_kernel(q_ref, k_ref, v_ref, gr_ref, small_ref, wgate_ref, bgate_ref,
                gnorm_ref, tri_ref, o_ref, st_ref):
    t = pl.program_id(1)
    T = q_ref.shape[0]
    C = GLA_CHUNK

    @pl.when(t == 0)
    def _init():
        st_ref[...] = jnp.zeros_like(st_ref)

    dk = q_ref.shape[1]
    gate = jnp.dot(small_ref[...].astype(BF16), wgate_ref[...],
                   preferred_element_type=F32) + bgate_ref[...]
    log_a = _log_sigmoid(gate) * (1.0 / GLA_TAU)

    pieces = jnp.concatenate([x.astype(BF16) for x in _split3(log_a)], axis=1)
    sums = jnp.dot(tri_ref[...].astype(BF16), pieces, preferred_element_type=F32)
    b = sums[:, 0:dk] + sums[:, dk:2 * dk] + sums[:, 2 * dk:3 * dk]
    b_tot = jnp.concatenate(
        [jnp.broadcast_to(b[c * C + C - 1:c * C + C, :], (C, dk)) for c in range(T // C)], axis=0)
    causal = tri_ref[...] > 0.0

    q = q_ref[...].astype(F32)
    k = k_ref[...].astype(F32)
    v = v_ref[...]
    q_dec = (q * jnp.exp(b)).astype(BF16)
    k_inv = (k * jnp.exp(-b)).astype(BF16)
    k_end = (k * jnp.exp(b_tot - b)).astype(BF16)
    a = lax.dot_general(q_dec, k_inv, (((1,), (1,)), ((), ())), preferred_element_type=F32)
    a = jnp.where(causal, a, 0.0).astype(BF16)
    o_intra = jnp.dot(a, v, preferred_element_type=F32)

    for ci in range(T // C):
        lo, hi = ci * C, (ci + 1) * C
        st = st_ref[...]
        o_c = o_intra[lo:hi, :] + lax.dot_general(
            q_dec[lo:hi, :], st.astype(BF16), (((1,), (1,)), ((), ())),
            preferred_element_type=F32)
        decay = jnp.exp(b_tot[lo:lo + 1, :])
        st_ref[...] = st * decay + lax.dot_general(
            v[lo:hi, :], k_end[lo:hi, :], (((0,), (0,)), ((), ())),
            preferred_element_type=F32)
        y = _rms(o_c, gnorm_ref[...])
        o_ref[lo:hi, :] = (y * _silu(gr_ref[lo:hi, :].astype(F32))).astype(o_ref.dtype)


def _chunk_causal_mask(n):
    r = jnp.arange(n)[:, None]
    c = jnp.arange(n)[None, :]
    return ((r // GLA_CHUNK == c // GLA_CHUNK) & (r >= c)).astype(F32)


def _gla(proj, small, wgate_pad, bgate, gnorm, *, bt):
    S = proj.shape[0]
    dk, dv = GLA_HEAD_K, GLA_HEAD_V
    return pl.pallas_call(
        _gla_kernel,
        name="gla",
        grid=(GLA_HEADS, S // bt),
        in_specs=[
            pl.BlockSpec((bt, dk), lambda h, t: (t, OFF_GQ // dk + h)),
            pl.BlockSpec((bt, dk), lambda h, t: (t, OFF_GK // dk + h)),
            pl.BlockSpec((bt, dv), lambda h, t: (t, OFF_GV // dv + h)),
            pl.BlockSpec((bt, dv), lambda h, t: (t, OFF_GR // dv + h)),
            pl.BlockSpec((bt, LANES), lambda h, t: (t, 0)),
            pl.BlockSpec((LANES, dk), lambda h, t: (0, h)),
            pl.BlockSpec((1, dk), lambda h, t: (0, h)),
            pl.BlockSpec((1, dv), lambda h, t: (0, 0)),
            pl.BlockSpec((bt, bt), lambda h, t: (0, 0)),
        ],
        out_specs=pl.BlockSpec((bt, dv), lambda h, t: (t, h)),
        out_shape=jax.ShapeDtypeStruct((S, GLA_WIDTH), BF16),
        scratch_shapes=[pltpu.VMEM((dv, dk), F32)],
        compiler_params=_params("parallel", "arbitrary"),
    )(proj, proj, proj, proj, small, wgate_pad, bgate, gnorm, _chunk_causal_mask(bt))


def _outproj_kernel(fox_ref, gla_ref, wt_ref, wb_ref, h_ref, g_ref, o_ref):
    n = pl.program_id(1)
    bn = wt_ref.shape[1]
    for sub in _sub_tiles(bn):
        cols = pl.ds(pl.multiple_of(n * bn + sub.start, MXU_COLS), MXU_COLS)
        o_ref[:, cols] = (jnp.dot(fox_ref[...], wt_ref[:, sub], preferred_element_type=F32)
                          + jnp.dot(gla_ref[...], wb_ref[:, sub], preferred_element_type=F32))

    @pl.when(n == pl.num_programs(1) - 1)
    def _epilogue():
        def emit(rows, y):
            o_ref[rows, :] = h_ref[rows, :] + y
        _rms_rows(o_ref, g_ref, emit)


def _outproj(o_fox, o_gla, w_o, h, g, *, bm, bn):
    S, D = h.shape
    half = o_fox.shape[1]
    return pl.pallas_call(
        _outproj_kernel,
        name="outproj",
        grid=(S // bm, D // bn),
        in_specs=[
            pl.BlockSpec((bm, half), lambda i, n: (i, 0)),
            pl.BlockSpec((bm, half), lambda i, n: (i, 0)),
            pl.BlockSpec((half, bn), lambda i, n: (0, n)),
            pl.BlockSpec((half, bn), lambda i, n: (1, n)),
            pl.BlockSpec((bm, D), lambda i, n: (i, 0)),
            pl.BlockSpec((1, D), lambda i, n: (0, 0)),
        ],
        out_specs=pl.BlockSpec((bm, D), lambda i, n: (i, 0)),
        out_shape=jax.ShapeDtypeStruct((S, D), F32),
        compiler_params=_params("parallel", "arbitrary"),
    )(o_fox, o_gla, w_o, w_o, h, g)


def _ple_kernel(h_ref, p_ref, wp_ref, g_ref, wg_ref, o_ref, hb_ref, e_ref):
    n = pl.program_id(1)
    bn = wg_ref.shape[1]

    @pl.when(n == 0)
    def _prologue():
        e_ref[...] = jnp.dot(p_ref[...].astype(BF16), wp_ref[...], preferred_element_type=F32)

        def emit(rows, y):
            hb_ref[rows, :] = h_ref[rows, :].astype(BF16)
            e_ref[rows, :] = y
        _rms_rows(e_ref, g_ref, emit)

    for sub in _sub_tiles(bn):
        cols = pl.ds(pl.multiple_of(n * bn + sub.start, MXU_COLS), MXU_COLS)
        gate = jax.nn.sigmoid(jnp.dot(hb_ref[...], wg_ref[:, sub], preferred_element_type=F32))
        o_ref[:, sub] = h_ref[:, cols] + e_ref[:, cols] * gate


def _ple(h, p, w_proj, g, w_gate, *, bm, bn):
    S, D = h.shape
    dp = p.shape[1]
    return pl.pallas_call(
        _ple_kernel,
        name="ple",
        grid=(S // bm, D // bn),
        in_specs=[
            pl.BlockSpec((bm, D), lambda i, n: (i, 0)),
            pl.BlockSpec((bm, dp), lambda i, n: (i, 0)),
            pl.BlockSpec((dp, D), lambda i, n: (0, 0)),
            pl.BlockSpec((1, D), lambda i, n: (0, 0)),
            pl.BlockSpec((D, bn), lambda i, n: (0, n)),
        ],
        out_specs=pl.BlockSpec((bm, bn), lambda i, n: (i, n)),
        out_shape=jax.ShapeDtypeStruct((S, D), F32),
        scratch_shapes=[pltpu.VMEM((bm, D), BF16), pltpu.VMEM((bm, D), F32)],
        compiler_params=_params("parallel", "arbitrary"),
    )(h, p, w_proj, g, w_gate)


def _tile(n, preferred):
    t = min(n, preferred)
    while n % t:
        t //= 2
    return t


def _split_w_in(w_in):
    D = w_in.shape[0]
    ff_lo = 3 * FOX_WIDTH
    ff_hi = ff_lo + FOX_HEADS
    glr_lo = ff_hi + 2 * GLA_KEY_WIDTH + 2 * GLA_WIDTH
    glr_hi = glr_lo + GLA_GATE_RANK
    w_t = w_in.T.astype(BF16)
    pad = jnp.zeros((LANES - FOX_HEADS - GLA_GATE_RANK, D), BF16)
    small_t = jnp.concatenate([w_t[ff_lo:ff_hi], w_t[glr_lo:glr_hi], pad], axis=0)
    return w_t, small_t, ff_hi


def kernel(x, p, ffn1_norm_pre, ffn1_w_gate, ffn1_w_up, ffn1_w_down, ffn1_norm_post, mix_norm_pre, w_in, fox_b_f, gla_w_gate, gla_b_gate, gla_norm_g, w_o, mix_norm_post, ffn2_norm_pre, ffn2_w_gate, ffn2_w_up, ffn2_w_down, ffn2_norm_post, ple_w_proj, ple_norm, ple_w_gate):
    B, S, D = x.shape
    assert B == 1, "the attention kernels treat the row axis as one sequence"
    depth = w_in.shape[0]
    h = x.reshape(S, D)

    bm = _tile(S, 512)
    bf = _tile(ffn1_w_gate.shape[-1], 256)
    bq = _tile(S, 512)
    bk = max(bq, _tile(S, 1024))
    bt = _tile(S, 512)

    col_scale = jnp.ones((1, MAIN_COLS), F32)
    col_scale = col_scale.at[:, OFF_FQ:OFF_FQ + FOX_WIDTH].set(FOX_HEAD_DIM ** -0.5 * LOG2_E)
    col_scale = col_scale.at[:, OFF_GQ:OFF_GQ + GLA_KEY_WIDTH].set(GLA_HEAD_K ** -0.5)

    def row(v):
        return v.reshape(1, -1).astype(F32)

    for i in range(depth):
        h = _ffn(h, row(ffn1_norm_pre[i]), ffn1_w_gate[i], ffn1_w_up[i], ffn1_w_down[i],
                 row(ffn1_norm_post[i]), bm=bm, bf=bf)

        w_t, w_small_t, rest_row = _split_w_in(w_in[i])
        bn = 1024
        proj, small = _inproj(h, row(mix_norm_pre[i]), w_t, w_small_t, col_scale, bm=_tile(S, 1024),
                              bn=bn, n1=3 * FOX_WIDTH // bn, rest_row=rest_row)

        bias_f = jnp.zeros((1, LANES), F32).at[0, :FOX_HEADS].set(fox_b_f[i])
        c2 = _fox_gate(small, bias_f, bt=bt)
        o_fox = _fox(proj, c2, bq=bq, bk=bk)

        wgate_pad = jnp.zeros((LANES, GLA_KEY_WIDTH), F32).at[
            FOX_HEADS:FOX_HEADS + GLA_GATE_RANK, :].set(gla_w_gate[i]).astype(BF16)
        o_gla = _gla(proj, small, wgate_pad, row(gla_b_gate[i]), row(gla_norm_g[i]), bt=bt)

        h = _outproj(o_fox, o_gla, w_o[i].astype(BF16), h, row(mix_norm_post[i]), bm=bm, bn=512)

        h = _ffn(h, row(ffn2_norm_pre[i]), ffn2_w_gate[i], ffn2_w_up[i], ffn2_w_down[i],
                 row(ffn2_norm_post[i]), bm=bm, bf=bf)

        h = _ple(h, p[i].reshape(S, -1), ple_w_proj[i].astype(BF16), row(ple_norm[i]),
                 ple_w_gate[i].astype(BF16), bm=bm, bn=1024)
    return h.reshape(B, S, D)
```

```python
import functools

import jax
import jax.numpy as jnp
import numpy as np
from jax import lax
from jax.experimental import pallas as pl
from jax.experimental.pallas import tpu as pltpu

F32 = jnp.float32
BF16 = jnp.bfloat16
HIGHEST = lax.Precision.HIGHEST

EPS = 1e-6
LOG2_E = 1.4426950408889634
MACARON_WEIGHT = 0.5
FOX_HEAD_DIM = 128
FOX_HEADS = 16
FOX_WIDTH = FOX_HEADS * FOX_HEAD_DIM
GLA_HEADS = 4
GLA_WIDTH = 2048
GLA_HEAD_V = GLA_WIDTH // GLA_HEADS
GLA_KEY_WIDTH = GLA_WIDTH // 2
GLA_HEAD_K = GLA_KEY_WIDTH // GLA_HEADS
GLA_GATE_RANK = 16
GLA_TAU = 16.0
GLA_CHUNK = 64

LANES = 128
MXU_COLS = 256
VMEM_LIMIT_BYTES = 56 * 1024 * 1024

MAIN_COLS = 3 * FOX_WIDTH + 2 * GLA_WIDTH + 2 * GLA_KEY_WIDTH
OFF_FQ = 0
OFF_FK = FOX_WIDTH
OFF_FV = 2 * FOX_WIDTH
OFF_GQ = 3 * FOX_WIDTH
OFF_GK = OFF_GQ + GLA_KEY_WIDTH
OFF_GV = OFF_GK + GLA_KEY_WIDTH
OFF_GR = OFF_GV + GLA_WIDTH

ROW_CHUNK = 32
ROW_BLOCK = 128


def _params(*semantics):
    return pltpu.CompilerParams(dimension_semantics=semantics,
                                vmem_limit_bytes=VMEM_LIMIT_BYTES)


def _rms(x, g):
    ms = jnp.mean(x * x, axis=-1, keepdims=True)
    return x * lax.rsqrt(ms + EPS) * g


def _log_sigmoid(x):
    return jnp.minimum(x, 0.0) - jnp.log(1.0 + jnp.exp(-jnp.abs(x)))


def _silu(x):
    return x * jax.nn.sigmoid(x)


def _sub_tiles(n_cols):
    return [slice(c, c + MXU_COLS) for c in range(0, n_cols, MXU_COLS)]


def _rms_rows(src_ref, g_ref, emit):
    def body(b, carry):
        chunks = [pl.ds(pl.multiple_of(b * ROW_BLOCK + c * ROW_CHUNK, ROW_CHUNK), ROW_CHUNK)
                  for c in range(ROW_BLOCK // ROW_CHUNK)]
        scales = []
        for rows in chunks:
            x = src_ref[rows, :]
            scales.append(lax.rsqrt(jnp.mean(x * x, axis=-1, keepdims=True) + EPS))
        for rows, scale in zip(chunks, scales):
            emit(rows, src_ref[rows, :] * scale * g_ref[...])
        return carry
    lax.fori_loop(0, src_ref.shape[0] // ROW_BLOCK, body, 0)


def _ffn_prologue(x_ref, gpre_ref, xn_ref, o_ref):
    def emit(rows, y):
        xn_ref[rows, :] = y.astype(BF16)
        o_ref[rows, :] = jnp.zeros((ROW_CHUNK, o_ref.shape[1]), F32)
    _rms_rows(x_ref, gpre_ref, emit)


def _ffn_epilogue(x_ref, gpost_ref, o_ref):
    def emit(rows, y):
        o_ref[rows, :] = x_ref[rows, :] + MACARON_WEIGHT * y
    _rms_rows(o_ref, gpost_ref, emit)


FFN_COPY_STEPS = 4


def _ffn_kernel(row_tab_ref, ff_tab_ref, x_ref, gpre_ref, wg_ref, wu_ref, wd_ref, gpost_ref,
                first_ref, o_ref, xn_ref, *, nf):
    del row_tab_ref
    s = pl.program_id(0)
    rows_c = first_ref.shape[0]

    @pl.when(s < FFN_COPY_STEPS)
    def _():
        o_ref[pl.ds(pl.multiple_of(s * rows_c, rows_c), rows_c), :] = first_ref[...]

    @pl.when(s >= FFN_COPY_STEPS)
    def _():
        f = ff_tab_ref[s]

        @pl.when(f == 0)
        def _():
            _ffn_prologue(x_ref, gpre_ref, xn_ref, o_ref)

        xn = xn_ref[...]
        g = jnp.dot(xn, wg_ref[...], preferred_element_type=F32)
        u = jnp.dot(xn, wu_ref[...], preferred_element_type=F32)
        hid = (_silu(g) * u).astype(BF16)
        o_ref[...] += jnp.dot(hid, wd_ref[...], preferred_element_type=F32)

        @pl.when(f == nf - 1)
        def _():
            _ffn_epilogue(x_ref, gpost_ref, o_ref)


def _ffn_first_kernel(x_ref, gpre_ref, wg_ref, wu_ref, wd_ref, gpost_ref,
                      o_ref, wg16_ref, wu16_ref, wd16_ref, xn_ref, g_ref, u_ref):
    f = pl.program_id(0)
    k = pl.program_id(1)
    dk = wg_ref.shape[0]

    @pl.when(jnp.logical_and(f == 0, k == 0))
    def _():
        _ffn_prologue(x_ref, gpre_ref, xn_ref, o_ref)

    wg = wg_ref[...].astype(BF16)
    wu = wu_ref[...].astype(BF16)
    wg16_ref[...] = wg
    wu16_ref[...] = wu
    xk = xn_ref[:, pl.ds(pl.multiple_of(k * dk, dk), dk)]
    g = jnp.dot(xk, wg, preferred_element_type=F32)
    u = jnp.dot(xk, wu, preferred_element_type=F32)

    @pl.when(k == 0)
    def _():
        g_ref[...] = g
        u_ref[...] = u

    @pl.when(k == 1)
    def _():
        wd = wd_ref[...].astype(BF16)
        wd16_ref[...] = wd
        hid = (_silu(g_ref[...] + g) * (u_ref[...] + u)).astype(BF16)
        o_ref[...] += jnp.dot(hid, wd, preferred_element_type=F32)

    @pl.when(jnp.logical_and(f == pl.num_programs(0) - 1, k == 1))
    def _():
        _ffn_epilogue(x_ref, gpost_ref, o_ref)


def _ffn(x, g_pre, wg, wu, wd, g_post, *, bm, bf):
    S, D = x.shape
    d_ff = wg.shape[1]
    once = dict(pipeline_mode=pl.Buffered(1))
    h, wg16, wu16, wd16 = pl.pallas_call(
        _ffn_first_kernel,
        name="ffn_first",
        grid=(d_ff // bf, 2),
        in_specs=[
            pl.BlockSpec((bm, D), lambda f, k: (0, 0), **once),
            pl.BlockSpec((1, D), lambda f, k: (0, 0)),
            pl.BlockSpec((D // 2, bf), lambda f, k: (k, f)),
            pl.BlockSpec((D // 2, bf), lambda f, k: (k, f)),
            pl.BlockSpec((bf, D), lambda f, k: (f, 0)),
            pl.BlockSpec((1, D), lambda f, k: (0, 0)),
        ],
        out_specs=[
            pl.BlockSpec((bm, D), lambda f, k: (0, 0)),
            pl.BlockSpec((D // 2, bf), lambda f, k: (k, f)),
            pl.BlockSpec((D // 2, bf), lambda f, k: (k, f)),
            pl.BlockSpec((bf, D), lambda f, k: (f, 0)),
        ],
        out_shape=[
            jax.ShapeDtypeStruct((bm, D), F32),
            jax.ShapeDtypeStruct((D, d_ff), BF16),
            jax.ShapeDtypeStruct((D, d_ff), BF16),
            jax.ShapeDtypeStruct((d_ff, D), BF16),
        ],
        scratch_shapes=[pltpu.VMEM((bm, D), BF16), pltpu.VMEM((bm, bf), F32), pltpu.VMEM((bm, bf), F32)],
        compiler_params=_params("arbitrary", "arbitrary"),
    )(x, g_pre, wg, wu, wd, g_post)
    if S == bm:
        return h
    nf = d_ff // bf
    c = FFN_COPY_STEPS

    steps = np.arange(c + (S // bm - 1) * nf)
    row_tab = np.where(steps < c, 0, 1 + (steps - c) // nf).astype(np.int32)
    ff_tab = np.where(steps < c, 0, (steps - c) % nf).astype(np.int32)
    grid_spec = pltpu.PrefetchScalarGridSpec(
        num_scalar_prefetch=2,
        grid=(steps.size,),
        in_specs=[
            pl.BlockSpec((bm, D), lambda s, rt, ft: (rt[s], 0)),
            pl.BlockSpec((1, D), lambda s, rt, ft: (0, 0)),
            pl.BlockSpec((D, bf), lambda s, rt, ft: (0, ft[s])),
            pl.BlockSpec((D, bf), lambda s, rt, ft: (0, ft[s])),
            pl.BlockSpec((bf, D), lambda s, rt, ft: (ft[s], 0)),
            pl.BlockSpec((1, D), lambda s, rt, ft: (0, 0)),
            pl.BlockSpec((bm // c, D), lambda s, rt, ft: (jnp.minimum(s, c - 1), 0)),
        ],
        out_specs=pl.BlockSpec((bm, D), lambda s, rt, ft: (rt[s], 0)),
        scratch_shapes=[pltpu.VMEM((bm, D), BF16)],
    )
    return pl.pallas_call(
        functools.partial(_ffn_kernel, nf=nf),
        name="ffn",
        grid_spec=grid_spec,
        out_shape=jax.ShapeDtypeStruct((S, D), F32),
        compiler_params=_params("arbitrary"),
    )(jnp.asarray(row_tab), jnp.asarray(ff_tab), x, g_pre, wg16, wu16, wd16, g_post, h)


NT_DIMS = (((1,), (1,)), ((), ()))


def _inproj_kernel(h_ref, g_ref, wt_ref, wst_ref, scale_ref, o_ref, os_ref, a_ref):
    n = pl.program_id(1)

    @pl.when(n == 0)
    def _prologue():
        def emit(rows, y):
            a_ref[rows, :] = y.astype(BF16)
        _rms_rows(h_ref, g_ref, emit)
        os_ref[...] = lax.dot_general(a_ref[...], wst_ref[...], NT_DIMS, preferred_element_type=F32)

    for cols in _sub_tiles(wt_ref.shape[0]):
        acc = lax.dot_general(a_ref[...], wt_ref[cols, :], NT_DIMS, preferred_element_type=F32)
        o_ref[:, cols] = (acc * scale_ref[:, cols]).astype(BF16)


def _inproj(h, g, w_t, w_small_t, col_scale, *, bm, bn, n1, rest_row):
    S, D = h.shape
    n_cols = col_scale.shape[1]

    def w_row(i, n):
        return pl.multiple_of(jnp.where(n < n1, n * bn, rest_row + (n - n1) * bn), FOX_HEADS)

    return pl.pallas_call(
        _inproj_kernel,
        name="inproj",
        grid=(S // bm, n_cols // bn),
        in_specs=[
            pl.BlockSpec((bm, D), lambda i, n: (i, 0), pipeline_mode=pl.Buffered(1)),
            pl.BlockSpec((1, D), lambda i, n: (0, 0)),
            pl.BlockSpec((pl.Element(bn), pl.Element(D)), lambda i, n: (w_row(i, n), 0)),
            pl.BlockSpec((LANES, D), lambda i, n: (0, 0)),
            pl.BlockSpec((1, bn), lambda i, n: (0, n)),
        ],
        out_specs=[
            pl.BlockSpec((bm, bn), lambda i, n: (i, n)),
            pl.BlockSpec((bm, LANES), lambda i, n: (i, 0)),
        ],
        out_shape=[
            jax.ShapeDtypeStruct((S, n_cols), BF16),
            jax.ShapeDtypeStruct((S, LANES), F32),
        ],
        scratch_shapes=[pltpu.VMEM((bm, D), BF16)],
        compiler_params=_params("parallel", "arbitrary"),
    )(h, g, w_t, w_small_t, col_scale)


def _fox_gate_kernel(small_ref, bias_ref, c_ref, carry_ref):
    t = pl.program_id(0)
    T = small_ref.shape[0]

    @pl.when(t == 0)
    def _init():
        carry_ref[...] = jnp.zeros_like(carry_ref)

    log_f = _log_sigmoid(small_ref[...] + bias_ref[...])
    row = lax.broadcasted_iota(jnp.int32, (T, T), 0)
    col = lax.broadcasted_iota(jnp.int32, (T, T), 1)
    tri = jnp.where(row >= col, 1.0, 0.0).astype(F32)
    c = jnp.dot(tri, log_f, precision=HIGHEST, preferred_element_type=F32) + carry_ref[...]
    c_ref[...] = c * LOG2_E
    carry_ref[...] = c[T - 1:T, :]


def _fox_gate(small, bias, *, bt):
    S = small.shape[0]
    return pl.pallas_call(
        _fox_gate_kernel,
        name="fox_gate",
        grid=(S // bt,),
        in_specs=[
            pl.BlockSpec((bt, LANES), lambda t: (t, 0)),
            pl.BlockSpec((1, LANES), lambda t: (0, 0)),
        ],
        out_specs=pl.BlockSpec((bt, LANES), lambda t: (t, 0)),
        out_shape=jax.ShapeDtypeStruct((S, LANES), F32),
        scratch_shapes=[pltpu.VMEM((1, LANES), F32)],
        compiler_params=_params("arbitrary"),
    )(small, bias)


FOX_PAIR = 2
FOX_ROWS = 16


def _split3(c):
    hi = c.astype(BF16).astype(F32)
    mid = (c - hi).astype(BF16).astype(F32)
    lo = (c - hi - mid).astype(BF16).astype(F32)
    return hi, mid, lo


def _lane_tile(x, n):
    return jnp.concatenate([x] * n, axis=1)


def _head_column(block, head):
    lane = lax.broadcasted_iota(jnp.int32, block.shape, 1)
    return jnp.sum(jnp.where(lane == head, block, 0.0), axis=1, keepdims=True)


def _fox_kernel(q_ref, k_ref, v_ref, c_ref, o_ref, kaug_ref, vaug_ref, qaug_ref, s_ref, p_ref,
                m_ref, alpha_ref, acc_ref, *, bk):
    hp = pl.program_id(0)
    i = pl.program_id(1)
    bq = q_ref.shape[0]
    S = k_ref.shape[0]
    dh = FOX_HEAD_DIM
    R = FOX_ROWS

    @pl.when(i == 0)
    def _build_kv_side():
        rows_per = 128

        def body(r, carry):
            rows = pl.ds(pl.multiple_of(r * rows_per, rows_per), rows_per)
            cblk = c_ref[rows, :]
            lane = lax.broadcasted_iota(jnp.int32, (rows_per, LANES), 1)
            for hh in range(FOX_PAIR):
                hi, mid, lo = _split3(_head_column(cblk, hp * FOX_PAIR + hh))
                extra = jnp.where(lane < 3, 1.0,
                                  jnp.where(lane == 3, -hi,
                                            jnp.where(lane == 4, -mid,
                                                      jnp.where(lane == 5, -lo, 0.0))))
                kaug_ref[hh, rows, 0:dh] = k_ref[rows, hh * dh:(hh + 1) * dh]
                kaug_ref[hh, rows, dh:2 * dh] = extra.astype(BF16)
                vaug_ref[hh, rows, 0:dh] = v_ref[rows, hh * dh:(hh + 1) * dh]
                vaug_ref[hh, rows, dh:2 * dh] = jnp.ones((rows_per, dh), BF16)
            return carry
        lax.fori_loop(0, S // rows_per, body, 0, unroll=4)

    cq_blk = c_ref[pl.ds(pl.multiple_of(i * bq, bq), bq), :]
    lane_q = lax.broadcasted_iota(jnp.int32, (bq, LANES), 1)
    for hh in range(FOX_PAIR):
        hi, mid, lo = _split3(_head_column(cq_blk, hp * FOX_PAIR + hh))
        extra = jnp.where(lane_q == 0, hi,
                          jnp.where(lane_q == 1, mid,
                                    jnp.where(lane_q == 2, lo,
                                              jnp.where(lane_q < 6, 1.0, 0.0))))
        qaug_ref[hh, :, 0:dh] = q_ref[:, hh * dh:(hh + 1) * dh]
        qaug_ref[hh, :, dh:2 * dh] = extra.astype(BF16)
        m_ref[hh] = jnp.full((bq, LANES), -jnp.inf, F32)
        acc_ref[hh] = jnp.zeros((bq, 2 * dh), F32)

    groups = [slice(g * R, (g + 1) * R) for g in range(bq // R)]

    def chunk(k0, width, masked):
        ks = pl.ds(pl.multiple_of(k0, bq), width)
        for hh in range(FOX_PAIR):
            s_ref[hh, :, 0:width] = lax.dot_general(
                qaug_ref[hh], kaug_ref[hh, ks, :], (((1,), (1,)), ((), ())),
                preferred_element_type=F32)
        if masked:
            diff = (lax.broadcasted_iota(jnp.int32, (R, width), 1)
                    - lax.broadcasted_iota(jnp.int32, (R, width), 0))
            limit = i * bq - k0
        for hh in range(FOX_PAIR):
            for g, rows in enumerate(groups):
                s = s_ref[hh, rows, 0:width]
                if masked:
                    s = jnp.where(diff <= limit + g * R, s, -jnp.inf)
                    s_ref[hh, rows, 0:width] = s
                m_old = m_ref[hh, rows, :]
                m_new = jnp.maximum(m_old, jnp.broadcast_to(
                    jnp.max(s, axis=1, keepdims=True), (R, LANES)))
                alpha_ref[hh, rows, :] = jnp.exp2(m_old - m_new)
                m_ref[hh, rows, :] = m_new
        for hh in range(FOX_PAIR):
            for rows in groups:
                m = _lane_tile(m_ref[hh, rows, :], width // LANES)
                p_ref[hh, rows, 0:width] = jnp.exp2(s_ref[hh, rows, 0:width] - m).astype(BF16)
        for hh in range(FOX_PAIR):
            alpha = _lane_tile(alpha_ref[hh], 2)
            acc_ref[hh] = alpha * acc_ref[hh] + jnp.dot(
                p_ref[hh, :, 0:width], vaug_ref[hh, ks, :], preferred_element_type=F32)

    n_full = (i * bq) // bk

    def pair_body(j, carry):
        chunk(2 * j * bk, bk, False)
        chunk((2 * j + 1) * bk, bk, False)
        return carry
    lax.fori_loop(0, n_full // 2, pair_body, 0)

    @pl.when(n_full % 2 == 1)
    def _odd():
        chunk((n_full - 1) * bk, bk, False)

    tail_start = n_full * bk
    tail_blocks = (i + 1) - n_full * (bk // bq)
    for t in range(1, bk // bq + 1):
        @pl.when(tail_blocks == t)
        def _tail(t=t):
            chunk(tail_start, t * bq, True)

    for hh in range(FOX_PAIR):
        acc = acc_ref[hh]
        o_ref[:, hh * dh:(hh + 1) * dh] = (acc[:, 0:dh] / acc[:, dh:2 * dh]).astype(o_ref.dtype)


def _fox(proj, c2, *, bq, bk):
    S = proj.shape[0]
    dh = FOX_HEAD_DIM
    w = FOX_PAIR * dh
    return pl.pallas_call(
        functools.partial(_fox_kernel, bk=bk),
        name="fox",
        grid=(FOX_HEADS // FOX_PAIR, S // bq),
        in_specs=[
            pl.BlockSpec((bq, w), lambda h, i: (i, OFF_FQ // w + h)),
            pl.BlockSpec((S, w), lambda h, i: (0, OFF_FK // w + h)),
            pl.BlockSpec((S, w), lambda h, i: (0, OFF_FV // w + h)),
            pl.BlockSpec((S, LANES), lambda h, i: (0, 0)),
        ],
        out_specs=pl.BlockSpec((bq, w), lambda h, i: (i, h)),
        out_shape=jax.ShapeDtypeStruct((S, FOX_WIDTH), BF16),
        scratch_shapes=[
            pltpu.VMEM((FOX_PAIR, S, 2 * dh), BF16),
            pltpu.VMEM((FOX_PAIR, S, 2 * dh), BF16),
            pltpu.VMEM((FOX_PAIR, bq, 2 * dh), BF16),
            pltpu.VMEM((FOX_PAIR, bq, bk), F32),
            pltpu.VMEM((FOX_PAIR, bq, bk), BF16),
            pltpu.VMEM((FOX_PAIR, bq, LANES), F32),
            pltpu.VMEM((FOX_PAIR, bq, LANES), F32),
            pltpu.VMEM((FOX_PAIR, bq, 2 * dh), F32),
        ],
        compiler_params=_params("parallel", "arbitrary"),
    )(proj, proj, proj, c2)


def _gla_kernel(q_ref, k_ref, v_ref, gr_ref, small_ref, wgate_ref, bgate_ref,
                gnorm_ref, tri_ref, o_ref, st_ref):
    t = pl.program_id(1)
    T = q_ref.shape[0]
    C = GLA_CHUNK

    @pl.when(t == 0)
    def _init():
        st_ref[...] = jnp.zeros_like(st_ref)

    dk = q_ref.shape[1]
    gate = jnp.dot(small_ref[...].astype(BF16), wgate_ref[...],
                   preferred_element_type=F32) + bgate_ref[...]
    log_a = _log_sigmoid(gate) * (1.0 / GLA_TAU)

    pieces = jnp.concatenate([x.astype(BF16) for x in _split3(log_a)], axis=1)
    sums = jnp.dot(tri_ref[...].astype(BF16), pieces, preferred_element_type=F32)
    b = sums[:, 0:dk] + sums[:, dk:2 * dk] + sums[:, 2 * dk:3 * dk]
    b_tot = jnp.concatenate(
        [jnp.broadcast_to(b[c * C + C - 1:c * C + C, :], (C, dk)) for c in range(T // C)], axis=0)
    causal = tri_ref[...] > 0.0

    q = q_ref[...].astype(F32)
    k = k_ref[...].astype(F32)
    v = v_ref[...]
    q_dec = (q * jnp.exp(b)).astype(BF16)
    k_inv = (k * jnp.exp(-b)).astype(BF16)
    k_end = (k * jnp.exp(b_tot - b)).astype(BF16)
    a = lax.dot_general(q_dec, k_inv, (((1,), (1,)), ((), ())), preferred_element_type=F32)
    a = jnp.where(causal, a, 0.0).astype(BF16)
    o_intra = jnp.dot(a, v, preferred_element_type=F32)

    for ci in range(T // C):
        lo, hi = ci * C, (ci + 1) * C
        st = st_ref[...]
        o_c = o_intra[lo:hi, :] + lax.dot_general(
            q_dec[lo:hi, :], st.astype(BF16), (((1,), (1,)), ((), ())),
            preferred_element_type=F32)
        decay = jnp.exp(b_tot[lo:lo + 1, :])
        st_ref[...] = st * decay + lax.dot_general(
            v[lo:hi, :], k_end[lo:hi, :], (((0,), (0,)), ((), ())),
            preferred_element_type=F32)
        y = _rms(o_c, gnorm_ref[...])
        o_ref[lo:hi, :] = (y * _silu(gr_ref[lo:hi, :].astype(F32))).astype(o_ref.dtype)


def _chunk_causal_mask(n):
    r = jnp.arange(n)[:, None]
    c = jnp.arange(n)[None, :]
    return ((r // GLA_CHUNK == c // GLA_CHUNK) & (r >= c)).astype(F32)


def _gla(proj, small, wgate_pad, bgate, gnorm, *, bt):
    S = proj.shape[0]
    dk, dv = GLA_HEAD_K, GLA_HEAD_V
    return pl.pallas_call(
        _gla_kernel,
        name="gla",
        grid=(GLA_HEADS, S // bt),
        in_specs=[
            pl.BlockSpec((bt, dk), lambda h, t: (t, OFF_GQ // dk + h)),
            pl.BlockSpec((bt, dk), lambda h, t: (t, OFF_GK // dk + h)),
            pl.BlockSpec((bt, dv), lambda h, t: (t, OFF_GV // dv + h)),
            pl.BlockSpec((bt, dv), lambda h, t: (t, OFF_GR // dv + h)),
            pl.BlockSpec((bt, LANES), lambda h, t: (t, 0)),
            pl.BlockSpec((LANES, dk), lambda h, t: (0, h)),
            pl.BlockSpec((1, dk), lambda h, t: (0, h)),
            pl.BlockSpec((1, dv), lambda h, t: (0, 0)),
            pl.BlockSpec((bt, bt), lambda h, t: (0, 0)),
        ],
        out_specs=pl.BlockSpec((bt, dv), lambda h, t: (t, h)),
        out_shape=jax.ShapeDtypeStruct((S, GLA_WIDTH), BF16),
        scratch_shapes=[pltpu.VMEM((dv, dk), F32)],
        compiler_params=_params("parallel", "arbitrary"),
    )(proj, proj, proj, proj, small, wgate_pad, bgate, gnorm, _chunk_causal_mask(bt))


def _outproj_kernel(fox_ref, gla_ref, wt_ref, wb_ref, h_ref, g_ref, o_ref):
    n = pl.program_id(1)
    bn = wt_ref.shape[1]
    for sub in _sub_tiles(bn):
        cols = pl.ds(pl.multiple_of(n * bn + sub.start, MXU_COLS), MXU_COLS)
        o_ref[:, cols] = (jnp.dot(fox_ref[...], wt_ref[:, sub], preferred_element_type=F32)
                          + jnp.dot(gla_ref[...], wb_ref[:, sub], preferred_element_type=F32))

    @pl.when(n == pl.num_programs(1) - 1)
    def _epilogue():
        def emit(rows, y):
            o_ref[rows, :] = h_ref[rows, :] + y
        _rms_rows(o_ref, g_ref, emit)


def _outproj(o_fox, o_gla, w_o, h, g, *, bm, bn):
    S, D = h.shape
    half = o_fox.shape[1]
    return pl.pallas_call(
        _outproj_kernel,
        name="outproj",
        grid=(S // bm, D // bn),
        in_specs=[
            pl.BlockSpec((bm, half), lambda i, n: (i, 0)),
            pl.BlockSpec((bm, half), lambda i, n: (i, 0)),
            pl.BlockSpec((half, bn), lambda i, n: (0, n)),
            pl.BlockSpec((half, bn), lambda i, n: (1, n)),
            pl.BlockSpec((bm, D), lambda i, n: (i, 0)),
            pl.BlockSpec((1, D), lambda i, n: (0, 0)),
        ],
        out_specs=pl.BlockSpec((bm, D), lambda i, n: (i, 0)),
        out_shape=jax.ShapeDtypeStruct((S, D), F32),
        compiler_params=_params("parallel", "arbitrary"),
    )(o_fox, o_gla, w_o, w_o, h, g)


def _ple_kernel(h_ref, p_ref, wp_ref, g_ref, wg_ref, o_ref, hb_ref, e_ref):
    n = pl.program_id(1)
    bn = wg_ref.shape[1]

    @pl.when(n == 0)
    def _prologue():
        e_ref[...] = jnp.dot(p_ref[...].astype(BF16), wp_ref[...], preferred_element_type=F32)

        def emit(rows, y):
            hb_ref[rows, :] = h_ref[rows, :].astype(BF16)
            e_ref[rows, :] = y
        _rms_rows(e_ref, g_ref, emit)

    for sub in _sub_tiles(bn):
        cols = pl.ds(pl.multiple_of(n * bn + sub.start, MXU_COLS), MXU_COLS)
        gate = jax.nn.sigmoid(jnp.dot(hb_ref[...], wg_ref[:, sub], preferred_element_type=F32))
        o_ref[:, sub] = h_ref[:, cols] + e_ref[:, cols] * gate


def _ple(h, p, w_proj, g, w_gate, *, bm, bn):
    S, D = h.shape
    dp = p.shape[1]
    return pl.pallas_call(
        _ple_kernel,
        name="ple",
        grid=(S // bm, D // bn),
        in_specs=[
            pl.BlockSpec((bm, D), lambda i, n: (i, 0)),
            pl.BlockSpec((bm, dp), lambda i, n: (i, 0)),
            pl.BlockSpec((dp, D), lambda i, n: (0, 0)),
            pl.BlockSpec((1, D), lambda i, n: (0, 0)),
            pl.BlockSpec((D, bn), lambda i, n: (0, n)),
        ],
        out_specs=pl.BlockSpec((bm, bn), lambda i, n: (i, n)),
        out_shape=jax.ShapeDtypeStruct((S, D), F32),
        scratch_shapes=[pltpu.VMEM((bm, D), BF16), pltpu.VMEM((bm, D), F32)],
        compiler_params=_params("parallel", "arbitrary"),
    )(h, p, w_proj, g, w_gate)


def _tile(n, preferred):
    t = min(n, preferred)
    while n % t:
        t //= 2
    return t


def _split_w_in(w_in):
    D = w_in.shape[0]
    ff_lo = 3 * FOX_WIDTH
    ff_hi = ff_lo + FOX_HEADS
    glr_lo = ff_hi + 2 * GLA_KEY_WIDTH + 2 * GLA_WIDTH
    glr_hi = glr_lo + GLA_GATE_RANK
    w_t = w_in.T.astype(BF16)
    pad = jnp.zeros((LANES - FOX_HEADS - GLA_GATE_RANK, D), BF16)
    small_t = jnp.concatenate([w_t[ff_lo:ff_hi], w_t[glr_lo:glr_hi], pad], axis=0)
    return w_t, small_t, ff_hi


def kernel(x, p, ffn1_norm_pre, ffn1_w_gate, ffn1_w_up, ffn1_w_down, ffn1_norm_post, mix_norm_pre, w_in, fox_b_f, gla_w_gate, gla_b_gate, gla_norm_g, w_o, mix_norm_post, ffn2_norm_pre, ffn2_w_gate, ffn2_w_up, ffn2_w_down, ffn2_norm_post, ple_w_proj, ple_norm, ple_w_gate):
    B, S, D = x.shape
    assert B == 1, "the attention kernels treat the row axis as one sequence"
    depth = w_in.shape[0]
    h = x.reshape(S, D)

    bm = _tile(S, 512)
    bf = _tile(ffn1_w_gate.shape[-1], 256)
    bq = _tile(S, 512)
    bk = max(bq, _tile(S, 1024))
    bt = _tile(S, 512)

    col_scale = jnp.ones((1, MAIN_COLS), F32)
    col_scale = col_scale.at[:, OFF_FQ:OFF_FQ + FOX_WIDTH].set(FOX_HEAD_DIM ** -0.5 * LOG2_E)
    col_scale = col_scale.at[:, OFF_GQ:OFF_GQ + GLA_KEY_WIDTH].set(GLA_HEAD_K ** -0.5)

    def row(v):
        return v.reshape(1, -1).astype(F32)

    for i in range(depth):
        h = _ffn(h, row(ffn1_norm_pre[i]), ffn1_w_gate[i], ffn1_w_up[i], ffn1_w_down[i],
                 row(ffn1_norm_post[i]), bm=bm, bf=bf)

        w_t, w_small_t, rest_row = _split_w_in(w_in[i])
        bn = 1024
        proj, small = _inproj(h, row(mix_norm_pre[i]), w_t, w_small_t, col_scale, bm=_tile(S, 1024),
                              bn=bn, n1=3 * FOX_WIDTH // bn, rest_row=rest_row)

        bias_f = jnp.zeros((1, LANES), F32).at[0, :FOX_HEADS].set(fox_b_f[i])
        c2 = _fox_gate(small, bias_f, bt=bt)
        o_fox = _fox(proj, c2, bq=bq, bk=bk)

        wgate_pad = jnp.zeros((LANES, GLA_KEY_WIDTH), F32).at[
            FOX_HEADS:FOX_HEADS + GLA_GATE_RANK, :].set(gla_w_gate[i]).astype(BF16)
        o_gla = _gla(proj, small, wgate_pad, row(gla_b_gate[i]), row(gla_norm_g[i]), bt=bt)

        h = _outproj(o_fox, o_gla, w_o[i].astype(BF16), h, row(mix_norm_post[i]), bm=bm, bn=512)

        h = _ffn(h, row(ffn2_norm_pre[i]), ffn2_w_gate[i], ffn2_w_up[i], ffn2_w_down[i],
                 row(ffn2_norm_post[i]), bm=bm, bf=bf)

        h = _ple(h, p[i].reshape(S, -1), ple_w_proj[i].astype(BF16), row(ple_norm[i]),
                 ple_w_gate[i].astype(BF16), bm=bm, bn=1024)
    return h.reshape(B, S, D)
```

```python
import functools

import jax
import jax.numpy as jnp
import numpy as np
from jax import lax
from jax.experimental import pallas as pl
from jax.experimental.pallas import tpu as pltpu

F32 = jnp.float32
BF16 = jnp.bfloat16
HIGHEST = lax.Precision.HIGHEST

EPS = 1e-6
LOG2_E = 1.4426950408889634
MACARON_WEIGHT = 0.5
FOX_HEAD_DIM = 128
FOX_HEADS = 16
FOX_WIDTH = FOX_HEADS * FOX_HEAD_DIM
GLA_HEADS = 4
GLA_WIDTH = 2048
GLA_HEAD_V = GLA_WIDTH // GLA_HEADS
GLA_KEY_WIDTH = GLA_WIDTH // 2
GLA_HEAD_K = GLA_KEY_WIDTH // GLA_HEADS
GLA_GATE_RANK = 16
GLA_TAU = 16.0
GLA_CHUNK = 64

LANES = 128
MXU_COLS = 256
VMEM_LIMIT_BYTES = 56 * 1024 * 1024

MAIN_COLS = 3 * FOX_WIDTH + 2 * GLA_WIDTH + 2 * GLA_KEY_WIDTH
OFF_FQ = 0
OFF_FK = FOX_WIDTH
OFF_FV = 2 * FOX_WIDTH
OFF_GQ = 3 * FOX_WIDTH
OFF_GK = OFF_GQ + GLA_KEY_WIDTH
OFF_GV = OFF_GK + GLA_KEY_WIDTH
OFF_GR = OFF_GV + GLA_WIDTH

ROW_CHUNK = 32
ROW_BLOCK = 128


def _params(*semantics):
    return pltpu.CompilerParams(dimension_semantics=semantics,
                                vmem_limit_bytes=VMEM_LIMIT_BYTES)


def _rms(x, g):
    ms = jnp.mean(x * x, axis=-1, keepdims=True)
    return x * lax.rsqrt(ms + EPS) * g


def _log_sigmoid(x):
    return jnp.minimum(x, 0.0) - jnp.log(1.0 + jnp.exp(-jnp.abs(x)))


def _silu(x):
    return x * jax.nn.sigmoid(x)


def _sub_tiles(n_cols):
    return [slice(c, c + MXU_COLS) for c in range(0, n_cols, MXU_COLS)]


def _rms_rows(src_ref, g_ref, emit):
    def body(b, carry):
        chunks = [pl.ds(pl.multiple_of(b * ROW_BLOCK + c * ROW_CHUNK, ROW_CHUNK), ROW_CHUNK)
                  for c in range(ROW_BLOCK // ROW_CHUNK)]
        scales = []
        for rows in chunks:
            x = src_ref[rows, :]
            scales.append(lax.rsqrt(jnp.mean(x * x, axis=-1, keepdims=True) + EPS))
        for rows, scale in zip(chunks, scales):
            emit(rows, src_ref[rows, :] * scale * g_ref[...])
        return carry
    lax.fori_loop(0, src_ref.shape[0] // ROW_BLOCK, body, 0)


def _ffn_prologue(x_ref, gpre_ref, xn_ref, o_ref):
    def emit(rows, y):
        xn_ref[rows, :] = y.astype(BF16)
        o_ref[rows, :] = jnp.zeros((ROW_CHUNK, o_ref.shape[1]), F32)
    _rms_rows(x_ref, gpre_ref, emit)


def _ffn_epilogue(x_ref, gpost_ref, o_ref):
    def emit(rows, y):
        o_ref[rows, :] = x_ref[rows, :] + MACARON_WEIGHT * y
    _rms_rows(o_ref, gpost_ref, emit)


FFN_COPY_STEPS = 4


def _ffn_kernel(row_tab_ref, ff_tab_ref, x_ref, gpre_ref, wg_ref, wu_ref, wd_ref, gpost_ref,
                first_ref, o_ref, xn_ref, *, nf):
    del row_tab_ref
    s = pl.program_id(0)
    f = ff_tab_ref[s]
    rows_c = first_ref.shape[0]

    @pl.when(s == 0)
    def _():
        xn_ref[...] = jnp.zeros_like(xn_ref)
        o_ref[...] = jnp.zeros_like(o_ref)

    @pl.when(f == 0)
    def _():
        _ffn_prologue(x_ref, gpre_ref, xn_ref, o_ref)

    xn = xn_ref[...]
    g = jnp.dot(xn, wg_ref[...], preferred_element_type=F32)
    u = jnp.dot(xn, wu_ref[...], preferred_element_type=F32)
    hid = (_silu(g) * u).astype(BF16)
    o_ref[...] += jnp.dot(hid, wd_ref[...], preferred_element_type=F32)

    @pl.when(f == nf - 1)
    def _():
        _ffn_epilogue(x_ref, gpost_ref, o_ref)

    @pl.when(s < FFN_COPY_STEPS)
    def _():
        o_ref[pl.ds(pl.multiple_of(s * rows_c, rows_c), rows_c), :] = first_ref[...]


def _ffn_first_kernel(x_ref, gpre_ref, wg_ref, wu_ref, wd_ref, gpost_ref,
                      o_ref, wg16_ref, wu16_ref, wd16_ref, xn_ref, g_ref, u_ref):
    f = pl.program_id(0)
    k = pl.program_id(1)
    dk = wg_ref.shape[0]

    @pl.when(jnp.logical_and(f == 0, k == 0))
    def _():
        _ffn_prologue(x_ref, gpre_ref, xn_ref, o_ref)

    wg = wg_ref[...].astype(BF16)
    wu = wu_ref[...].astype(BF16)
    wg16_ref[...] = wg
    wu16_ref[...] = wu
    xk = xn_ref[:, pl.ds(pl.multiple_of(k * dk, dk), dk)]
    g = jnp.dot(xk, wg, preferred_element_type=F32)
    u = jnp.dot(xk, wu, preferred_element_type=F32)

    @pl.when(k == 0)
    def _():
        g_ref[...] = g
        u_ref[...] = u

    @pl.when(k == 1)
    def _():
        wd = wd_ref[...].astype(BF16)
        wd16_ref[...] = wd
        hid = (_silu(g_ref[...] + g) * (u_ref[...] + u)).astype(BF16)
        o_ref[...] += jnp.dot(hid, wd, preferred_element_type=F32)

    @pl.when(jnp.logical_and(f == pl.num_programs(0) - 1, k == 1))
    def _():
        _ffn_epilogue(x_ref, gpost_ref, o_ref)


def _ffn(x, g_pre, wg, wu, wd, g_post, *, bm, bf):
    S, D = x.shape
    d_ff = wg.shape[1]
    once = dict(pipeline_mode=pl.Buffered(1))
    h, wg16, wu16, wd16 = pl.pallas_call(
        _ffn_first_kernel,
        name="ffn_first",
        grid=(d_ff // bf, 2),
        in_specs=[
            pl.BlockSpec((bm, D), lambda f, k: (0, 0), **once),
            pl.BlockSpec((1, D), lambda f, k: (0, 0)),
            pl.BlockSpec((D // 2, bf), lambda f, k: (k, f)),
            pl.BlockSpec((D // 2, bf), lambda f, k: (k, f)),
            pl.BlockSpec((bf, D), lambda f, k: (f, 0)),
            pl.BlockSpec((1, D), lambda f, k: (0, 0)),
        ],
        out_specs=[
            pl.BlockSpec((bm, D), lambda f, k: (0, 0)),
            pl.BlockSpec((D // 2, bf), lambda f, k: (k, f)),
            pl.BlockSpec((D // 2, bf), lambda f, k: (k, f)),
            pl.BlockSpec((bf, D), lambda f, k: (f, 0)),
        ],
        out_shape=[
            jax.ShapeDtypeStruct((bm, D), F32),
            jax.ShapeDtypeStruct((D, d_ff), BF16),
            jax.ShapeDtypeStruct((D, d_ff), BF16),
            jax.ShapeDtypeStruct((d_ff, D), BF16),
        ],
        scratch_shapes=[pltpu.VMEM((bm, D), BF16), pltpu.VMEM((bm, bf), F32), pltpu.VMEM((bm, bf), F32)],
        compiler_params=_params("arbitrary", "arbitrary"),
    )(x, g_pre, wg, wu, wd, g_post)
    if S == bm:
        return h
    nf = d_ff // bf
    c = FFN_COPY_STEPS
    assert nf > 2, "copy steps are tagged with d_ff tile 1, which must be neither first nor last"

    steps = np.arange(c + (S // bm - 1) * nf)
    row_tab = np.where(steps < c, 0, 1 + (steps - c) // nf).astype(np.int32)
    ff_tab = np.where(steps < c, 1, (steps - c) % nf).astype(np.int32)
    grid_spec = pltpu.PrefetchScalarGridSpec(
        num_scalar_prefetch=2,
        grid=(steps.size,),
        in_specs=[
            pl.BlockSpec((bm, D), lambda s, rt, ft: (rt[s], 0)),
            pl.BlockSpec((1, D), lambda s, rt, ft: (0, 0)),
            pl.BlockSpec((D, bf), lambda s, rt, ft: (0, ft[s])),
            pl.BlockSpec((D, bf), lambda s, rt, ft: (0, ft[s])),
            pl.BlockSpec((bf, D), lambda s, rt, ft: (ft[s], 0)),
            pl.BlockSpec((1, D), lambda s, rt, ft: (0, 0)),
            pl.BlockSpec((bm // c, D), lambda s, rt, ft: (jnp.minimum(s, c - 1), 0)),
        ],
        out_specs=pl.BlockSpec((bm, D), lambda s, rt, ft: (rt[s], 0)),
        scratch_shapes=[pltpu.VMEM((bm, D), BF16)],
    )
    return pl.pallas_call(
        functools.partial(_ffn_kernel, nf=nf),
        name="ffn",
        grid_spec=grid_spec,
        out_shape=jax.ShapeDtypeStruct((S, D), F32),
        compiler_params=_params("arbitrary"),
    )(jnp.asarray(row_tab), jnp.asarray(ff_tab), x, g_pre, wg16, wu16, wd16, g_post, h)


NT_DIMS = (((1,), (1,)), ((), ()))


def _inproj_kernel(h_ref, g_ref, wt_ref, wst_ref, scale_ref, o_ref, os_ref, a_ref):
    n = pl.program_id(1)

    @pl.when(n == 0)
    def _prologue():
        def emit(rows, y):
            a_ref[rows, :] = y.astype(BF16)
        _rms_rows(h_ref, g_ref, emit)
        os_ref[...] = lax.dot_general(a_ref[...], wst_ref[...], NT_DIMS, preferred_element_type=F32)

    for cols in _sub_tiles(wt_ref.shape[0]):
        acc = lax.dot_general(a_ref[...], wt_ref[cols, :], NT_DIMS, preferred_element_type=F32)
        o_ref[:, cols] = (acc * scale_ref[:, cols]).astype(BF16)


def _inproj(h, g, w_t, w_small_t, col_scale, *, bm, bn, n1, rest_row):
    S, D = h.shape
    n_cols = col_scale.shape[1]

    def w_row(i, n):
        return pl.multiple_of(jnp.where(n < n1, n * bn, rest_row + (n - n1) * bn), FOX_HEADS)

    return pl.pallas_call(
        _inproj_kernel,
        name="inproj",
        grid=(S // bm, n_cols // bn),
        in_specs=[
            pl.BlockSpec((bm, D), lambda i, n: (i, 0), pipeline_mode=pl.Buffered(1)),
            pl.BlockSpec((1, D), lambda i, n: (0, 0)),
            pl.BlockSpec((pl.Element(bn), pl.Element(D)), lambda i, n: (w_row(i, n), 0)),
            pl.BlockSpec((LANES, D), lambda i, n: (0, 0)),
            pl.BlockSpec((1, bn), lambda i, n: (0, n)),
        ],
        out_specs=[
            pl.BlockSpec((bm, bn), lambda i, n: (i, n)),
            pl.BlockSpec((bm, LANES), lambda i, n: (i, 0)),
        ],
        out_shape=[
            jax.ShapeDtypeStruct((S, n_cols), BF16),
            jax.ShapeDtypeStruct((S, LANES), F32),
        ],
        scratch_shapes=[pltpu.VMEM((bm, D), BF16)],
        compiler_params=_params("parallel", "arbitrary"),
    )(h, g, w_t, w_small_t, col_scale)


def _fox_gate_kernel(small_ref, bias_ref, c_ref, carry_ref):
    t = pl.program_id(0)
    T = small_ref.shape[0]

    @pl.when(t == 0)
    def _init():
        carry_ref[...] = jnp.zeros_like(carry_ref)

    log_f = _log_sigmoid(small_ref[...] + bias_ref[...])
    row = lax.broadcasted_iota(jnp.int32, (T, T), 0)
    col = lax.broadcasted_iota(jnp.int32, (T, T), 1)
    tri = jnp.where(row >= col, 1.0, 0.0).astype(F32)
    c = jnp.dot(tri, log_f, precision=HIGHEST, preferred_element_type=F32) + carry_ref[...]
    c_ref[...] = c * LOG2_E
    carry_ref[...] = c[T - 1:T, :]


def _fox_gate(small, bias, *, bt):
    S = small.shape[0]
    return pl.pallas_call(
        _fox_gate_kernel,
        name="fox_gate",
        grid=(S // bt,),
        in_specs=[
            pl.BlockSpec((bt, LANES), lambda t: (t, 0)),
            pl.BlockSpec((1, LANES), lambda t: (0, 0)),
        ],
        out_specs=pl.BlockSpec((bt, LANES), lambda t: (t, 0)),
        out_shape=jax.ShapeDtypeStruct((S, LANES), F32),
        scratch_shapes=[pltpu.VMEM((1, LANES), F32)],
        compiler_params=_params("arbitrary"),
    )(small, bias)


FOX_PAIR = 2
FOX_ROWS = 16


def _split3(c):
    hi = c.astype(BF16).astype(F32)
    mid = (c - hi).astype(BF16).astype(F32)
    lo = (c - hi - mid).astype(BF16).astype(F32)
    return hi, mid, lo


def _lane_tile(x, n):
    return jnp.concatenate([x] * n, axis=1)


def _head_column(block, head):
    lane = lax.broadcasted_iota(jnp.int32, block.shape, 1)
    return jnp.sum(jnp.where(lane == head, block, 0.0), axis=1, keepdims=True)


def _fox_kernel(q_ref, k_ref, v_ref, c_ref, o_ref, kaug_ref, vaug_ref, qaug_ref, s_ref, p_ref,
                m_ref, alpha_ref, acc_ref, *, bk):
    hp = pl.program_id(0)
    i = pl.program_id(1)
    bq = q_ref.shape[0]
    S = k_ref.shape[0]
    dh = FOX_HEAD_DIM
    R = FOX_ROWS

    @pl.when(i == 0)
    def _build_kv_side():
        rows_per = 128

        def body(r, carry):
            rows = pl.ds(pl.multiple_of(r * rows_per, rows_per), rows_per)
            cblk = c_ref[rows, :]
            lane = lax.broadcasted_iota(jnp.int32, (rows_per, LANES), 1)
            for hh in range(FOX_PAIR):
                hi, mid, lo = _split3(_head_column(cblk, hp * FOX_PAIR + hh))
                extra = jnp.where(lane < 3, 1.0,
                                  jnp.where(lane == 3, -hi,
                                            jnp.where(lane == 4, -mid,
                                                      jnp.where(lane == 5, -lo, 0.0))))
                kaug_ref[hh, rows, 0:dh] = k_ref[rows, hh * dh:(hh + 1) * dh]
                kaug_ref[hh, rows, dh:2 * dh] = extra.astype(BF16)
                vaug_ref[hh, rows, 0:dh] = v_ref[rows, hh * dh:(hh + 1) * dh]
                vaug_ref[hh, rows, dh:2 * dh] = jnp.ones((rows_per, dh), BF16)
            return carry
        lax.fori_loop(0, S // rows_per, body, 0, unroll=4)

    cq_blk = c_ref[pl.ds(pl.multiple_of(i * bq, bq), bq), :]
    lane_q = lax.broadcasted_iota(jnp.int32, (bq, LANES), 1)
    for hh in range(FOX_PAIR):
        hi, mid, lo = _split3(_head_column(cq_blk, hp * FOX_PAIR + hh))
        extra = jnp.where(lane_q == 0, hi,
                          jnp.where(lane_q == 1, mid,
                                    jnp.where(lane_q == 2, lo,
                                              jnp.where(lane_q < 6, 1.0, 0.0))))
        qaug_ref[hh, :, 0:dh] = q_ref[:, hh * dh:(hh + 1) * dh]
        qaug_ref[hh, :, dh:2 * dh] = extra.astype(BF16)
        m_ref[hh] = jnp.full((bq, LANES), -jnp.inf, F32)
        acc_ref[hh] = jnp.zeros((bq, 2 * dh), F32)

    groups = [slice(g * R, (g + 1) * R) for g in range(bq // R)]

    def chunk(k0, width, masked):
        ks = pl.ds(pl.multiple_of(k0, bq), width)
        for hh in range(FOX_PAIR):
            s_ref[hh, :, 0:width] = lax.dot_general(
                qaug_ref[hh], kaug_ref[hh, ks, :], (((1,), (1,)), ((), ())),
                preferred_element_type=F32)
        if masked:
            diff = (lax.broadcasted_iota(jnp.int32, (R, width), 1)
                    - lax.broadcasted_iota(jnp.int32, (R, width), 0))
            limit = i * bq - k0
        for hh in range(FOX_PAIR):
            for g, rows in enumerate(groups):
                s = s_ref[hh, rows, 0:width]
                if masked:
                    s = jnp.where(diff <= limit + g * R, s, -jnp.inf)
                    s_ref[hh, rows, 0:width] = s
                m_old = m_ref[hh, rows, :]
                m_new = jnp.maximum(m_old, jnp.broadcast_to(
                    jnp.max(s, axis=1, keepdims=True), (R, LANES)))
                alpha_ref[hh, rows, :] = jnp.exp2(m_old - m_new)
                m_ref[hh, rows, :] = m_new
        for hh in range(FOX_PAIR):
            for rows in groups:
                m = _lane_tile(m_ref[hh, rows, :], width // LANES)
                p_ref[hh, rows, 0:width] = jnp.exp2(s_ref[hh, rows, 0:width] - m).astype(BF16)
        for hh in range(FOX_PAIR):
            alpha = _lane_tile(alpha_ref[hh], 2)
            acc_ref[hh] = alpha * acc_ref[hh] + jnp.dot(
                p_ref[hh, :, 0:width], vaug_ref[hh, ks, :], preferred_element_type=F32)

    n_full = (i * bq) // bk

    def pair_body(j, carry):
        chunk(2 * j * bk, bk, False)
        chunk((2 * j + 1) * bk, bk, False)
        return carry
    lax.fori_loop(0, n_full // 2, pair_body, 0)

    @pl.when(n_full % 2 == 1)
    def _odd():
        chunk((n_full - 1) * bk, bk, False)

    tail_start = n_full * bk
    tail_blocks = (i + 1) - n_full * (bk // bq)
    for t in range(1, bk // bq + 1):
        @pl.when(tail_blocks == t)
        def _tail(t=t):
            chunk(tail_start, t * bq, True)

    for hh in range(FOX_PAIR):
        acc = acc_ref[hh]
        o_ref[:, hh * dh:(hh + 1) * dh] = (acc[:, 0:dh] / acc[:, dh:2 * dh]).astype(o_ref.dtype)


def _fox(proj, c2, *, bq, bk):
    S = proj.shape[0]
    dh = FOX_HEAD_DIM
    w = FOX_PAIR * dh
    return pl.pallas_call(
        functools.partial(_fox_kernel, bk=bk),
        name="fox",
        grid=(FOX_HEADS // FOX_PAIR, S // bq),
        in_specs=[
            pl.BlockSpec((bq, w), lambda h, i: (i, OFF_FQ // w + h)),
            pl.BlockSpec((S, w), lambda h, i: (0, OFF_FK // w + h)),
            pl.BlockSpec((S, w), lambda h, i: (0, OFF_FV // w + h)),
            pl.BlockSpec((S, LANES), lambda h, i: (0, 0)),
        ],
        out_specs=pl.BlockSpec((bq, w), lambda h, i: (i, h)),
        out_shape=jax.ShapeDtypeStruct((S, FOX_WIDTH), BF16),
        scratch_shapes=[
            pltpu.VMEM((FOX_PAIR, S, 2 * dh), BF16),
            pltpu.VMEM((FOX_PAIR, S, 2 * dh), BF16),
            pltpu.VMEM((FOX_PAIR, bq, 2 * dh), BF16),
            pltpu.VMEM((FOX_PAIR, bq, bk), F32),
            pltpu.VMEM((FOX_PAIR, bq, bk), BF16),
            pltpu.VMEM((FOX_PAIR, bq, LANES), F32),
            pltpu.VMEM((FOX_PAIR, bq, LANES), F32),
            pltpu.VMEM((FOX_PAIR, bq, 2 * dh), F32),
        ],
        compiler_params=_params("parallel", "arbitrary"),
    )(proj, proj, proj, c2)


def _gla_kernel(q_ref, k_ref, v_ref, gr_ref, small_ref, wgate_ref, bgate_ref,
                gnorm_ref, tri_ref, o_ref, st_ref):
    t = pl.program_id(1)
    T = q_ref.shape[0]
    C = GLA_CHUNK

    @pl.when(t == 0)
    def _init():
        st_ref[...] = jnp.zeros_like(st_ref)

    dk = q_ref.shape[1]
    gate = jnp.dot(small_ref[...].astype(BF16), wgate_ref[...],
                   preferred_element_type=F32) + bgate_ref[...]
    log_a = _log_sigmoid(gate) * (1.0 / GLA_TAU)

    pieces = jnp.concatenate([x.astype(BF16) for x in _split3(log_a)], axis=1)
    sums = jnp.dot(tri_ref[...].astype(BF16), pieces, preferred_element_type=F32)
    b = sums[:, 0:dk] + sums[:, dk:2 * dk] + sums[:, 2 * dk:3 * dk]
    b_tot = jnp.concatenate(
        [jnp.broadcast_to(b[c * C + C - 1:c * C + C, :], (C, dk)) for c in range(T // C)], axis=0)
    causal = tri_ref[...] > 0.0

    q = q_ref[...].astype(F32)
    k = k_ref[...].astype(F32)
    v = v_ref[...]
    q_dec = (q * jnp.exp(b)).astype(BF16)
    k_inv = (k * jnp.exp(-b)).astype(BF16)
    k_end = (k * jnp.exp(b_tot - b)).astype(BF16)
    a = lax.dot_general(q_dec, k_inv, (((1,), (1,)), ((), ())), preferred_element_type=F32)
    a = jnp.where(causal, a, 0.0).astype(BF16)
    o_intra = jnp.dot(a, v, preferred_element_type=F32)

    for ci in range(T // C):
        lo, hi = ci * C, (ci + 1) * C
        st = st_ref[...]
        o_c = o_intra[lo:hi, :] + lax.dot_general(
            q_dec[lo:hi, :], st.astype(BF16), (((1,), (1,)), ((), ())),
            preferred_element_type=F32)
        decay = jnp.exp(b_tot[lo:lo + 1, :])
        st_ref[...] = st * decay + lax.dot_general(
            v[lo:hi, :], k_end[lo:hi, :], (((0,), (0,)), ((), ())),
            preferred_element_type=F32)
        y = _rms(o_c, gnorm_ref[...])
        o_ref[lo:hi, :] = (y * _silu(gr_ref[lo:hi, :].astype(F32))).astype(o_ref.dtype)


def _chunk_causal_mask(n):
    r = jnp.arange(n)[:, None]
    c = jnp.arange(n)[None, :]
    return ((r // GLA_CHUNK == c // GLA_CHUNK) & (r >= c)).astype(F32)


def _gla(proj, small, wgate_pad, bgate, gnorm, *, bt):
    S = proj.shape[0]
    dk, dv = GLA_HEAD_K, GLA_HEAD_V
    return pl.pallas_call(
        _gla_kernel,
        name="gla",
        grid=(GLA_HEADS, S // bt),
        in_specs=[
            pl.BlockSpec((bt, dk), lambda h, t: (t, OFF_GQ // dk + h)),
            pl.BlockSpec((bt, dk), lambda h, t: (t, OFF_GK // dk + h)),
            pl.BlockSpec((bt, dv), lambda h, t: (t, OFF_GV // dv + h)),
            pl.BlockSpec((bt, dv), lambda h, t: (t, OFF_GR // dv + h)),
            pl.BlockSpec((bt, LANES), lambda h, t: (t, 0)),
            pl.BlockSpec((LANES, dk), lambda h, t: (0, h)),
            pl.BlockSpec((1, dk), lambda h, t: (0, h)),
            pl.BlockSpec((1, dv), lambda h, t: (0, 0)),
            pl.BlockSpec((bt, bt), lambda h, t: (0, 0)),
        ],
        out_specs=pl.BlockSpec((bt, dv), lambda h, t: (t, h)),
        out_shape=jax.ShapeDtypeStruct((S, GLA_WIDTH), BF16),
        scratch_shapes=[pltpu.VMEM((dv, dk), F32)],
        compiler_params=_params("parallel", "arbitrary"),
    )(proj, proj, proj, proj, small, wgate_pad, bgate, gnorm, _chunk_causal_mask(bt))


def _outproj_kernel(fox_ref, gla_ref, wt_ref, wb_ref, h_ref, g_ref, o_ref):
    n = pl.program_id(1)
    bn = wt_ref.shape[1]
    for sub in _sub_tiles(bn):
        cols = pl.ds(pl.multiple_of(n * bn + sub.start, MXU_COLS), MXU_COLS)
        o_ref[:, cols] = (jnp.dot(fox_ref[...], wt_ref[:, sub], preferred_element_type=F32)
                          + jnp.dot(gla_ref[...], wb_ref[:, sub], preferred_element_type=F32))

    @pl.when(n == pl.num_programs(1) - 1)
    def _epilogue():
        def emit(rows, y):
            o_ref[rows, :] = h_ref[rows, :] + y
        _rms_rows(o_ref, g_ref, emit)


def _outproj(o_fox, o_gla, w_o, h, g, *, bm, bn):
    S, D = h.shape
    half = o_fox.shape[1]
    return pl.pallas_call(
        _outproj_kernel,
        name="outproj",
        grid=(S // bm, D // bn),
        in_specs=[
            pl.BlockSpec((bm, half), lambda i, n: (i, 0)),
            pl.BlockSpec((bm, half), lambda i, n: (i, 0)),
            pl.BlockSpec((half, bn), lambda i, n: (0, n)),
            pl.BlockSpec((half, bn), lambda i, n: (1, n)),
            pl.BlockSpec((bm, D), lambda i, n: (i, 0)),
            pl.BlockSpec((1, D), lambda i, n: (0, 0)),
        ],
        out_specs=pl.BlockSpec((bm, D), lambda i, n: (i, 0)),
        out_shape=jax.ShapeDtypeStruct((S, D), F32),
        compiler_params=_params("parallel", "arbitrary"),
    )(o_fox, o_gla, w_o, w_o, h, g)


def _ple_kernel(h_ref, p_ref, wp_ref, g_ref, wg_ref, o_ref, hb_ref, e_ref):
    n = pl.program_id(1)
    bn = wg_ref.shape[1]

    @pl.when(n == 0)
    def _prologue():
        e_ref[...] = jnp.dot(p_ref[...].astype(BF16), wp_ref[...], preferred_element_type=F32)

        def emit(rows, y):
            hb_ref[rows, :] = h_ref[rows, :].astype(BF16)
            e_ref[rows, :] = y
        _rms_rows(e_ref, g_ref, emit)

    for sub in _sub_tiles(bn):
        cols = pl.ds(pl.multiple_of(n * bn + sub.start, MXU_COLS), MXU_COLS)
        gate = jax.nn.sigmoid(jnp.dot(hb_ref[...], wg_ref[:, sub], preferred_element_type=F32))
        o_ref[:, sub] = h_ref[:, cols] + e_ref[:, cols] * gate


def _ple(h, p, w_proj, g, w_gate, *, bm, bn):
    S, D = h.shape
    dp = p.shape[1]
    return pl.pallas_call(
        _ple_kernel,
        name="ple",
        grid=(S // bm, D // bn),
        in_specs=[
            pl.BlockSpec((bm, D), lambda i, n: (i, 0)),
            pl.BlockSpec((bm, dp), lambda i, n: (i, 0)),
            pl.BlockSpec((dp, D), lambda i, n: (0, 0)),
            pl.BlockSpec((1, D), lambda i, n: (0, 0)),
            pl.BlockSpec((D, bn), lambda i, n: (0, n)),
        ],
        out_specs=pl.BlockSpec((bm, bn), lambda i, n: (i, n)),
        out_shape=jax.ShapeDtypeStruct((S, D), F32),
        scratch_shapes=[pltpu.VMEM((bm, D), BF16), pltpu.VMEM((bm, D), F32)],
        compiler_params=_params("parallel", "arbitrary"),
    )(h, p, w_proj, g, w_gate)


def _tile(n, preferred):
    t = min(n, preferred)
    while n % t:
        t //= 2
    return t


def _split_w_in(w_in):
    D = w_in.shape[0]
    ff_lo = 3 * FOX_WIDTH
    ff_hi = ff_lo + FOX_HEADS
    glr_lo = ff_hi + 2 * GLA_KEY_WIDTH + 2 * GLA_WIDTH
    glr_hi = glr_lo + GLA_GATE_RANK
    w_t = w_in.T.astype(BF16)
    pad = jnp.zeros((LANES - FOX_HEADS - GLA_GATE_RANK, D), BF16)
    small_t = jnp.concatenate([w_t[ff_lo:ff_hi], w_t[glr_lo:glr_hi], pad], axis=0)
    return w_t, small_t, ff_hi


def kernel(x, p, ffn1_norm_pre, ffn1_w_gate, ffn1_w_up, ffn1_w_down, ffn1_norm_post, mix_norm_pre, w_in, fox_b_f, gla_w_gate, gla_b_gate, gla_norm_g, w_o, mix_norm_post, ffn2_norm_pre, ffn2_w_gate, ffn2_w_up, ffn2_w_down, ffn2_norm_post, ple_w_proj, ple_norm, ple_w_gate):
    B, S, D = x.shape
    assert B == 1, "the attention kernels treat the row axis as one sequence"
    depth = w_in.shape[0]
    h = x.reshape(S, D)

    bm = _tile(S, 512)
    bf = _tile(ffn1_w_gate.shape[-1], 256)
    bq = _tile(S, 512)
    bk = max(bq, _tile(S, 1024))
    bt = _tile(S, 512)

    col_scale = jnp.ones((1, MAIN_COLS), F32)
    col_scale = col_scale.at[:, OFF_FQ:OFF_FQ + FOX_WIDTH].set(FOX_HEAD_DIM ** -0.5 * LOG2_E)
    col_scale = col_scale.at[:, OFF_GQ:OFF_GQ + GLA_KEY_WIDTH].set(GLA_HEAD_K ** -0.5)

    def row(v):
        return v.reshape(1, -1).astype(F32)

    for i in range(depth):
        h = _ffn(h, row(ffn1_norm_pre[i]), ffn1_w_gate[i], ffn1_w_up[i], ffn1_w_down[i],
                 row(ffn1_norm_post[i]), bm=bm, bf=bf)

        w_t, w_small_t, rest_row = _split_w_in(w_in[i])
        bn = 1024
        proj, small = _inproj(h, row(mix_norm_pre[i]), w_t, w_small_t, col_scale, bm=_tile(S, 1024),
                              bn=bn, n1=3 * FOX_WIDTH // bn, rest_row=rest_row)

        bias_f = jnp.zeros((1, LANES), F32).at[0, :FOX_HEADS].set(fox_b_f[i])
        c2 = _fox_gate(small, bias_f, bt=bt)
        o_fox = _fox(proj, c2, bq=bq, bk=bk)

        wgate_pad = jnp.zeros((LANES, GLA_KEY_WIDTH), F32).at[
            FOX_HEADS:FOX_HEADS + GLA_GATE_RANK, :].set(gla_w_gate[i]).astype(BF16)
        o_gla = _gla(proj, small, wgate_pad, row(gla_b_gate[i]), row(gla_norm_g[i]), bt=bt)

        h = _outproj(o_fox, o_gla, w_o[i].astype(BF16), h, row(mix_norm_post[i]), bm=bm, bn=512)

        h = _ffn(h, row(ffn2_norm_pre[i]), ffn2_w_gate[i], ffn2_w_up[i], ffn2_w_down[i],
                 row(ffn2_norm_post[i]), bm=bm, bf=bf)

        h = _ple(h, p[i].reshape(S, -1), ple_w_proj[i].astype(BF16), row(ple_norm[i]),
                 ple_w_gate[i].astype(BF16), bm=bm, bn=1024)
    return h.reshape(B, S, D)
```

```python
import functools

import jax
import jax.numpy as jnp
from jax import lax
from jax.experimental import pallas as pl
from jax.experimental.pallas import tpu as pltpu

F32 = jnp.float32
BF16 = jnp.bfloat16
HIGHEST = lax.Precision.HIGHEST

EPS = 1e-6
LOG2_E = 1.4426950408889634
MACARON_WEIGHT = 0.5
FOX_HEAD_DIM = 128
FOX_HEADS = 16
FOX_WIDTH = FOX_HEADS * FOX_HEAD_DIM
GLA_HEADS = 4
GLA_WIDTH = 2048
GLA_HEAD_V = GLA_WIDTH // GLA_HEADS
GLA_KEY_WIDTH = GLA_WIDTH // 2
GLA_HEAD_K = GLA_KEY_WIDTH // GLA_HEADS
GLA_GATE_RANK = 16
GLA_TAU = 16.0
GLA_CHUNK = 64

LANES = 128
MXU_COLS = 256
VMEM_LIMIT_BYTES = 56 * 1024 * 1024

MAIN_COLS = 3 * FOX_WIDTH + 2 * GLA_WIDTH + 2 * GLA_KEY_WIDTH
OFF_FQ = 0
OFF_FK = FOX_WIDTH
OFF_FV = 2 * FOX_WIDTH
OFF_GQ = 3 * FOX_WIDTH
OFF_GK = OFF_GQ + GLA_KEY_WIDTH
OFF_GV = OFF_GK + GLA_KEY_WIDTH
OFF_GR = OFF_GV + GLA_WIDTH

ROW_CHUNK = 32
ROW_BLOCK = 128


def _params(*semantics):
    return pltpu.CompilerParams(dimension_semantics=semantics,
                                vmem_limit_bytes=VMEM_LIMIT_BYTES)


def _rms(x, g):
    ms = jnp.mean(x * x, axis=-1, keepdims=True)
    return x * lax.rsqrt(ms + EPS) * g


def _log_sigmoid(x):
    return jnp.minimum(x, 0.0) - jnp.log(1.0 + jnp.exp(-jnp.abs(x)))


def _silu(x):
    return x * jax.nn.sigmoid(x)


def _sub_tiles(n_cols):
    return [slice(c, c + MXU_COLS) for c in range(0, n_cols, MXU_COLS)]


def _rms_rows(src_ref, g_ref, emit):
    def body(b, carry):
        chunks = [pl.ds(pl.multiple_of(b * ROW_BLOCK + c * ROW_CHUNK, ROW_CHUNK), ROW_CHUNK)
                  for c in range(ROW_BLOCK // ROW_CHUNK)]
        scales = []
        for rows in chunks:
            x = src_ref[rows, :]
            scales.append(lax.rsqrt(jnp.mean(x * x, axis=-1, keepdims=True) + EPS))
        for rows, scale in zip(chunks, scales):
            emit(rows, src_ref[rows, :] * scale * g_ref[...])
        return carry
    lax.fori_loop(0, src_ref.shape[0] // ROW_BLOCK, body, 0)


def _ffn_prologue(x_ref, gpre_ref, xn_ref, o_ref):
    def emit(rows, y):
        xn_ref[rows, :] = y.astype(BF16)
        o_ref[rows, :] = jnp.zeros((ROW_CHUNK, o_ref.shape[1]), F32)
    _rms_rows(x_ref, gpre_ref, emit)


def _ffn_epilogue(x_ref, gpost_ref, o_ref):
    def emit(rows, y):
        o_ref[rows, :] = x_ref[rows, :] + MACARON_WEIGHT * y
    _rms_rows(o_ref, gpost_ref, emit)


FFN_COPY_STEPS = 4


def _ffn_kernel(x_ref, gpre_ref, wg_ref, wu_ref, wd_ref, gpost_ref, first_ref, o_ref, xn_ref):
    i = pl.program_id(0)
    f = pl.program_id(1)
    rows_c = first_ref.shape[0]

    @pl.when(jnp.logical_and(i == 0, f < FFN_COPY_STEPS))
    def _():
        o_ref[pl.ds(pl.multiple_of(f * rows_c, rows_c), rows_c), :] = first_ref[...]

    @pl.when(i > 0)
    def _():
        @pl.when(f == 0)
        def _():
            _ffn_prologue(x_ref, gpre_ref, xn_ref, o_ref)

        xn = xn_ref[...]
        g = jnp.dot(xn, wg_ref[...], preferred_element_type=F32)
        u = jnp.dot(xn, wu_ref[...], preferred_element_type=F32)
        hid = (_silu(g) * u).astype(BF16)
        o_ref[...] += jnp.dot(hid, wd_ref[...], preferred_element_type=F32)

        @pl.when(f == pl.num_programs(1) - 1)
        def _():
            _ffn_epilogue(x_ref, gpost_ref, o_ref)


def _ffn_first_kernel(x_ref, gpre_ref, wg_ref, wu_ref, wd_ref, gpost_ref,
                      o_ref, wg16_ref, wu16_ref, wd16_ref, xn_ref, g_ref, u_ref):
    f = pl.program_id(0)
    k = pl.program_id(1)
    dk = wg_ref.shape[0]

    @pl.when(jnp.logical_and(f == 0, k == 0))
    def _():
        _ffn_prologue(x_ref, gpre_ref, xn_ref, o_ref)

    wg = wg_ref[...].astype(BF16)
    wu = wu_ref[...].astype(BF16)
    wg16_ref[...] = wg
    wu16_ref[...] = wu
    xk = xn_ref[:, pl.ds(pl.multiple_of(k * dk, dk), dk)]
    g = jnp.dot(xk, wg, preferred_element_type=F32)
    u = jnp.dot(xk, wu, preferred_element_type=F32)

    @pl.when(k == 0)
    def _():
        g_ref[...] = g
        u_ref[...] = u

    @pl.when(k == 1)
    def _():
        wd = wd_ref[...].astype(BF16)
        wd16_ref[...] = wd
        hid = (_silu(g_ref[...] + g) * (u_ref[...] + u)).astype(BF16)
        o_ref[...] += jnp.dot(hid, wd, preferred_element_type=F32)

    @pl.when(jnp.logical_and(f == pl.num_programs(0) - 1, k == 1))
    def _():
        _ffn_epilogue(x_ref, gpost_ref, o_ref)


def _ffn(x, g_pre, wg, wu, wd, g_post, *, bm, bf):
    S, D = x.shape
    d_ff = wg.shape[1]
    once = dict(pipeline_mode=pl.Buffered(1))
    h, wg16, wu16, wd16 = pl.pallas_call(
        _ffn_first_kernel,
        name="ffn_first",
        grid=(d_ff // bf, 2),
        in_specs=[
            pl.BlockSpec((bm, D), lambda f, k: (0, 0), **once),
            pl.BlockSpec((1, D), lambda f, k: (0, 0)),
            pl.BlockSpec((D // 2, bf), lambda f, k: (k, f)),
            pl.BlockSpec((D // 2, bf), lambda f, k: (k, f)),
            pl.BlockSpec((bf, D), lambda f, k: (f, 0)),
            pl.BlockSpec((1, D), lambda f, k: (0, 0)),
        ],
        out_specs=[
            pl.BlockSpec((bm, D), lambda f, k: (0, 0)),
            pl.BlockSpec((D // 2, bf), lambda f, k: (k, f)),
            pl.BlockSpec((D // 2, bf), lambda f, k: (k, f)),
            pl.BlockSpec((bf, D), lambda f, k: (f, 0)),
        ],
        out_shape=[
            jax.ShapeDtypeStruct((bm, D), F32),
            jax.ShapeDtypeStruct((D, d_ff), BF16),
            jax.ShapeDtypeStruct((D, d_ff), BF16),
            jax.ShapeDtypeStruct((d_ff, D), BF16),
        ],
        scratch_shapes=[pltpu.VMEM((bm, D), BF16), pltpu.VMEM((bm, bf), F32), pltpu.VMEM((bm, bf), F32)],
        compiler_params=_params("arbitrary", "arbitrary"),
    )(x, g_pre, wg, wu, wd, g_post)
    if S == bm:
        return h
    nf = d_ff // bf
    c = FFN_COPY_STEPS
    assert nf >= c

    def ff_tile(i, f):
        return jnp.where(i == 0, 0, f)

    return pl.pallas_call(
        _ffn_kernel,
        name="ffn",
        grid=(S // bm, nf),
        in_specs=[
            pl.BlockSpec((bm, D), lambda i, f: (i, 0)),
            pl.BlockSpec((1, D), lambda i, f: (0, 0)),
            pl.BlockSpec((D, bf), lambda i, f: (0, ff_tile(i, f))),
            pl.BlockSpec((D, bf), lambda i, f: (0, ff_tile(i, f))),
            pl.BlockSpec((bf, D), lambda i, f: (ff_tile(i, f), 0)),
            pl.BlockSpec((1, D), lambda i, f: (0, 0)),
            pl.BlockSpec((bm // c, D), lambda i, f: (jnp.where(i == 0, jnp.minimum(f, c - 1), c - 1), 0)),
        ],
        out_specs=pl.BlockSpec((bm, D), lambda i, f: (i, 0)),
        out_shape=jax.ShapeDtypeStruct((S, D), F32),
        scratch_shapes=[pltpu.VMEM((bm, D), BF16)],
        compiler_params=_params("parallel", "arbitrary"),
    )(x, g_pre, wg16, wu16, wd16, g_post, h)


NT_DIMS = (((1,), (1,)), ((), ()))


def _inproj_kernel(h_ref, g_ref, wt_ref, wst_ref, scale_ref, o_ref, os_ref, a_ref):
    n = pl.program_id(1)

    @pl.when(n == 0)
    def _prologue():
        def emit(rows, y):
            a_ref[rows, :] = y.astype(BF16)
        _rms_rows(h_ref, g_ref, emit)
        os_ref[...] = lax.dot_general(a_ref[...], wst_ref[...], NT_DIMS, preferred_element_type=F32)

    for cols in _sub_tiles(wt_ref.shape[0]):
        acc = lax.dot_general(a_ref[...], wt_ref[cols, :], NT_DIMS, preferred_element_type=F32)
        o_ref[:, cols] = (acc * scale_ref[:, cols]).astype(BF16)


def _inproj(h, g, w_t, w_small_t, col_scale, *, bm, bn, n1, rest_row):
    S, D = h.shape
    n_cols = col_scale.shape[1]

    def w_row(i, n):
        return pl.multiple_of(jnp.where(n < n1, n * bn, rest_row + (n - n1) * bn), FOX_HEADS)

    return pl.pallas_call(
        _inproj_kernel,
        name="inproj",
        grid=(S // bm, n_cols // bn),
        in_specs=[
            pl.BlockSpec((bm, D), lambda i, n: (i, 0), pipeline_mode=pl.Buffered(1)),
            pl.BlockSpec((1, D), lambda i, n: (0, 0)),
            pl.BlockSpec((pl.Element(bn), pl.Element(D)), lambda i, n: (w_row(i, n), 0)),
            pl.BlockSpec((LANES, D), lambda i, n: (0, 0)),
            pl.BlockSpec((1, bn), lambda i, n: (0, n)),
        ],
        out_specs=[
            pl.BlockSpec((bm, bn), lambda i, n: (i, n)),
            pl.BlockSpec((bm, LANES), lambda i, n: (i, 0)),
        ],
        out_shape=[
            jax.ShapeDtypeStruct((S, n_cols), BF16),
            jax.ShapeDtypeStruct((S, LANES), F32),
        ],
        scratch_shapes=[pltpu.VMEM((bm, D), BF16)],
        compiler_params=_params("parallel", "arbitrary"),
    )(h, g, w_t, w_small_t, col_scale)


def _fox_gate_kernel(small_ref, bias_ref, c_ref, carry_ref):
    t = pl.program_id(0)
    T = small_ref.shape[0]

    @pl.when(t == 0)
    def _init():
        carry_ref[...] = jnp.zeros_like(carry_ref)

    log_f = _log_sigmoid(small_ref[...] + bias_ref[...])
    row = lax.broadcasted_iota(jnp.int32, (T, T), 0)
    col = lax.broadcasted_iota(jnp.int32, (T, T), 1)
    tri = jnp.where(row >= col, 1.0, 0.0).astype(F32)
    c = jnp.dot(tri, log_f, precision=HIGHEST, preferred_element_type=F32) + carry_ref[...]
    c_ref[...] = c * LOG2_E
    carry_ref[...] = c[T - 1:T, :]


def _fox_gate(small, bias, *, bt):
    S = small.shape[0]
    return pl.pallas_call(
        _fox_gate_kernel,
        name="fox_gate",
        grid=(S // bt,),
        in_specs=[
            pl.BlockSpec((bt, LANES), lambda t: (t, 0)),
            pl.BlockSpec((1, LANES), lambda t: (0, 0)),
        ],
        out_specs=pl.BlockSpec((bt, LANES), lambda t: (t, 0)),
        out_shape=jax.ShapeDtypeStruct((S, LANES), F32),
        scratch_shapes=[pltpu.VMEM((1, LANES), F32)],
        compiler_params=_params("arbitrary"),
    )(small, bias)


FOX_PAIR = 2
FOX_ROWS = 16


def _split3(c):
    hi = c.astype(BF16).astype(F32)
    mid = (c - hi).astype(BF16).astype(F32)
    lo = (c - hi - mid).astype(BF16).astype(F32)
    return hi, mid, lo


def _lane_tile(x, n):
    return jnp.concatenate([x] * n, axis=1)


def _head_column(block, head):
    lane = lax.broadcasted_iota(jnp.int32, block.shape, 1)
    return jnp.sum(jnp.where(lane == head, block, 0.0), axis=1, keepdims=True)


def _fox_kernel(q_ref, k_ref, v_ref, c_ref, o_ref, kaug_ref, vaug_ref, qaug_ref, s_ref, p_ref,
                m_ref, alpha_ref, acc_ref, *, bk):
    hp = pl.program_id(0)
    i = pl.program_id(1)
    bq = q_ref.shape[0]
    S = k_ref.shape[0]
    dh = FOX_HEAD_DIM
    R = FOX_ROWS

    @pl.when(i == 0)
    def _build_kv_side():
        rows_per = 128

        def body(r, carry):
            rows = pl.ds(pl.multiple_of(r * rows_per, rows_per), rows_per)
            cblk = c_ref[rows, :]
            lane = lax.broadcasted_iota(jnp.int32, (rows_per, LANES), 1)
            for hh in range(FOX_PAIR):
                hi, mid, lo = _split3(_head_column(cblk, hp * FOX_PAIR + hh))
                extra = jnp.where(lane < 3, 1.0,
                                  jnp.where(lane == 3, -hi,
                                            jnp.where(lane == 4, -mid,
                                                      jnp.where(lane == 5, -lo, 0.0))))
                kaug_ref[hh, rows, 0:dh] = k_ref[rows, hh * dh:(hh + 1) * dh]
                kaug_ref[hh, rows, dh:2 * dh] = extra.astype(BF16)
                vaug_ref[hh, rows, 0:dh] = v_ref[rows, hh * dh:(hh + 1) * dh]
                vaug_ref[hh, rows, dh:2 * dh] = jnp.ones((rows_per, dh), BF16)
            return carry
        lax.fori_loop(0, S // rows_per, body, 0, unroll=4)

    cq_blk = c_ref[pl.ds(pl.multiple_of(i * bq, bq), bq), :]
    lane_q = lax.broadcasted_iota(jnp.int32, (bq, LANES), 1)
    for hh in range(FOX_PAIR):
        hi, mid, lo = _split3(_head_column(cq_blk, hp * FOX_PAIR + hh))
        extra = jnp.where(lane_q == 0, hi,
                          jnp.where(lane_q == 1, mid,
                                    jnp.where(lane_q == 2, lo,
                                              jnp.where(lane_q < 6, 1.0, 0.0))))
        qaug_ref[hh, :, 0:dh] = q_ref[:, hh * dh:(hh + 1) * dh]
        qaug_ref[hh, :, dh:2 * dh] = extra.astype(BF16)
        m_ref[hh] = jnp.full((bq, LANES), -jnp.inf, F32)
        acc_ref[hh] = jnp.zeros((bq, 2 * dh), F32)

    groups = [slice(g * R, (g + 1) * R) for g in range(bq // R)]

    def chunk(k0, width, masked):
        ks = pl.ds(pl.multiple_of(k0, bq), width)
        for hh in range(FOX_PAIR):
            s_ref[hh, :, 0:width] = lax.dot_general(
                qaug_ref[hh], kaug_ref[hh, ks, :], (((1,), (1,)), ((), ())),
                preferred_element_type=F32)
        if masked:
            diff = (lax.broadcasted_iota(jnp.int32, (R, width), 1)
                    - lax.broadcasted_iota(jnp.int32, (R, width), 0))
            limit = i * bq - k0
        for hh in range(FOX_PAIR):
            for g, rows in enumerate(groups):
                s = s_ref[hh, rows, 0:width]
                if masked:
                    s = jnp.where(diff <= limit + g * R, s, -jnp.inf)
                    s_ref[hh, rows, 0:width] = s
                m_old = m_ref[hh, rows, :]
                m_new = jnp.maximum(m_old, jnp.broadcast_to(
                    jnp.max(s, axis=1, keepdims=True), (R, LANES)))
                alpha_ref[hh, rows, :] = jnp.exp2(m_old - m_new)
                m_ref[hh, rows, :] = m_new
        for hh in range(FOX_PAIR):
            for rows in groups:
                m = _lane_tile(m_ref[hh, rows, :], width // LANES)
                p_ref[hh, rows, 0:width] = jnp.exp2(s_ref[hh, rows, 0:width] - m).astype(BF16)
        for hh in range(FOX_PAIR):
            alpha = _lane_tile(alpha_ref[hh], 2)
            acc_ref[hh] = alpha * acc_ref[hh] + jnp.dot(
                p_ref[hh, :, 0:width], vaug_ref[hh, ks, :], preferred_element_type=F32)

    n_full = (i * bq) // bk

    def pair_body(j, carry):
        chunk(2 * j * bk, bk, False)
        chunk((2 * j + 1) * bk, bk, False)
        return carry
    lax.fori_loop(0, n_full // 2, pair_body, 0)

    @pl.when(n_full % 2 == 1)
    def _odd():
        chunk((n_full - 1) * bk, bk, False)

    tail_start = n_full * bk
    tail_blocks = (i + 1) - n_full * (bk // bq)
    for t in range(1, bk // bq + 1):
        @pl.when(tail_blocks == t)
        def _tail(t=t):
            chunk(tail_start, t * bq, True)

    for hh in range(FOX_PAIR):
        acc = acc_ref[hh]
        o_ref[:, hh * dh:(hh + 1) * dh] = (acc[:, 0:dh] / acc[:, dh:2 * dh]).astype(o_ref.dtype)


def _fox(proj, c2, *, bq, bk):
    S = proj.shape[0]
    dh = FOX_HEAD_DIM
    w = FOX_PAIR * dh
    return pl.pallas_call(
        functools.partial(_fox_kernel, bk=bk),
        name="fox",
        grid=(FOX_HEADS // FOX_PAIR, S // bq),
        in_specs=[
            pl.BlockSpec((bq, w), lambda h, i: (i, OFF_FQ // w + h)),
            pl.BlockSpec((S, w), lambda h, i: (0, OFF_FK // w + h)),
            pl.BlockSpec((S, w), lambda h, i: (0, OFF_FV // w + h)),
            pl.BlockSpec((S, LANES), lambda h, i: (0, 0)),
        ],
        out_specs=pl.BlockSpec((bq, w), lambda h, i: (i, h)),
        out_shape=jax.ShapeDtypeStruct((S, FOX_WIDTH), BF16),
        scratch_shapes=[
            pltpu.VMEM((FOX_PAIR, S, 2 * dh), BF16),
            pltpu.VMEM((FOX_PAIR, S, 2 * dh), BF16),
            pltpu.VMEM((FOX_PAIR, bq, 2 * dh), BF16),
            pltpu.VMEM((FOX_PAIR, bq, bk), F32),
            pltpu.VMEM((FOX_PAIR, bq, bk), BF16),
            pltpu.VMEM((FOX_PAIR, bq, LANES), F32),
            pltpu.VMEM((FOX_PAIR, bq, LANES), F32),
            pltpu.VMEM((FOX_PAIR, bq, 2 * dh), F32),
        ],
        compiler_params=_params("parallel", "arbitrary"),
    )(proj, proj, proj, c2)


def _gla_kernel(q_ref, k_ref, v_ref, gr_ref, small_ref, wgate_ref, bgate_ref,
                gnorm_ref, tri_ref, o_ref, st_ref):
    t = pl.program_id(1)
    T = q_ref.shape[0]
    C = GLA_CHUNK

    @pl.when(t == 0)
    def _init():
        st_ref[...] = jnp.zeros_like(st_ref)

    dk = q_ref.shape[1]
    gate = jnp.dot(small_ref[...].astype(BF16), wgate_ref[...],
                   preferred_element_type=F32) + bgate_ref[...]
    log_a = _log_sigmoid(gate) * (1.0 / GLA_TAU)

    pieces = jnp.concatenate([x.astype(BF16) for x in _split3(log_a)], axis=1)
    sums = jnp.dot(tri_ref[...].astype(BF16), pieces, preferred_element_type=F32)
    b = sums[:, 0:dk] + sums[:, dk:2 * dk] + sums[:, 2 * dk:3 * dk]
    b_tot = jnp.concatenate(
        [jnp.broadcast_to(b[c * C + C - 1:c * C + C, :], (C, dk)) for c in range(T // C)], axis=0)
    causal = tri_ref[...] > 0.0

    q = q_ref[...].astype(F32)
    k = k_ref[...].astype(F32)
    v = v_ref[...]
    q_dec = (q * jnp.exp(b)).astype(BF16)
    k_inv = (k * jnp.exp(-b)).astype(BF16)
    k_end = (k * jnp.exp(b_tot - b)).astype(BF16)
    a = lax.dot_general(q_dec, k_inv, (((1,), (1,)), ((), ())), preferred_element_type=F32)
    a = jnp.where(causal, a, 0.0).astype(BF16)
    o_intra = jnp.dot(a, v, preferred_element_type=F32)

    for ci in range(T // C):
        lo, hi = ci * C, (ci + 1) * C
        st = st_ref[...]
        o_c = o_intra[lo:hi, :] + lax.dot_general(
            q_dec[lo:hi, :], st.astype(BF16), (((1,), (1,)), ((), ())),
            preferred_element_type=F32)
        decay = jnp.exp(b_tot[lo:lo + 1, :])
        st_ref[...] = st * decay + lax.dot_general(
            v[lo:hi, :], k_end[lo:hi, :], (((0,), (0,)), ((), ())),
            preferred_element_type=F32)
        y = _rms(o_c, gnorm_ref[...])
        o_ref[lo:hi, :] = (y * _silu(gr_ref[lo:hi, :].astype(F32))).astype(o_ref.dtype)


def _chunk_causal_mask(n):
    r = jnp.arange(n)[:, None]
    c = jnp.arange(n)[None, :]
    return ((r // GLA_CHUNK == c // GLA_CHUNK) & (r >= c)).astype(F32)


def _gla(proj, small, wgate_pad, bgate, gnorm, *, bt):
    S = proj.shape[0]
    dk, dv = GLA_HEAD_K, GLA_HEAD_V
    return pl.pallas_call(
        _gla_kernel,
        name="gla",
        grid=(GLA_HEADS, S // bt),
        in_specs=[
            pl.BlockSpec((bt, dk), lambda h, t: (t, OFF_GQ // dk + h)),
            pl.BlockSpec((bt, dk), lambda h, t: (t, OFF_GK // dk + h)),
            pl.BlockSpec((bt, dv), lambda h, t: (t, OFF_GV // dv + h)),
            pl.BlockSpec((bt, dv), lambda h, t: (t, OFF_GR // dv + h)),
            pl.BlockSpec((bt, LANES), lambda h, t: (t, 0)),
            pl.BlockSpec((LANES, dk), lambda h, t: (0, h)),
            pl.BlockSpec((1, dk), lambda h, t: (0, h)),
            pl.BlockSpec((1, dv), lambda h, t: (0, 0)),
            pl.BlockSpec((bt, bt), lambda h, t: (0, 0)),
        ],
        out_specs=pl.BlockSpec((bt, dv), lambda h, t: (t, h)),
        out_shape=jax.ShapeDtypeStruct((S, GLA_WIDTH), BF16),
        scratch_shapes=[pltpu.VMEM((dv, dk), F32)],
        compiler_params=_params("parallel", "arbitrary"),
    )(proj, proj, proj, proj, small, wgate_pad, bgate, gnorm, _chunk_causal_mask(bt))


def _outproj_kernel(fox_ref, gla_ref, wt_ref, wb_ref, h_ref, g_ref, o_ref):
    n = pl.program_id(1)
    bn = wt_ref.shape[1]
    for sub in _sub_tiles(bn):
        cols = pl.ds(pl.multiple_of(n * bn + sub.start, MXU_COLS), MXU_COLS)
        o_ref[:, cols] = (jnp.dot(fox_ref[...], wt_ref[:, sub], preferred_element_type=F32)
                          + jnp.dot(gla_ref[...], wb_ref[:, sub], preferred_element_type=F32))

    @pl.when(n == pl.num_programs(1) - 1)
    def _epilogue():
        def emit(rows, y):
            o_ref[rows, :] = h_ref[rows, :] + y
        _rms_rows(o_ref, g_ref, emit)


def _outproj(o_fox, o_gla, w_o, h, g, *, bm, bn):
    S, D = h.shape
    half = o_fox.shape[1]
    return pl.pallas_call(
        _outproj_kernel,
        name="outproj",
        grid=(S // bm, D // bn),
        in_specs=[
            pl.BlockSpec((bm, half), lambda i, n: (i, 0)),
            pl.BlockSpec((bm, half), lambda i, n: (i, 0)),
            pl.BlockSpec((half, bn), lambda i, n: (0, n)),
            pl.BlockSpec((half, bn), lambda i, n: (1, n)),
            pl.BlockSpec((bm, D), lambda i, n: (i, 0)),
            pl.BlockSpec((1, D), lambda i, n: (0, 0)),
        ],
        out_specs=pl.BlockSpec((bm, D), lambda i, n: (i, 0)),
        out_shape=jax.ShapeDtypeStruct((S, D), F32),
        compiler_params=_params("parallel", "arbitrary"),
    )(o_fox, o_gla, w_o, w_o, h, g)


def _ple_kernel(h_ref, p_ref, wp_ref, g_ref, wg_ref, o_ref, hb_ref, e_ref):
    n = pl.program_id(1)
    bn = wg_ref.shape[1]

    @pl.when(n == 0)
    def _prologue():
        e_ref[...] = jnp.dot(p_ref[...].astype(BF16), wp_ref[...], preferred_element_type=F32)

        def emit(rows, y):
            hb_ref[rows, :] = h_ref[rows, :].astype(BF16)
            e_ref[rows, :] = y
        _rms_rows(e_ref, g_ref, emit)

    for sub in _sub_tiles(bn):
        cols = pl.ds(pl.multiple_of(n * bn + sub.start, MXU_COLS), MXU_COLS)
        gate = jax.nn.sigmoid(jnp.dot(hb_ref[...], wg_ref[:, sub], preferred_element_type=F32))
        o_ref[:, sub] = h_ref[:, cols] + e_ref[:, cols] * gate


def _ple(h, p, w_proj, g, w_gate, *, bm, bn):
    S, D = h.shape
    dp = p.shape[1]
    return pl.pallas_call(
        _ple_kernel,
        name="ple",
        grid=(S // bm, D // bn),
        in_specs=[
            pl.BlockSpec((bm, D), lambda i, n: (i, 0)),
            pl.BlockSpec((bm, dp), lambda i, n: (i, 0)),
            pl.BlockSpec((dp, D), lambda i, n: (0, 0)),
            pl.BlockSpec((1, D), lambda i, n: (0, 0)),
            pl.BlockSpec((D, bn), lambda i, n: (0, n)),
        ],
        out_specs=pl.BlockSpec((bm, bn), lambda i, n: (i, n)),
        out_shape=jax.ShapeDtypeStruct((S, D), F32),
        scratch_shapes=[pltpu.VMEM((bm, D), BF16), pltpu.VMEM((bm, D), F32)],
        compiler_params=_params("parallel", "arbitrary"),
    )(h, p, w_proj, g, w_gate)


def _tile(n, preferred):
    t = min(n, preferred)
    while n % t:
        t //= 2
    return t


def _split_w_in(w_in):
    D = w_in.shape[0]
    ff_lo = 3 * FOX_WIDTH
    ff_hi = ff_lo + FOX_HEADS
    glr_lo = ff_hi + 2 * GLA_KEY_WIDTH + 2 * GLA_WIDTH
    glr_hi = glr_lo + GLA_GATE_RANK
    w_t = w_in.T.astype(BF16)
    pad = jnp.zeros((LANES - FOX_HEADS - GLA_GATE_RANK, D), BF16)
    small_t = jnp.concatenate([w_t[ff_lo:ff_hi], w_t[glr_lo:glr_hi], pad], axis=0)
    return w_t, small_t, ff_hi


def kernel(x, p, ffn1_norm_pre, ffn1_w_gate, ffn1_w_up, ffn1_w_down, ffn1_norm_post, mix_norm_pre, w_in, fox_b_f, gla_w_gate, gla_b_gate, gla_norm_g, w_o, mix_norm_post, ffn2_norm_pre, ffn2_w_gate, ffn2_w_up, ffn2_w_down, ffn2_norm_post, ple_w_proj, ple_norm, ple_w_gate):
    B, S, D = x.shape
    assert B == 1, "the attention kernels treat the row axis as one sequence"
    depth = w_in.shape[0]
    h = x.reshape(S, D)

    bm = _tile(S, 512)
    bf = _tile(ffn1_w_gate.shape[-1], 256)
    bq = _tile(S, 512)
    bk = max(bq, _tile(S, 1024))
    bt = _tile(S, 512)

    col_scale = jnp.ones((1, MAIN_COLS), F32)
    col_scale = col_scale.at[:, OFF_FQ:OFF_FQ + FOX_WIDTH].set(FOX_HEAD_DIM ** -0.5 * LOG2_E)
    col_scale = col_scale.at[:, OFF_GQ:OFF_GQ + GLA_KEY_WIDTH].set(GLA_HEAD_K ** -0.5)

    def row(v):
        return v.reshape(1, -1).astype(F32)

    for i in range(depth):
        h = _ffn(h, row(ffn1_norm_pre[i]), ffn1_w_gate[i], ffn1_w_up[i], ffn1_w_down[i],
                 row(ffn1_norm_post[i]), bm=bm, bf=bf)

        w_t, w_small_t, rest_row = _split_w_in(w_in[i])
        bn = 1024
        proj, small = _inproj(h, row(mix_norm_pre[i]), w_t, w_small_t, col_scale, bm=_tile(S, 1024),
                              bn=bn, n1=3 * FOX_WIDTH // bn, rest_row=rest_row)

        bias_f = jnp.zeros((1, LANES), F32).at[0, :FOX_HEADS].set(fox_b_f[i])
        c2 = _fox_gate(small, bias_f, bt=bt)
        o_fox = _fox(proj, c2, bq=bq, bk=bk)

        wgate_pad = jnp.zeros((LANES, GLA_KEY_WIDTH), F32).at[
            FOX_HEADS:FOX_HEADS + GLA_GATE_RANK, :].set(gla_w_gate[i]).astype(BF16)
        o_gla = _gla(proj, small, wgate_pad, row(gla_b_gate[i]), row(gla_norm_g[i]), bt=bt)

        h = _outproj(o_fox, o_gla, w_o[i].astype(BF16), h, row(mix_norm_post[i]), bm=bm, bn=512)

        h = _ffn(h, row(ffn2_norm_pre[i]), ffn2_w_gate[i], ffn2_w_up[i], ffn2_w_down[i],
                 row(ffn2_norm_post[i]), bm=bm, bf=bf)

        h = _ple(h, p[i].reshape(S, -1), ple_w_proj[i].astype(BF16), row(ple_norm[i]),
                 ple_w_gate[i].astype(BF16), bm=bm, bn=1024)
    return h.reshape(B, S, D)
```

```python
import functools

import jax
import jax.numpy as jnp
from jax import lax
from jax.experimental import pallas as pl
from jax.experimental.pallas import tpu as pltpu

F32 = jnp.float32
BF16 = jnp.bfloat16
HIGHEST = lax.Precision.HIGHEST

EPS = 1e-6
LOG2_E = 1.4426950408889634
MACARON_WEIGHT = 0.5
FOX_HEAD_DIM = 128
FOX_HEADS = 16
FOX_WIDTH = FOX_HEADS * FOX_HEAD_DIM
GLA_HEADS = 4
GLA_WIDTH = 2048
GLA_HEAD_V = GLA_WIDTH // GLA_HEADS
GLA_KEY_WIDTH = GLA_WIDTH // 2
GLA_HEAD_K = GLA_KEY_WIDTH // GLA_HEADS
GLA_GATE_RANK = 16
GLA_TAU = 16.0
GLA_CHUNK = 64

LANES = 128
MXU_COLS = 256
VMEM_LIMIT_BYTES = 56 * 1024 * 1024

MAIN_COLS = 3 * FOX_WIDTH + 2 * GLA_WIDTH + 2 * GLA_KEY_WIDTH
OFF_FQ = 0
OFF_FK = FOX_WIDTH
OFF_FV = 2 * FOX_WIDTH
OFF_GQ = 3 * FOX_WIDTH
OFF_GK = OFF_GQ + GLA_KEY_WIDTH
OFF_GV = OFF_GK + GLA_KEY_WIDTH
OFF_GR = OFF_GV + GLA_WIDTH

ROW_CHUNK = 32
ROW_BLOCK = 128


def _params(*semantics):
    return pltpu.CompilerParams(dimension_semantics=semantics,
                                vmem_limit_bytes=VMEM_LIMIT_BYTES)


def _rms(x, g):
    ms = jnp.mean(x * x, axis=-1, keepdims=True)
    return x * lax.rsqrt(ms + EPS) * g


def _log_sigmoid(x):
    return jnp.minimum(x, 0.0) - jnp.log(1.0 + jnp.exp(-jnp.abs(x)))


def _silu(x):
    return x * jax.nn.sigmoid(x)


def _sub_tiles(n_cols):
    return [slice(c, c + MXU_COLS) for c in range(0, n_cols, MXU_COLS)]


def _rms_rows(src_ref, g_ref, emit):
    def body(b, carry):
        chunks = [pl.ds(pl.multiple_of(b * ROW_BLOCK + c * ROW_CHUNK, ROW_CHUNK), ROW_CHUNK)
                  for c in range(ROW_BLOCK // ROW_CHUNK)]
        scales = []
        for rows in chunks:
            x = src_ref[rows, :]
            scales.append(lax.rsqrt(jnp.mean(x * x, axis=-1, keepdims=True) + EPS))
        for rows, scale in zip(chunks, scales):
            emit(rows, src_ref[rows, :] * scale * g_ref[...])
        return carry
    lax.fori_loop(0, src_ref.shape[0] // ROW_BLOCK, body, 0)


def _ffn_prologue(x_ref, gpre_ref, xn_ref, o_ref):
    def emit(rows, y):
        xn_ref[rows, :] = y.astype(BF16)
        o_ref[rows, :] = jnp.zeros((ROW_CHUNK, o_ref.shape[1]), F32)
    _rms_rows(x_ref, gpre_ref, emit)


def _ffn_epilogue(x_ref, gpost_ref, o_ref):
    def emit(rows, y):
        o_ref[rows, :] = x_ref[rows, :] + MACARON_WEIGHT * y
    _rms_rows(o_ref, gpost_ref, emit)


FFN_COPY_STEPS = 4


def _ffn_kernel(x_ref, gpre_ref, wg_ref, wu_ref, wd_ref, gpost_ref, first_ref, o_ref, xn_ref):
    i = pl.program_id(0)
    f = pl.program_id(1)
    rows_c = first_ref.shape[0]

    @pl.when(jnp.logical_and(i == 0, f < FFN_COPY_STEPS))
    def _():
        o_ref[pl.ds(pl.multiple_of(f * rows_c, rows_c), rows_c), :] = first_ref[...]

    @pl.when(i > 0)
    def _():
        @pl.when(f == 0)
        def _():
            _ffn_prologue(x_ref, gpre_ref, xn_ref, o_ref)

        xn = xn_ref[...]
        g = jnp.dot(xn, wg_ref[...], preferred_element_type=F32)
        u = jnp.dot(xn, wu_ref[...], preferred_element_type=F32)
        hid = (_silu(g) * u).astype(BF16)
        o_ref[...] += jnp.dot(hid, wd_ref[...], preferred_element_type=F32)

        @pl.when(f == pl.num_programs(1) - 1)
        def _():
            _ffn_epilogue(x_ref, gpost_ref, o_ref)


def _ffn_first_kernel(x_ref, gpre_ref, wg_ref, wu_ref, wd_ref, gpost_ref,
                      o_ref, wg16_ref, wu16_ref, wd16_ref, xn_ref, g_ref, u_ref):
    f = pl.program_id(0)
    k = pl.program_id(1)
    dk = wg_ref.shape[0]

    @pl.when(jnp.logical_and(f == 0, k == 0))
    def _():
        _ffn_prologue(x_ref, gpre_ref, xn_ref, o_ref)

    wg = wg_ref[...].astype(BF16)
    wu = wu_ref[...].astype(BF16)
    wg16_ref[...] = wg
    wu16_ref[...] = wu
    xk = xn_ref[:, pl.ds(pl.multiple_of(k * dk, dk), dk)]
    g = jnp.dot(xk, wg, preferred_element_type=F32)
    u = jnp.dot(xk, wu, preferred_element_type=F32)

    @pl.when(k == 0)
    def _():
        g_ref[...] = g
        u_ref[...] = u

    @pl.when(k == 1)
    def _():
        wd = wd_ref[...].astype(BF16)
        wd16_ref[...] = wd
        hid = (_silu(g_ref[...] + g) * (u_ref[...] + u)).astype(BF16)
        o_ref[...] += jnp.dot(hid, wd, preferred_element_type=F32)

    @pl.when(jnp.logical_and(f == pl.num_programs(0) - 1, k == 1))
    def _():
        _ffn_epilogue(x_ref, gpost_ref, o_ref)


def _ffn(x, g_pre, wg, wu, wd, g_post, *, bm, bf):
    S, D = x.shape
    d_ff = wg.shape[1]
    once = dict(pipeline_mode=pl.Buffered(1))
    h, wg16, wu16, wd16 = pl.pallas_call(
        _ffn_first_kernel,
        name="ffn_first",
        grid=(d_ff // bf, 2),
        in_specs=[
            pl.BlockSpec((bm, D), lambda f, k: (0, 0), **once),
            pl.BlockSpec((1, D), lambda f, k: (0, 0)),
            pl.BlockSpec((D // 2, bf), lambda f, k: (k, f)),
            pl.BlockSpec((D // 2, bf), lambda f, k: (k, f)),
            pl.BlockSpec((bf, D), lambda f, k: (f, 0)),
            pl.BlockSpec((1, D), lambda f, k: (0, 0)),
        ],
        out_specs=[
            pl.BlockSpec((bm, D), lambda f, k: (0, 0)),
            pl.BlockSpec((D // 2, bf), lambda f, k: (k, f)),
            pl.BlockSpec((D // 2, bf), lambda f, k: (k, f)),
            pl.BlockSpec((bf, D), lambda f, k: (f, 0)),
        ],
        out_shape=[
            jax.ShapeDtypeStruct((bm, D), F32),
            jax.ShapeDtypeStruct((D, d_ff), BF16),
            jax.ShapeDtypeStruct((D, d_ff), BF16),
            jax.ShapeDtypeStruct((d_ff, D), BF16),
        ],
        scratch_shapes=[pltpu.VMEM((bm, D), BF16), pltpu.VMEM((bm, bf), F32), pltpu.VMEM((bm, bf), F32)],
        compiler_params=_params("arbitrary", "arbitrary"),
    )(x, g_pre, wg, wu, wd, g_post)
    if S == bm:
        return h
    nf = d_ff // bf
    c = FFN_COPY_STEPS
    assert nf >= c

    def ff_tile(i, f):
        return jnp.where(i == 0, 0, f)

    return pl.pallas_call(
        _ffn_kernel,
        name="ffn",
        grid=(S // bm, nf),
        in_specs=[
            pl.BlockSpec((bm, D), lambda i, f: (i, 0)),
            pl.BlockSpec((1, D), lambda i, f: (0, 0)),
            pl.BlockSpec((D, bf), lambda i, f: (0, ff_tile(i, f))),
            pl.BlockSpec((D, bf), lambda i, f: (0, ff_tile(i, f))),
            pl.BlockSpec((bf, D), lambda i, f: (ff_tile(i, f), 0)),
            pl.BlockSpec((1, D), lambda i, f: (0, 0)),
            pl.BlockSpec((bm // c, D), lambda i, f: (jnp.where(i == 0, jnp.minimum(f, c - 1), c - 1), 0)),
        ],
        out_specs=pl.BlockSpec((bm, D), lambda i, f: (i, 0)),
        out_shape=jax.ShapeDtypeStruct((S, D), F32),
        scratch_shapes=[pltpu.VMEM((bm, D), BF16)],
        compiler_params=_params("parallel", "arbitrary"),
    )(x, g_pre, wg16, wu16, wd16, g_post, h)


NT_DIMS = (((1,), (1,)), ((), ()))


def _inproj_kernel(h_ref, g_ref, wt_ref, wst_ref, scale_ref, o_ref, os_ref, a_ref):
    n = pl.program_id(1)

    @pl.when(n == 0)
    def _prologue():
        def emit(rows, y):
            a_ref[rows, :] = y.astype(BF16)
        _rms_rows(h_ref, g_ref, emit)
        os_ref[...] = lax.dot_general(a_ref[...], wst_ref[...], NT_DIMS, preferred_element_type=F32)

    for cols in _sub_tiles(wt_ref.shape[0]):
        acc = lax.dot_general(a_ref[...], wt_ref[cols, :], NT_DIMS, preferred_element_type=F32)
        o_ref[:, cols] = (acc * scale_ref[:, cols]).astype(BF16)


def _inproj(h, g, w_t, w_small_t, col_scale, *, bm, bn, n1, rest_row):
    S, D = h.shape
    n_cols = col_scale.shape[1]

    def w_row(i, n):
        return pl.multiple_of(jnp.where(n < n1, n * bn, rest_row + (n - n1) * bn), FOX_HEADS)

    return pl.pallas_call(
        _inproj_kernel,
        name="inproj",
        grid=(S // bm, n_cols // bn),
        in_specs=[
            pl.BlockSpec((bm, D), lambda i, n: (i, 0), pipeline_mode=pl.Buffered(1)),
            pl.BlockSpec((1, D), lambda i, n: (0, 0)),
            pl.BlockSpec((pl.Element(bn), pl.Element(D)), lambda i, n: (w_row(i, n), 0)),
            pl.BlockSpec((LANES, D), lambda i, n: (0, 0)),
            pl.BlockSpec((1, bn), lambda i, n: (0, n)),
        ],
        out_specs=[
            pl.BlockSpec((bm, bn), lambda i, n: (i, n)),
            pl.BlockSpec((bm, LANES), lambda i, n: (i, 0)),
        ],
        out_shape=[
            jax.ShapeDtypeStruct((S, n_cols), BF16),
            jax.ShapeDtypeStruct((S, LANES), F32),
        ],
        scratch_shapes=[pltpu.VMEM((bm, D), BF16)],
        compiler_params=_params("parallel", "arbitrary"),
    )(h, g, w_t, w_small_t, col_scale)


def _fox_gate_kernel(small_ref, bias_ref, c_ref, carry_ref):
    t = pl.program_id(0)
    T = small_ref.shape[0]

    @pl.when(t == 0)
    def _init():
        carry_ref[...] = jnp.zeros_like(carry_ref)

    log_f = _log_sigmoid(small_ref[...] + bias_ref[...])
    row = lax.broadcasted_iota(jnp.int32, (T, T), 0)
    col = lax.broadcasted_iota(jnp.int32, (T, T), 1)
    tri = jnp.where(row >= col, 1.0, 0.0).astype(F32)
    c = jnp.dot(tri, log_f, precision=HIGHEST, preferred_element_type=F32) + carry_ref[...]
    c_ref[...] = c * LOG2_E
    carry_ref[...] = c[T - 1:T, :]


def _fox_gate(small, bias, *, bt):
    S = small.shape[0]
    return pl.pallas_call(
        _fox_gate_kernel,
        name="fox_gate",
        grid=(S // bt,),
        in_specs=[
            pl.BlockSpec((bt, LANES), lambda t: (t, 0)),
            pl.BlockSpec((1, LANES), lambda t: (0, 0)),
        ],
        out_specs=pl.BlockSpec((bt, LANES), lambda t: (t, 0)),
        out_shape=jax.ShapeDtypeStruct((S, LANES), F32),
        scratch_shapes=[pltpu.VMEM((1, LANES), F32)],
        compiler_params=_params("arbitrary"),
    )(small, bias)


FOX_PAIR = 2
FOX_ROWS = 16


def _split3(c):
    hi = c.astype(BF16).astype(F32)
    mid = (c - hi).astype(BF16).astype(F32)
    lo = (c - hi - mid).astype(BF16).astype(F32)
    return hi, mid, lo


def _lane_tile(x, n):
    return jnp.concatenate([x] * n, axis=1)


def _head_column(block, head):
    lane = lax.broadcasted_iota(jnp.int32, block.shape, 1)
    return jnp.sum(jnp.where(lane == head, block, 0.0), axis=1, keepdims=True)


def _fox_kernel(q_ref, k_ref, v_ref, c_ref, o_ref, kaug_ref, vaug_ref, qaug_ref, s_ref, p_ref,
                m_ref, alpha_ref, acc_ref, *, bk):
    hp = pl.program_id(0)
    i = pl.program_id(1)
    bq = q_ref.shape[0]
    S = k_ref.shape[0]
    dh = FOX_HEAD_DIM
    R = FOX_ROWS

    @pl.when(i == 0)
    def _build_kv_side():
        rows_per = 128

        def body(r, carry):
            rows = pl.ds(pl.multiple_of(r * rows_per, rows_per), rows_per)
            cblk = c_ref[rows, :]
            lane = lax.broadcasted_iota(jnp.int32, (rows_per, LANES), 1)
            for hh in range(FOX_PAIR):
                hi, mid, lo = _split3(_head_column(cblk, hp * FOX_PAIR + hh))
                extra = jnp.where(lane < 3, 1.0,
                                  jnp.where(lane == 3, -hi,
                                            jnp.where(lane == 4, -mid,
                                                      jnp.where(lane == 5, -lo, 0.0))))
                kaug_ref[hh, rows, 0:dh] = k_ref[rows, hh * dh:(hh + 1) * dh]
                kaug_ref[hh, rows, dh:2 * dh] = extra.astype(BF16)
                vaug_ref[hh, rows, 0:dh] = v_ref[rows, hh * dh:(hh + 1) * dh]
                vaug_ref[hh, rows, dh:2 * dh] = jnp.ones((rows_per, dh), BF16)
            return carry
        lax.fori_loop(0, S // rows_per, body, 0, unroll=4)

    cq_blk = c_ref[pl.ds(pl.multiple_of(i * bq, bq), bq), :]
    lane_q = lax.broadcasted_iota(jnp.int32, (bq, LANES), 1)
    for hh in range(FOX_PAIR):
        hi, mid, lo = _split3(_head_column(cq_blk, hp * FOX_PAIR + hh))
        extra = jnp.where(lane_q == 0, hi,
                          jnp.where(lane_q == 1, mid,
                                    jnp.where(lane_q == 2, lo,
                                              jnp.where(lane_q < 6, 1.0, 0.0))))
        qaug_ref[hh, :, 0:dh] = q_ref[:, hh * dh:(hh + 1) * dh]
        qaug_ref[hh, :, dh:2 * dh] = extra.astype(BF16)
        m_ref[hh] = jnp.full((bq, LANES), -jnp.inf, F32)
        acc_ref[hh] = jnp.zeros((bq, 2 * dh), F32)

    groups = [slice(g * R, (g + 1) * R) for g in range(bq // R)]

    def chunk(k0, width, masked):
        ks = pl.ds(pl.multiple_of(k0, bq), width)
        for hh in range(FOX_PAIR):
            s_ref[hh, :, 0:width] = lax.dot_general(
                qaug_ref[hh], kaug_ref[hh, ks, :], (((1,), (1,)), ((), ())),
                preferred_element_type=F32)
        if masked:
            diff = (lax.broadcasted_iota(jnp.int32, (R, width), 1)
                    - lax.broadcasted_iota(jnp.int32, (R, width), 0))
            limit = i * bq - k0
        for hh in range(FOX_PAIR):
            for g, rows in enumerate(groups):
                s = s_ref[hh, rows, 0:width]
                if masked:
                    s = jnp.where(diff <= limit + g * R, s, -jnp.inf)
                    s_ref[hh, rows, 0:width] = s
                m_old = m_ref[hh, rows, :]
                m_new = jnp.maximum(m_old, jnp.broadcast_to(
                    jnp.max(s, axis=1, keepdims=True), (R, LANES)))
                alpha_ref[hh, rows, :] = jnp.exp2(m_old - m_new)
                m_ref[hh, rows, :] = m_new
        for hh in range(FOX_PAIR):
            for rows in groups:
                m = _lane_tile(m_ref[hh, rows, :], width // LANES)
                p_ref[hh, rows, 0:width] = jnp.exp2(s_ref[hh, rows, 0:width] - m).astype(BF16)
        for hh in range(FOX_PAIR):
            alpha = _lane_tile(alpha_ref[hh], 2)
            acc_ref[hh] = alpha * acc_ref[hh] + jnp.dot(
                p_ref[hh, :, 0:width], vaug_ref[hh, ks, :], preferred_element_type=F32)

    n_full = (i * bq) // bk

    def pair_body(j, carry):
        chunk(2 * j * bk, bk, False)
        chunk((2 * j + 1) * bk, bk, False)
        return carry
    lax.fori_loop(0, n_full // 2, pair_body, 0)

    @pl.when(n_full % 2 == 1)
    def _odd():
        chunk((n_full - 1) * bk, bk, False)

    tail_start = n_full * bk
    tail_blocks = (i + 1) - n_full * (bk // bq)
    for t in range(1, bk // bq + 1):
        @pl.when(tail_blocks == t)
        def _tail(t=t):
            chunk(tail_start, t * bq, True)

    for hh in range(FOX_PAIR):
        acc = acc_ref[hh]
        o_ref[:, hh * dh:(hh + 1) * dh] = (acc[:, 0:dh] / acc[:, dh:2 * dh]).astype(o_ref.dtype)


def _fox(proj, c2, *, bq, bk):
    S = proj.shape[0]
    dh = FOX_HEAD_DIM
    w = FOX_PAIR * dh
    return pl.pallas_call(
        functools.partial(_fox_kernel, bk=bk),
        name="fox",
        grid=(FOX_HEADS // FOX_PAIR, S // bq),
        in_specs=[
            pl.BlockSpec((bq, w), lambda h, i: (i, OFF_FQ // w + h)),
            pl.BlockSpec((S, w), lambda h, i: (0, OFF_FK // w + h)),
            pl.BlockSpec((S, w), lambda h, i: (0, OFF_FV // w + h)),
            pl.BlockSpec((S, LANES), lambda h, i: (0, 0)),
        ],
        out_specs=pl.BlockSpec((bq, w), lambda h, i: (i, h)),
        out_shape=jax.ShapeDtypeStruct((S, FOX_WIDTH), BF16),
        scratch_shapes=[
            pltpu.VMEM((FOX_PAIR, S, 2 * dh), BF16),
            pltpu.VMEM((FOX_PAIR, S, 2 * dh), BF16),
            pltpu.VMEM((FOX_PAIR, bq, 2 * dh), BF16),
            pltpu.VMEM((FOX_PAIR, bq, bk), F32),
            pltpu.VMEM((FOX_PAIR, bq, bk), BF16),
            pltpu.VMEM((FOX_PAIR, bq, LANES), F32),
            pltpu.VMEM((FOX_PAIR, bq, LANES), F32),
            pltpu.VMEM((FOX_PAIR, bq, 2 * dh), F32),
        ],
        compiler_params=_params("parallel", "arbitrary"),
    )(proj, proj, proj, c2)


def _gla_kernel(q_ref, k_ref, v_ref, gr_ref, small_ref, wgate_ref, bgate_ref,
                gnorm_ref, tri_ref, o_ref, st_ref):
    t = pl.program_id(1)
    T = q_ref.shape[0]
    C = GLA_CHUNK

    @pl.when(t == 0)
    def _init():
        st_ref[...] = jnp.zeros_like(st_ref)

    dk = q_ref.shape[1]
    gate = jnp.dot(small_ref[...].astype(BF16), wgate_ref[...],
                   preferred_element_type=F32) + bgate_ref[...]
    log_a = _log_sigmoid(gate) * (1.0 / GLA_TAU)

    M = tri_ref.shape[0]
    parts = [slice(r, r + M) for r in range(0, T, M)]
    pieces = jnp.concatenate([x.astype(BF16) for x in _split3(log_a)], axis=1)
    tri16 = tri_ref[...].astype(BF16)
    sums = jnp.concatenate(
        [jnp.dot(tri16, pieces[rows, :], preferred_element_type=F32) for rows in parts], axis=0)
    b = sums[:, 0:dk] + sums[:, dk:2 * dk] + sums[:, 2 * dk:3 * dk]
    b_tot = jnp.concatenate(
        [jnp.broadcast_to(b[c * C + C - 1:c * C + C, :], (C, dk)) for c in range(T // C)], axis=0)
    causal = tri_ref[...] > 0.0

    q = q_ref[...].astype(F32)
    k = k_ref[...].astype(F32)
    v = v_ref[...]
    q_dec = (q * jnp.exp(b)).astype(BF16)
    k_inv = (k * jnp.exp(-b)).astype(BF16)
    k_end = (k * jnp.exp(b_tot - b)).astype(BF16)
    o_parts = []
    for rows in parts:
        a = lax.dot_general(q_dec[rows, :], k_inv[rows, :], (((1,), (1,)), ((), ())),
                            preferred_element_type=F32)
        a = jnp.where(causal, a, 0.0).astype(BF16)
        o_parts.append(jnp.dot(a, v[rows, :], preferred_element_type=F32))
    o_intra = jnp.concatenate(o_parts, axis=0)

    for ci in range(T // C):
        lo, hi = ci * C, (ci + 1) * C
        st = st_ref[...]
        o_c = o_intra[lo:hi, :] + lax.dot_general(
            q_dec[lo:hi, :], st.astype(BF16), (((1,), (1,)), ((), ())),
            preferred_element_type=F32)
        decay = jnp.exp(b_tot[lo:lo + 1, :])
        st_ref[...] = st * decay + lax.dot_general(
            v[lo:hi, :], k_end[lo:hi, :], (((0,), (0,)), ((), ())),
            preferred_element_type=F32)
        y = _rms(o_c, gnorm_ref[...])
        o_ref[lo:hi, :] = (y * _silu(gr_ref[lo:hi, :].astype(F32))).astype(o_ref.dtype)


def _chunk_causal_mask(n):
    r = jnp.arange(n)[:, None]
    c = jnp.arange(n)[None, :]
    return ((r // GLA_CHUNK == c // GLA_CHUNK) & (r >= c)).astype(F32)


def _gla(proj, small, wgate_pad, bgate, gnorm, *, bt):
    S = proj.shape[0]
    dk, dv = GLA_HEAD_K, GLA_HEAD_V
    mask_rows = min(bt, MXU_COLS)
    return pl.pallas_call(
        _gla_kernel,
        name="gla",
        grid=(GLA_HEADS, S // bt),
        in_specs=[
            pl.BlockSpec((bt, dk), lambda h, t: (t, OFF_GQ // dk + h)),
            pl.BlockSpec((bt, dk), lambda h, t: (t, OFF_GK // dk + h)),
            pl.BlockSpec((bt, dv), lambda h, t: (t, OFF_GV // dv + h)),
            pl.BlockSpec((bt, dv), lambda h, t: (t, OFF_GR // dv + h)),
            pl.BlockSpec((bt, LANES), lambda h, t: (t, 0)),
            pl.BlockSpec((LANES, dk), lambda h, t: (0, h)),
            pl.BlockSpec((1, dk), lambda h, t: (0, h)),
            pl.BlockSpec((1, dv), lambda h, t: (0, 0)),
            pl.BlockSpec((mask_rows, mask_rows), lambda h, t: (0, 0)),
        ],
        out_specs=pl.BlockSpec((bt, dv), lambda h, t: (t, h)),
        out_shape=jax.ShapeDtypeStruct((S, GLA_WIDTH), BF16),
        scratch_shapes=[pltpu.VMEM((dv, dk), F32)],
        compiler_params=_params("parallel", "arbitrary"),
    )(proj, proj, proj, proj, small, wgate_pad, bgate, gnorm, _chunk_causal_mask(mask_rows))


def _outproj_kernel(fox_ref, gla_ref, wt_ref, wb_ref, h_ref, g_ref, o_ref):
    n = pl.program_id(1)
    bn = wt_ref.shape[1]
    for sub in _sub_tiles(bn):
        cols = pl.ds(pl.multiple_of(n * bn + sub.start, MXU_COLS), MXU_COLS)
        o_ref[:, cols] = (jnp.dot(fox_ref[...], wt_ref[:, sub], preferred_element_type=F32)
                          + jnp.dot(gla_ref[...], wb_ref[:, sub], preferred_element_type=F32))

    @pl.when(n == pl.num_programs(1) - 1)
    def _epilogue():
        def emit(rows, y):
            o_ref[rows, :] = h_ref[rows, :] + y
        _rms_rows(o_ref, g_ref, emit)


def _outproj(o_fox, o_gla, w_o, h, g, *, bm, bn):
    S, D = h.shape
    half = o_fox.shape[1]
    return pl.pallas_call(
        _outproj_kernel,
        name="outproj",
        grid=(S // bm, D // bn),
        in_specs=[
            pl.BlockSpec((bm, half), lambda i, n: (i, 0)),
            pl.BlockSpec((bm, half), lambda i, n: (i, 0)),
            pl.BlockSpec((half, bn), lambda i, n: (0, n)),
            pl.BlockSpec((half, bn), lambda i, n: (1, n)),
            pl.BlockSpec((bm, D), lambda i, n: (i, 0)),
            pl.BlockSpec((1, D), lambda i, n: (0, 0)),
        ],
        out_specs=pl.BlockSpec((bm, D), lambda i, n: (i, 0)),
        out_shape=jax.ShapeDtypeStruct((S, D), F32),
        compiler_params=_params("parallel", "arbitrary"),
    )(o_fox, o_gla, w_o, w_o, h, g)


def _ple_kernel(h_ref, p_ref, wp_ref, g_ref, wg_ref, o_ref, hb_ref, e_ref):
    n = pl.program_id(1)
    bn = wg_ref.shape[1]

    @pl.when(n == 0)
    def _prologue():
        e_ref[...] = jnp.dot(p_ref[...].astype(BF16), wp_ref[...], preferred_element_type=F32)

        def emit(rows, y):
            hb_ref[rows, :] = h_ref[rows, :].astype(BF16)
            e_ref[rows, :] = y
        _rms_rows(e_ref, g_ref, emit)

    for sub in _sub_tiles(bn):
        cols = pl.ds(pl.multiple_of(n * bn + sub.start, MXU_COLS), MXU_COLS)
        gate = jax.nn.sigmoid(jnp.dot(hb_ref[...], wg_ref[:, sub], preferred_element_type=F32))
        o_ref[:, sub] = h_ref[:, cols] + e_ref[:, cols] * gate


def _ple(h, p, w_proj, g, w_gate, *, bm, bn):
    S, D = h.shape
    dp = p.shape[1]
    return pl.pallas_call(
        _ple_kernel,
        name="ple",
        grid=(S // bm, D // bn),
        in_specs=[
            pl.BlockSpec((bm, D), lambda i, n: (i, 0)),
            pl.BlockSpec((bm, dp), lambda i, n: (i, 0)),
            pl.BlockSpec((dp, D), lambda i, n: (0, 0)),
            pl.BlockSpec((1, D), lambda i, n: (0, 0)),
            pl.BlockSpec((D, bn), lambda i, n: (0, n)),
        ],
        out_specs=pl.BlockSpec((bm, bn), lambda i, n: (i, n)),
        out_shape=jax.ShapeDtypeStruct((S, D), F32),
        scratch_shapes=[pltpu.VMEM((bm, D), BF16), pltpu.VMEM((bm, D), F32)],
        compiler_params=_params("parallel", "arbitrary"),
    )(h, p, w_proj, g, w_gate)


def _tile(n, preferred):
    t = min(n, preferred)
    while n % t:
        t //= 2
    return t


def _split_w_in(w_in):
    D = w_in.shape[0]
    ff_lo = 3 * FOX_WIDTH
    ff_hi = ff_lo + FOX_HEADS
    glr_lo = ff_hi + 2 * GLA_KEY_WIDTH + 2 * GLA_WIDTH
    glr_hi = glr_lo + GLA_GATE_RANK
    w_t = w_in.T.astype(BF16)
    pad = jnp.zeros((LANES - FOX_HEADS - GLA_GATE_RANK, D), BF16)
    small_t = jnp.concatenate([w_t[ff_lo:ff_hi], w_t[glr_lo:glr_hi], pad], axis=0)
    return w_t, small_t, ff_hi


def kernel(x, p, ffn1_norm_pre, ffn1_w_gate, ffn1_w_up, ffn1_w_down, ffn1_norm_post, mix_norm_pre, w_in, fox_b_f, gla_w_gate, gla_b_gate, gla_norm_g, w_o, mix_norm_post, ffn2_norm_pre, ffn2_w_gate, ffn2_w_up, ffn2_w_down, ffn2_norm_post, ple_w_proj, ple_norm, ple_w_gate):
    B, S, D = x.shape
    assert B == 1, "the attention kernels treat the row axis as one sequence"
    depth = w_in.shape[0]
    h = x.reshape(S, D)

    bm = _tile(S, 512)
    bf = _tile(ffn1_w_gate.shape[-1], 256)
    bq = _tile(S, 512)
    bk = max(bq, _tile(S, 1024))
    bt = _tile(S, 512)

    col_scale = jnp.ones((1, MAIN_COLS), F32)
    col_scale = col_scale.at[:, OFF_FQ:OFF_FQ + FOX_WIDTH].set(FOX_HEAD_DIM ** -0.5 * LOG2_E)
    col_scale = col_scale.at[:, OFF_GQ:OFF_GQ + GLA_KEY_WIDTH].set(GLA_HEAD_K ** -0.5)

    def row(v):
        return v.reshape(1, -1).astype(F32)

    for i in range(depth):
        h = _ffn(h, row(ffn1_norm_pre[i]), ffn1_w_gate[i], ffn1_w_up[i], ffn1_w_down[i],
                 row(ffn1_norm_post[i]), bm=bm, bf=bf)

        w_t, w_small_t, rest_row = _split_w_in(w_in[i])
        bn = 1024
        proj, small = _inproj(h, row(mix_norm_pre[i]), w_t, w_small_t, col_scale, bm=_tile(S, 1024),
                              bn=bn, n1=3 * FOX_WIDTH // bn, rest_row=rest_row)

        bias_f = jnp.zeros((1, LANES), F32).at[0, :FOX_HEADS].set(fox_b_f[i])
        c2 = _fox_gate(small, bias_f, bt=bt)
        o_fox = _fox(proj, c2, bq=bq, bk=bk)

        wgate_pad = jnp.zeros((LANES, GLA_KEY_WIDTH), F32).at[
            FOX_HEADS:FOX_HEADS + GLA_GATE_RANK, :].set(gla_w_gate[i]).astype(BF16)
        o_gla = _gla(proj, small, wgate_pad, row(gla_b_gate[i]), row(gla_norm_g[i]), bt=bt)

        h = _outproj(o_fox, o_gla, w_o[i].astype(BF16), h, row(mix_norm_post[i]), bm=bm, bn=512)

        h = _ffn(h, row(ffn2_norm_pre[i]), ffn2_w_gate[i], ffn2_w_up[i], ffn2_w_down[i],
                 row(ffn2_norm_post[i]), bm=bm, bf=bf)

        h = _ple(h, p[i].reshape(S, -1), ple_w_proj[i].astype(BF16), row(ple_norm[i]),
                 ple_w_gate[i].astype(BF16), bm=bm, bn=1024)
    return h.reshape(B, S, D)
```

```python
import functools

import jax
import jax.numpy as jnp
from jax import lax
from jax.experimental import pallas as pl
from jax.experimental.pallas import tpu as pltpu

F32 = jnp.float32
BF16 = jnp.bfloat16
HIGHEST = lax.Precision.HIGHEST

EPS = 1e-6
LOG2_E = 1.4426950408889634
MACARON_WEIGHT = 0.5
FOX_HEAD_DIM = 128
FOX_HEADS = 16
FOX_WIDTH = FOX_HEADS * FOX_HEAD_DIM
GLA_HEADS = 4
GLA_WIDTH = 2048
GLA_HEAD_V = GLA_WIDTH // GLA_HEADS
GLA_KEY_WIDTH = GLA_WIDTH // 2
GLA_HEAD_K = GLA_KEY_WIDTH // GLA_HEADS
GLA_GATE_RANK = 16
GLA_TAU = 16.0
GLA_CHUNK = 64

LANES = 128
MXU_COLS = 256
VMEM_LIMIT_BYTES = 56 * 1024 * 1024

MAIN_COLS = 3 * FOX_WIDTH + 2 * GLA_WIDTH + 2 * GLA_KEY_WIDTH
OFF_FQ = 0
OFF_FK = FOX_WIDTH
OFF_FV = 2 * FOX_WIDTH
OFF_GQ = 3 * FOX_WIDTH
OFF_GK = OFF_GQ + GLA_KEY_WIDTH
OFF_GV = OFF_GK + GLA_KEY_WIDTH
OFF_GR = OFF_GV + GLA_WIDTH

ROW_CHUNK = 32
ROW_BLOCK = 128


def _params(*semantics):
    return pltpu.CompilerParams(dimension_semantics=semantics,
                                vmem_limit_bytes=VMEM_LIMIT_BYTES)


def _rms(x, g):
    ms = jnp.mean(x * x, axis=-1, keepdims=True)
    return x * lax.rsqrt(ms + EPS) * g


def _log_sigmoid(x):
    return jnp.minimum(x, 0.0) - jnp.log(1.0 + jnp.exp(-jnp.abs(x)))


def _silu(x):
    return x * jax.nn.sigmoid(x)


def _sub_tiles(n_cols):
    return [slice(c, c + MXU_COLS) for c in range(0, n_cols, MXU_COLS)]


def _rms_rows(src_ref, g_ref, emit):
    def body(b, carry):
        chunks = [pl.ds(pl.multiple_of(b * ROW_BLOCK + c * ROW_CHUNK, ROW_CHUNK), ROW_CHUNK)
                  for c in range(ROW_BLOCK // ROW_CHUNK)]
        scales = []
        for rows in chunks:
            x = src_ref[rows, :]
            scales.append(lax.rsqrt(jnp.mean(x * x, axis=-1, keepdims=True) + EPS))
        for rows, scale in zip(chunks, scales):
            emit(rows, src_ref[rows, :] * scale * g_ref[...])
        return carry
    lax.fori_loop(0, src_ref.shape[0] // ROW_BLOCK, body, 0)


def _ffn_prologue(x_ref, gpre_ref, xn_ref, o_ref):
    def emit(rows, y):
        xn_ref[rows, :] = y.astype(BF16)
        o_ref[rows, :] = jnp.zeros((ROW_CHUNK, o_ref.shape[1]), F32)
    _rms_rows(x_ref, gpre_ref, emit)


def _ffn_epilogue(x_ref, gpost_ref, o_ref):
    def emit(rows, y):
        o_ref[rows, :] = x_ref[rows, :] + MACARON_WEIGHT * y
    _rms_rows(o_ref, gpost_ref, emit)


FFN_COPY_STEPS = 4


def _ffn_kernel(x_ref, gpre_ref, wg_ref, wu_ref, wd_ref, gpost_ref, first_ref, o_ref, xn_ref):
    i = pl.program_id(0)
    f = pl.program_id(1)
    rows_c = first_ref.shape[0]

    @pl.when(jnp.logical_and(i == 0, f < FFN_COPY_STEPS))
    def _():
        o_ref[pl.ds(pl.multiple_of(f * rows_c, rows_c), rows_c), :] = first_ref[...]

    @pl.when(i > 0)
    def _():
        @pl.when(f == 0)
        def _():
            _ffn_prologue(x_ref, gpre_ref, xn_ref, o_ref)

        xn = xn_ref[...]
        g = jnp.dot(xn, wg_ref[...], preferred_element_type=F32)
        u = jnp.dot(xn, wu_ref[...], preferred_element_type=F32)
        hid = (_silu(g) * u).astype(BF16)
        o_ref[...] += jnp.dot(hid, wd_ref[...], preferred_element_type=F32)

        @pl.when(f == pl.num_programs(1) - 1)
        def _():
            _ffn_epilogue(x_ref, gpost_ref, o_ref)


FFN_RING = 3


def _ffn_first_kernel(x_ref, gpre_ref, wg_hbm, wu_hbm, wd_ref, gpost_ref,
                      o_ref, wg16_ref, wu16_ref, wd16_ref, xn_ref, g_ref, u_ref,
                      wg_ring, wu_ring, sem):
    f = pl.program_id(0)
    k = pl.program_id(1)
    nf = pl.num_programs(0)
    _, dk, bf = wg_ring.shape
    s = f * 2 + k

    def tile_copies(ff, kk, slot):
        rows = pl.ds(pl.multiple_of(kk * dk, dk), dk)
        cols = pl.ds(pl.multiple_of(ff * bf, bf), bf)
        return (pltpu.make_async_copy(wg_hbm.at[rows, cols], wg_ring.at[slot], sem.at[0, slot]),
                pltpu.make_async_copy(wu_hbm.at[rows, cols], wu_ring.at[slot], sem.at[1, slot]))

    @pl.when(s == 0)
    def _():
        for cp in tile_copies(0, 0, 0) + tile_copies(0, 1, 1):
            cp.start()
        _ffn_prologue(x_ref, gpre_ref, xn_ref, o_ref)

    @pl.when(f + 1 < nf)
    def _():
        for cp in tile_copies(f + 1, k, (s + 2) % FFN_RING):
            cp.start()

    slot = s % FFN_RING
    for cp in tile_copies(f, k, slot):
        cp.wait()
    wg = wg_ring[slot].astype(BF16)
    wu = wu_ring[slot].astype(BF16)
    wg16_ref[...] = wg
    wu16_ref[...] = wu
    xk = xn_ref[:, pl.ds(pl.multiple_of(k * dk, dk), dk)]
    g = jnp.dot(xk, wg, preferred_element_type=F32)
    u = jnp.dot(xk, wu, preferred_element_type=F32)

    @pl.when(k == 0)
    def _():
        g_ref[...] = g
        u_ref[...] = u

    @pl.when(k == 1)
    def _():
        wd = wd_ref[...].astype(BF16)
        wd16_ref[...] = wd
        hid = (_silu(g_ref[...] + g) * (u_ref[...] + u)).astype(BF16)
        o_ref[...] += jnp.dot(hid, wd, preferred_element_type=F32)

    @pl.when(jnp.logical_and(f == pl.num_programs(0) - 1, k == 1))
    def _():
        _ffn_epilogue(x_ref, gpost_ref, o_ref)


def _ffn(x, g_pre, wg, wu, wd, g_post, *, bm, bf):
    S, D = x.shape
    d_ff = wg.shape[1]
    once = dict(pipeline_mode=pl.Buffered(1))
    h, wg16, wu16, wd16 = pl.pallas_call(
        _ffn_first_kernel,
        name="ffn_first",
        grid=(d_ff // bf, 2),
        in_specs=[
            pl.BlockSpec((bm, D), lambda f, k: (0, 0), **once),
            pl.BlockSpec((1, D), lambda f, k: (0, 0)),
            pl.BlockSpec(memory_space=pl.ANY),
            pl.BlockSpec(memory_space=pl.ANY),
            pl.BlockSpec((bf, D), lambda f, k: (f, 0)),
            pl.BlockSpec((1, D), lambda f, k: (0, 0)),
        ],
        out_specs=[
            pl.BlockSpec((bm, D), lambda f, k: (0, 0), **once),
            pl.BlockSpec((D // 2, bf), lambda f, k: (k, f)),
            pl.BlockSpec((D // 2, bf), lambda f, k: (k, f)),
            pl.BlockSpec((bf, D), lambda f, k: (f, 0)),
        ],
        out_shape=[
            jax.ShapeDtypeStruct((bm, D), F32),
            jax.ShapeDtypeStruct((D, d_ff), BF16),
            jax.ShapeDtypeStruct((D, d_ff), BF16),
            jax.ShapeDtypeStruct((d_ff, D), BF16),
        ],
        scratch_shapes=[pltpu.VMEM((bm, D), BF16), pltpu.VMEM((bm, bf), F32), pltpu.VMEM((bm, bf), F32),
                        pltpu.VMEM((FFN_RING, D // 2, bf), F32), pltpu.VMEM((FFN_RING, D // 2, bf), F32),
                        pltpu.SemaphoreType.DMA((2, FFN_RING))],
        compiler_params=_params("arbitrary", "arbitrary"),
    )(x, g_pre, wg, wu, wd, g_post)
    if S == bm:
        return h
    nf = d_ff // bf
    c = FFN_COPY_STEPS
    assert nf >= c

    def ff_tile(i, f):
        return jnp.where(i == 0, 0, f)

    return pl.pallas_call(
        _ffn_kernel,
        name="ffn",
        grid=(S // bm, nf),
        in_specs=[
            pl.BlockSpec((bm, D), lambda i, f: (i, 0)),
            pl.BlockSpec((1, D), lambda i, f: (0, 0)),
            pl.BlockSpec((D, bf), lambda i, f: (0, ff_tile(i, f))),
            pl.BlockSpec((D, bf), lambda i, f: (0, ff_tile(i, f))),
            pl.BlockSpec((bf, D), lambda i, f: (ff_tile(i, f), 0)),
            pl.BlockSpec((1, D), lambda i, f: (0, 0)),
            pl.BlockSpec((bm // c, D), lambda i, f: (jnp.where(i == 0, jnp.minimum(f, c - 1), c - 1), 0)),
        ],
        out_specs=pl.BlockSpec((bm, D), lambda i, f: (i, 0)),
        out_shape=jax.ShapeDtypeStruct((S, D), F32),
        scratch_shapes=[pltpu.VMEM((bm, D), BF16)],
        compiler_params=_params("parallel", "arbitrary"),
    )(x, g_pre, wg16, wu16, wd16, g_post, h)


NT_DIMS = (((1,), (1,)), ((), ()))


def _inproj_kernel(h_ref, g_ref, wt_ref, wst_ref, scale_ref, o_ref, os_ref, a_ref):
    n = pl.program_id(1)

    @pl.when(n == 0)
    def _prologue():
        def emit(rows, y):
            a_ref[rows, :] = y.astype(BF16)
        _rms_rows(h_ref, g_ref, emit)
        os_ref[...] = lax.dot_general(a_ref[...], wst_ref[...], NT_DIMS, preferred_element_type=F32)

    for cols in _sub_tiles(wt_ref.shape[0]):
        acc = lax.dot_general(a_ref[...], wt_ref[cols, :], NT_DIMS, preferred_element_type=F32)
        o_ref[:, cols] = (acc * scale_ref[:, cols]).astype(BF16)


def _inproj(h, g, w_t, w_small_t, col_scale, *, bm, bn, n1, rest_row):
    S, D = h.shape
    n_cols = col_scale.shape[1]

    def w_row(i, n):
        return pl.multiple_of(jnp.where(n < n1, n * bn, rest_row + (n - n1) * bn), FOX_HEADS)

    return pl.pallas_call(
        _inproj_kernel,
        name="inproj",
        grid=(S // bm, n_cols // bn),
        in_specs=[
            pl.BlockSpec((bm, D), lambda i, n: (i, 0), pipeline_mode=pl.Buffered(1)),
            pl.BlockSpec((1, D), lambda i, n: (0, 0)),
            pl.BlockSpec((pl.Element(bn), pl.Element(D)), lambda i, n: (w_row(i, n), 0)),
            pl.BlockSpec((LANES, D), lambda i, n: (0, 0)),
            pl.BlockSpec((1, bn), lambda i, n: (0, n)),
        ],
        out_specs=[
            pl.BlockSpec((bm, bn), lambda i, n: (i, n)),
            pl.BlockSpec((bm, LANES), lambda i, n: (i, 0)),
        ],
        out_shape=[
            jax.ShapeDtypeStruct((S, n_cols), BF16),
            jax.ShapeDtypeStruct((S, LANES), F32),
        ],
        scratch_shapes=[pltpu.VMEM((bm, D), BF16)],
        compiler_params=_params("parallel", "arbitrary"),
    )(h, g, w_t, w_small_t, col_scale)


def _fox_gate_kernel(small_ref, bias_ref, c_ref, carry_ref):
    t = pl.program_id(0)
    T = small_ref.shape[0]

    @pl.when(t == 0)
    def _init():
        carry_ref[...] = jnp.zeros_like(carry_ref)

    log_f = _log_sigmoid(small_ref[...] + bias_ref[...])
    row = lax.broadcasted_iota(jnp.int32, (T, T), 0)
    col = lax.broadcasted_iota(jnp.int32, (T, T), 1)
    tri = jnp.where(row >= col, 1.0, 0.0).astype(F32)
    c = jnp.dot(tri, log_f, precision=HIGHEST, preferred_element_type=F32) + carry_ref[...]
    c_ref[...] = c * LOG2_E
    carry_ref[...] = c[T - 1:T, :]


def _fox_gate(small, bias, *, bt):
    S = small.shape[0]
    return pl.pallas_call(
        _fox_gate_kernel,
        name="fox_gate",
        grid=(S // bt,),
        in_specs=[
            pl.BlockSpec((bt, LANES), lambda t: (t, 0)),
            pl.BlockSpec((1, LANES), lambda t: (0, 0)),
        ],
        out_specs=pl.BlockSpec((bt, LANES), lambda t: (t, 0)),
        out_shape=jax.ShapeDtypeStruct((S, LANES), F32),
        scratch_shapes=[pltpu.VMEM((1, LANES), F32)],
        compiler_params=_params("arbitrary"),
    )(small, bias)


FOX_PAIR = 2
FOX_ROWS = 16


def _split3(c):
    hi = c.astype(BF16).astype(F32)
    mid = (c - hi).astype(BF16).astype(F32)
    lo = (c - hi - mid).astype(BF16).astype(F32)
    return hi, mid, lo


def _lane_tile(x, n):
    return jnp.concatenate([x] * n, axis=1)


def _head_column(block, head):
    lane = lax.broadcasted_iota(jnp.int32, block.shape, 1)
    return jnp.sum(jnp.where(lane == head, block, 0.0), axis=1, keepdims=True)


def _fox_kernel(q_ref, k_ref, v_ref, c_ref, o_ref, kaug_ref, vaug_ref, qaug_ref, s_ref, p_ref,
                m_ref, alpha_ref, acc_ref, *, bk):
    hp = pl.program_id(0)
    i = pl.program_id(1)
    bq = q_ref.shape[0]
    S = k_ref.shape[0]
    dh = FOX_HEAD_DIM
    R = FOX_ROWS

    @pl.when(i == 0)
    def _build_kv_side():
        rows_per = 128

        def body(r, carry):
            rows = pl.ds(pl.multiple_of(r * rows_per, rows_per), rows_per)
            cblk = c_ref[rows, :]
            lane = lax.broadcasted_iota(jnp.int32, (rows_per, LANES), 1)
            for hh in range(FOX_PAIR):
                hi, mid, lo = _split3(_head_column(cblk, hp * FOX_PAIR + hh))
                extra = jnp.where(lane < 3, 1.0,
                                  jnp.where(lane == 3, -hi,
                                            jnp.where(lane == 4, -mid,
                                                      jnp.where(lane == 5, -lo, 0.0))))
                kaug_ref[hh, rows, 0:dh] = k_ref[rows, hh * dh:(hh + 1) * dh]
                kaug_ref[hh, rows, dh:2 * dh] = extra.astype(BF16)
                vaug_ref[hh, rows, 0:dh] = v_ref[rows, hh * dh:(hh + 1) * dh]
                vaug_ref[hh, rows, dh:2 * dh] = jnp.ones((rows_per, dh), BF16)
            return carry
        lax.fori_loop(0, S // rows_per, body, 0, unroll=4)

    cq_blk = c_ref[pl.ds(pl.multiple_of(i * bq, bq), bq), :]
    lane_q = lax.broadcasted_iota(jnp.int32, (bq, LANES), 1)
    for hh in range(FOX_PAIR):
        hi, mid, lo = _split3(_head_column(cq_blk, hp * FOX_PAIR + hh))
        extra = jnp.where(lane_q == 0, hi,
                          jnp.where(lane_q == 1, mid,
                                    jnp.where(lane_q == 2, lo,
                                              jnp.where(lane_q < 6, 1.0, 0.0))))
        qaug_ref[hh, :, 0:dh] = q_ref[:, hh * dh:(hh + 1) * dh]
        qaug_ref[hh, :, dh:2 * dh] = extra.astype(BF16)
        m_ref[hh] = jnp.full((bq, LANES), -jnp.inf, F32)
        acc_ref[hh] = jnp.zeros((bq, 2 * dh), F32)

    groups = [slice(g * R, (g + 1) * R) for g in range(bq // R)]

    def chunk(k0, width, masked):
        ks = pl.ds(pl.multiple_of(k0, bq), width)
        for hh in range(FOX_PAIR):
            s_ref[hh, :, 0:width] = lax.dot_general(
                qaug_ref[hh], kaug_ref[hh, ks, :], (((1,), (1,)), ((), ())),
                preferred_element_type=F32)
        if masked:
            diff = (lax.broadcasted_iota(jnp.int32, (R, width), 1)
                    - lax.broadcasted_iota(jnp.int32, (R, width), 0))
            limit = i * bq - k0
        for hh in range(FOX_PAIR):
            for g, rows in enumerate(groups):
                s = s_ref[hh, rows, 0:width]
                if masked:
                    s = jnp.where(diff <= limit + g * R, s, -jnp.inf)
                    s_ref[hh, rows, 0:width] = s
                m_old = m_ref[hh, rows, :]
                m_new = jnp.maximum(m_old, jnp.broadcast_to(
                    jnp.max(s, axis=1, keepdims=True), (R, LANES)))
                alpha_ref[hh, rows, :] = jnp.exp2(m_old - m_new)
                m_ref[hh, rows, :] = m_new
        for hh in range(FOX_PAIR):
            for rows in groups:
                m = _lane_tile(m_ref[hh, rows, :], width // LANES)
                p_ref[hh, rows, 0:width] = jnp.exp2(s_ref[hh, rows, 0:width] - m).astype(BF16)
        for hh in range(FOX_PAIR):
            alpha = _lane_tile(alpha_ref[hh], 2)
            acc_ref[hh] = alpha * acc_ref[hh] + jnp.dot(
                p_ref[hh, :, 0:width], vaug_ref[hh, ks, :], preferred_element_type=F32)

    n_full = (i * bq) // bk

    def pair_body(j, carry):
        chunk(2 * j * bk, bk, False)
        chunk((2 * j + 1) * bk, bk, False)
        return carry
    lax.fori_loop(0, n_full // 2, pair_body, 0)

    @pl.when(n_full % 2 == 1)
    def _odd():
        chunk((n_full - 1) * bk, bk, False)

    tail_start = n_full * bk
    tail_blocks = (i + 1) - n_full * (bk // bq)
    for t in range(1, bk // bq + 1):
        @pl.when(tail_blocks == t)
        def _tail(t=t):
            chunk(tail_start, t * bq, True)

    for hh in range(FOX_PAIR):
        acc = acc_ref[hh]
        o_ref[:, hh * dh:(hh + 1) * dh] = (acc[:, 0:dh] / acc[:, dh:2 * dh]).astype(o_ref.dtype)


def _fox(proj, c2, *, bq, bk):
    S = proj.shape[0]
    dh = FOX_HEAD_DIM
    w = FOX_PAIR * dh
    return pl.pallas_call(
        functools.partial(_fox_kernel, bk=bk),
        name="fox",
        grid=(FOX_HEADS // FOX_PAIR, S // bq),
        in_specs=[
            pl.BlockSpec((bq, w), lambda h, i: (i, OFF_FQ // w + h)),
            pl.BlockSpec((S, w), lambda h, i: (0, OFF_FK // w + h)),
            pl.BlockSpec((S, w), lambda h, i: (0, OFF_FV // w + h)),
            pl.BlockSpec((S, LANES), lambda h, i: (0, 0)),
        ],
        out_specs=pl.BlockSpec((bq, w), lambda h, i: (i, h)),
        out_shape=jax.ShapeDtypeStruct((S, FOX_WIDTH), BF16),
        scratch_shapes=[
            pltpu.VMEM((FOX_PAIR, S, 2 * dh), BF16),
            pltpu.VMEM((FOX_PAIR, S, 2 * dh), BF16),
            pltpu.VMEM((FOX_PAIR, bq, 2 * dh), BF16),
            pltpu.VMEM((FOX_PAIR, bq, bk), F32),
            pltpu.VMEM((FOX_PAIR, bq, bk), BF16),
            pltpu.VMEM((FOX_PAIR, bq, LANES), F32),
            pltpu.VMEM((FOX_PAIR, bq, LANES), F32),
            pltpu.VMEM((FOX_PAIR, bq, 2 * dh), F32),
        ],
        compiler_params=_params("parallel", "arbitrary"),
    )(proj, proj, proj, c2)


def _gla_kernel(q_ref, k_ref, v_ref, gr_ref, small_ref, wgate_ref, bgate_ref,
                gnorm_ref, tri_ref, o_ref, st_ref):
    t = pl.program_id(1)
    T = q_ref.shape[0]
    C = GLA_CHUNK

    @pl.when(t == 0)
    def _init():
        st_ref[...] = jnp.zeros_like(st_ref)

    dk = q_ref.shape[1]
    gate = jnp.dot(small_ref[...].astype(BF16), wgate_ref[...],
                   preferred_element_type=F32) + bgate_ref[...]
    log_a = _log_sigmoid(gate) * (1.0 / GLA_TAU)

    M = tri_ref.shape[0]
    parts = [slice(r, r + M) for r in range(0, T, M)]
    pieces = jnp.concatenate([x.astype(BF16) for x in _split3(log_a)], axis=1)
    tri16 = tri_ref[...].astype(BF16)
    sums = jnp.concatenate(
        [jnp.dot(tri16, pieces[rows, :], preferred_element_type=F32) for rows in parts], axis=0)
    b = sums[:, 0:dk] + sums[:, dk:2 * dk] + sums[:, 2 * dk:3 * dk]
    b_tot = jnp.concatenate(
        [jnp.broadcast_to(b[c * C + C - 1:c * C + C, :], (C, dk)) for c in range(T // C)], axis=0)
    causal = tri_ref[...] > 0.0

    q = q_ref[...].astype(F32)
    k = k_ref[...].astype(F32)
    v = v_ref[...]
    q_dec = (q * jnp.exp(b)).astype(BF16)
    k_inv = (k * jnp.exp(-b)).astype(BF16)
    k_end = (k * jnp.exp(b_tot - b)).astype(BF16)
    o_parts = []
    for rows in parts:
        a = lax.dot_general(q_dec[rows, :], k_inv[rows, :], (((1,), (1,)), ((), ())),
                            preferred_element_type=F32)
        a = jnp.where(causal, a, 0.0).astype(BF16)
        o_parts.append(jnp.dot(a, v[rows, :], preferred_element_type=F32))
    o_intra = jnp.concatenate(o_parts, axis=0)

    for ci in range(T // C):
        lo, hi = ci * C, (ci + 1) * C
        st = st_ref[...]
        o_c = o_intra[lo:hi, :] + lax.dot_general(
            q_dec[lo:hi, :], st.astype(BF16), (((1,), (1,)), ((), ())),
            preferred_element_type=F32)
        decay = jnp.exp(b_tot[lo:lo + 1, :])
        st_ref[...] = st * decay + lax.dot_general(
            v[lo:hi, :], k_end[lo:hi, :], (((0,), (0,)), ((), ())),
            preferred_element_type=F32)
        y = _rms(o_c, gnorm_ref[...])
        o_ref[lo:hi, :] = (y * _silu(gr_ref[lo:hi, :].astype(F32))).astype(o_ref.dtype)


def _chunk_causal_mask(n):
    r = jnp.arange(n)[:, None]
    c = jnp.arange(n)[None, :]
    return ((r // GLA_CHUNK == c // GLA_CHUNK) & (r >= c)).astype(F32)


def _gla(proj, small, wgate_pad, bgate, gnorm, *, bt):
    S = proj.shape[0]
    dk, dv = GLA_HEAD_K, GLA_HEAD_V
    mask_rows = min(bt, MXU_COLS)
    return pl.pallas_call(
        _gla_kernel,
        name="gla",
        grid=(GLA_HEADS, S // bt),
        in_specs=[
            pl.BlockSpec((bt, dk), lambda h, t: (t, OFF_GQ // dk + h)),
            pl.BlockSpec((bt, dk), lambda h, t: (t, OFF_GK // dk + h)),
            pl.BlockSpec((bt, dv), lambda h, t: (t, OFF_GV // dv + h)),
            pl.BlockSpec((bt, dv), lambda h, t: (t, OFF_GR // dv + h)),
            pl.BlockSpec((bt, LANES), lambda h, t: (t, 0)),
            pl.BlockSpec((LANES, dk), lambda h, t: (0, h)),
            pl.BlockSpec((1, dk), lambda h, t: (0, h)),
            pl.BlockSpec((1, dv), lambda h, t: (0, 0)),
            pl.BlockSpec((mask_rows, mask_rows), lambda h, t: (0, 0)),
        ],
        out_specs=pl.BlockSpec((bt, dv), lambda h, t: (t, h)),
        out_shape=jax.ShapeDtypeStruct((S, GLA_WIDTH), BF16),
        scratch_shapes=[pltpu.VMEM((dv, dk), F32)],
        compiler_params=_params("parallel", "arbitrary"),
    )(proj, proj, proj, proj, small, wgate_pad, bgate, gnorm, _chunk_causal_mask(mask_rows))


def _outproj_kernel(fox_ref, gla_ref, wt_ref, wb_ref, h_ref, g_ref, o_ref):
    n = pl.program_id(1)
    bn = wt_ref.shape[1]
    for sub in _sub_tiles(bn):
        cols = pl.ds(pl.multiple_of(n * bn + sub.start, MXU_COLS), MXU_COLS)
        o_ref[:, cols] = (jnp.dot(fox_ref[...], wt_ref[:, sub], preferred_element_type=F32)
                          + jnp.dot(gla_ref[...], wb_ref[:, sub], preferred_element_type=F32))

    @pl.when(n == pl.num_programs(1) - 1)
    def _epilogue():
        def emit(rows, y):
            o_ref[rows, :] = h_ref[rows, :] + y
        _rms_rows(o_ref, g_ref, emit)


def _outproj(o_fox, o_gla, w_o, h, g, *, bm, bn):
    S, D = h.shape
    half = o_fox.shape[1]
    return pl.pallas_call(
        _outproj_kernel,
        name="outproj",
        grid=(S // bm, D // bn),
        in_specs=[
            pl.BlockSpec((bm, half), lambda i, n: (i, 0)),
            pl.BlockSpec((bm, half), lambda i, n: (i, 0)),
            pl.BlockSpec((half, bn), lambda i, n: (0, n)),
            pl.BlockSpec((half, bn), lambda i, n: (1, n)),
            pl.BlockSpec((bm, D), lambda i, n: (i, 0)),
            pl.BlockSpec((1, D), lambda i, n: (0, 0)),
        ],
        out_specs=pl.BlockSpec((bm, D), lambda i, n: (i, 0)),
        out_shape=jax.ShapeDtypeStruct((S, D), F32),
        compiler_params=_params("parallel", "arbitrary"),
    )(o_fox, o_gla, w_o, w_o, h, g)


def _ple_kernel(h_ref, p_ref, wp_ref, g_ref, wg_ref, o_ref, hb_ref, e_ref):
    n = pl.program_id(1)
    bn = wg_ref.shape[1]

    @pl.when(n == 0)
    def _prologue():
        e_ref[...] = jnp.dot(p_ref[...].astype(BF16), wp_ref[...], preferred_element_type=F32)

        def emit(rows, y):
            hb_ref[rows, :] = h_ref[rows, :].astype(BF16)
            e_ref[rows, :] = y
        _rms_rows(e_ref, g_ref, emit)

    for sub in _sub_tiles(bn):
        cols = pl.ds(pl.multiple_of(n * bn + sub.start, MXU_COLS), MXU_COLS)
        gate = jax.nn.sigmoid(jnp.dot(hb_ref[...], wg_ref[:, sub], preferred_element_type=F32))
        o_ref[:, sub] = h_ref[:, cols] + e_ref[:, cols] * gate


def _ple(h, p, w_proj, g, w_gate, *, bm, bn):
    S, D = h.shape
    dp = p.shape[1]
    return pl.pallas_call(
        _ple_kernel,
        name="ple",
        grid=(S // bm, D // bn),
        in_specs=[
            pl.BlockSpec((bm, D), lambda i, n: (i, 0)),
            pl.BlockSpec((bm, dp), lambda i, n: (i, 0)),
            pl.BlockSpec((dp, D), lambda i, n: (0, 0)),
            pl.BlockSpec((1, D), lambda i, n: (0, 0)),
            pl.BlockSpec((D, bn), lambda i, n: (0, n)),
        ],
        out_specs=pl.BlockSpec((bm, bn), lambda i, n: (i, n)),
        out_shape=jax.ShapeDtypeStruct((S, D), F32),
        scratch_shapes=[pltpu.VMEM((bm, D), BF16), pltpu.VMEM((bm, D), F32)],
        compiler_params=_params("parallel", "arbitrary"),
    )(h, p, w_proj, g, w_gate)


def _tile(n, preferred):
    t = min(n, preferred)
    while n % t:
        t //= 2
    return t


def _split_w_in(w_in):
    D = w_in.shape[0]
    ff_lo = 3 * FOX_WIDTH
    ff_hi = ff_lo + FOX_HEADS
    glr_lo = ff_hi + 2 * GLA_KEY_WIDTH + 2 * GLA_WIDTH
    glr_hi = glr_lo + GLA_GATE_RANK
    w_t = w_in.T.astype(BF16)
    pad = jnp.zeros((LANES - FOX_HEADS - GLA_GATE_RANK, D), BF16)
    small_t = jnp.concatenate([w_t[ff_lo:ff_hi], w_t[glr_lo:glr_hi], pad], axis=0)
    return w_t, small_t, ff_hi


def kernel(x, p, ffn1_norm_pre, ffn1_w_gate, ffn1_w_up, ffn1_w_down, ffn1_norm_post, mix_norm_pre, w_in, fox_b_f, gla_w_gate, gla_b_gate, gla_norm_g, w_o, mix_norm_post, ffn2_norm_pre, ffn2_w_gate, ffn2_w_up, ffn2_w_down, ffn2_norm_post, ple_w_proj, ple_norm, ple_w_gate):
    B, S, D = x.shape
    assert B == 1, "the attention kernels treat the row axis as one sequence"
    depth = w_in.shape[0]
    h = x.reshape(S, D)

    bm = _tile(S, 512)
    bf = _tile(ffn1_w_gate.shape[-1], 256)
    bq = _tile(S, 512)
    bk = max(bq, _tile(S, 1024))
    bt = _tile(S, 512)

    col_scale = jnp.ones((1, MAIN_COLS), F32)
    col_scale = col_scale.at[:, OFF_FQ:OFF_FQ + FOX_WIDTH].set(FOX_HEAD_DIM ** -0.5 * LOG2_E)
    col_scale = col_scale.at[:, OFF_GQ:OFF_GQ + GLA_KEY_WIDTH].set(GLA_HEAD_K ** -0.5)

    def row(v):
        return v.reshape(1, -1).astype(F32)

    for i in range(depth):
        h = _ffn(h, row(ffn1_norm_pre[i]), ffn1_w_gate[i], ffn1_w_up[i], ffn1_w_down[i],
                 row(ffn1_norm_post[i]), bm=bm, bf=bf)

        w_t, w_small_t, rest_row = _split_w_in(w_in[i])
        bn = 1024
        proj, small = _inproj(h, row(mix_norm_pre[i]), w_t, w_small_t, col_scale, bm=_tile(S, 1024),
                              bn=bn, n1=3 * FOX_WIDTH // bn, rest_row=rest_row)

        bias_f = jnp.zeros((1, LANES), F32).at[0, :FOX_HEADS].set(fox_b_f[i])
        c2 = _fox_gate(small, bias_f, bt=bt)
        o_fox = _fox(proj, c2, bq=bq, bk=bk)

        wgate_pad = jnp.zeros((LANES, GLA_KEY_WIDTH), F32).at[
            FOX_HEADS:FOX_HEADS + GLA_GATE_RANK, :].set(gla_w_gate[i]).astype(BF16)
        o_gla = _gla(proj, small, wgate_pad, row(gla_b_gate[i]), row(gla_norm_g[i]), bt=bt)

        h = _outproj(o_fox, o_gla, w_o[i].astype(BF16), h, row(mix_norm_post[i]), bm=bm, bn=512)

        h = _ffn(h, row(ffn2_norm_pre[i]), ffn2_w_gate[i], ffn2_w_up[i], ffn2_w_down[i],
                 row(ffn2_norm_post[i]), bm=bm, bf=bf)

        h = _ple(h, p[i].reshape(S, -1), ple_w_proj[i].astype(BF16), row(ple_norm[i]),
                 ple_w_gate[i].astype(BF16), bm=bm, bn=1024)
    return h.reshape(B, S, D)
```

```python
import functools

import jax
import jax.numpy as jnp
from jax import lax
from jax.experimental import pallas as pl
from jax.experimental.pallas import tpu as pltpu

F32 = jnp.float32
BF16 = jnp.bfloat16
HIGHEST = lax.Precision.HIGHEST

EPS = 1e-6
LOG2_E = 1.4426950408889634
MACARON_WEIGHT = 0.5
FOX_HEAD_DIM = 128
FOX_HEADS = 16
FOX_WIDTH = FOX_HEADS * FOX_HEAD_DIM
GLA_HEADS = 4
GLA_WIDTH = 2048
GLA_HEAD_V = GLA_WIDTH // GLA_HEADS
GLA_KEY_WIDTH = GLA_WIDTH // 2
GLA_HEAD_K = GLA_KEY_WIDTH // GLA_HEADS
GLA_GATE_RANK = 16
GLA_TAU = 16.0
GLA_CHUNK = 64

LANES = 128
MXU_COLS = 256
VMEM_LIMIT_BYTES = 56 * 1024 * 1024

MAIN_COLS = 3 * FOX_WIDTH + 2 * GLA_WIDTH + 2 * GLA_KEY_WIDTH
OFF_FQ = 0
OFF_FK = FOX_WIDTH
OFF_FV = 2 * FOX_WIDTH
OFF_GQ = 3 * FOX_WIDTH
OFF_GK = OFF_GQ + GLA_KEY_WIDTH
OFF_GV = OFF_GK + GLA_KEY_WIDTH
OFF_GR = OFF_GV + GLA_WIDTH

ROW_CHUNK = 32
ROW_BLOCK = 128


def _params(*semantics):
    return pltpu.CompilerParams(dimension_semantics=semantics,
                                vmem_limit_bytes=VMEM_LIMIT_BYTES)


def _rms(x, g):
    ms = jnp.mean(x * x, axis=-1, keepdims=True)
    return x * lax.rsqrt(ms + EPS) * g


def _log_sigmoid(x):
    return jnp.minimum(x, 0.0) - jnp.log(1.0 + jnp.exp(-jnp.abs(x)))


def _silu(x):
    return x * jax.nn.sigmoid(x)


def _sub_tiles(n_cols):
    return [slice(c, c + MXU_COLS) for c in range(0, n_cols, MXU_COLS)]


def _rms_rows(src_ref, g_ref, emit):
    def body(b, carry):
        chunks = [pl.ds(pl.multiple_of(b * ROW_BLOCK + c * ROW_CHUNK, ROW_CHUNK), ROW_CHUNK)
                  for c in range(ROW_BLOCK // ROW_CHUNK)]
        scales = []
        for rows in chunks:
            x = src_ref[rows, :]
            scales.append(lax.rsqrt(jnp.mean(x * x, axis=-1, keepdims=True) + EPS))
        for rows, scale in zip(chunks, scales):
            emit(rows, src_ref[rows, :] * scale * g_ref[...])
        return carry
    lax.fori_loop(0, src_ref.shape[0] // ROW_BLOCK, body, 0)


def _ffn_prologue(x_ref, gpre_ref, xn_ref, o_ref):
    def emit(rows, y):
        xn_ref[rows, :] = y.astype(BF16)
        o_ref[rows, :] = jnp.zeros((ROW_CHUNK, o_ref.shape[1]), F32)
    _rms_rows(x_ref, gpre_ref, emit)


def _ffn_epilogue(x_ref, gpost_ref, o_ref):
    def emit(rows, y):
        o_ref[rows, :] = x_ref[rows, :] + MACARON_WEIGHT * y
    _rms_rows(o_ref, gpost_ref, emit)


FFN_COPY_STEPS = 4


def _ffn_kernel(x_ref, gpre_ref, wg_ref, wu_ref, wd_ref, gpost_ref, first_ref, o_ref, xn_ref):
    i = pl.program_id(0)
    f = pl.program_id(1)
    rows_c = first_ref.shape[0]

    @pl.when(jnp.logical_and(i == 0, f < FFN_COPY_STEPS))
    def _():
        o_ref[pl.ds(pl.multiple_of(f * rows_c, rows_c), rows_c), :] = first_ref[...]

    @pl.when(i > 0)
    def _():
        @pl.when(f == 0)
        def _():
            _ffn_prologue(x_ref, gpre_ref, xn_ref, o_ref)

        xn = xn_ref[...]
        g = jnp.dot(xn, wg_ref[...], preferred_element_type=F32)
        u = jnp.dot(xn, wu_ref[...], preferred_element_type=F32)
        hid = (_silu(g) * u).astype(BF16)
        o_ref[...] += jnp.dot(hid, wd_ref[...], preferred_element_type=F32)

        @pl.when(f == pl.num_programs(1) - 1)
        def _():
            _ffn_epilogue(x_ref, gpost_ref, o_ref)


FFN_RING = 3


def _ffn_first_kernel(x_ref, gpre_ref, wg_hbm, wu_hbm, wd_ref, gpost_ref,
                      o_ref, wg16_ref, wu16_ref, wd16_ref, xn_ref, g_ref, u_ref,
                      wg_ring, wu_ring, sem):
    f = pl.program_id(0)
    k = pl.program_id(1)
    nf = pl.num_programs(0)
    _, dk, bf = wg_ring.shape
    s = f * 2 + k

    def tile_copies(ff, kk, slot):
        rows = pl.ds(pl.multiple_of(kk * dk, dk), dk)
        cols = pl.ds(pl.multiple_of(ff * bf, bf), bf)
        return (pltpu.make_async_copy(wg_hbm.at[rows, cols], wg_ring.at[slot], sem.at[0, slot]),
                pltpu.make_async_copy(wu_hbm.at[rows, cols], wu_ring.at[slot], sem.at[1, slot]))

    def start(copies):
        for priority, cp in enumerate(copies):
            cp.start(priority=priority)

    @pl.when(s == 0)
    def _():
        start(tile_copies(0, 0, 0))
        start(tile_copies(0, 1, 1))
        _ffn_prologue(x_ref, gpre_ref, xn_ref, o_ref)

    @pl.when(f + 1 < nf)
    def _():
        start(tile_copies(f + 1, k, (s + 2) % FFN_RING))

    slot = s % FFN_RING
    for cp in tile_copies(f, k, slot):
        cp.wait()
    wg = wg_ring[slot].astype(BF16)
    wu = wu_ring[slot].astype(BF16)
    wg16_ref[...] = wg
    wu16_ref[...] = wu
    xk = xn_ref[:, pl.ds(pl.multiple_of(k * dk, dk), dk)]
    g = jnp.dot(xk, wg, preferred_element_type=F32)
    u = jnp.dot(xk, wu, preferred_element_type=F32)

    @pl.when(k == 0)
    def _():
        g_ref[...] = g
        u_ref[...] = u

    @pl.when(k == 1)
    def _():
        wd = wd_ref[...].astype(BF16)
        wd16_ref[...] = wd
        hid = (_silu(g_ref[...] + g) * (u_ref[...] + u)).astype(BF16)
        o_ref[...] += jnp.dot(hid, wd, preferred_element_type=F32)

    @pl.when(jnp.logical_and(f == pl.num_programs(0) - 1, k == 1))
    def _():
        _ffn_epilogue(x_ref, gpost_ref, o_ref)


def _ffn(x, g_pre, wg, wu, wd, g_post, *, bm, bf):
    S, D = x.shape
    d_ff = wg.shape[1]
    once = dict(pipeline_mode=pl.Buffered(1))
    h, wg16, wu16, wd16 = pl.pallas_call(
        _ffn_first_kernel,
        name="ffn_first",
        grid=(d_ff // bf, 2),
        in_specs=[
            pl.BlockSpec((bm, D), lambda f, k: (0, 0), **once),
            pl.BlockSpec((1, D), lambda f, k: (0, 0)),
            pl.BlockSpec(memory_space=pl.ANY),
            pl.BlockSpec(memory_space=pl.ANY),
            pl.BlockSpec((bf, D), lambda f, k: (f, 0)),
            pl.BlockSpec((1, D), lambda f, k: (0, 0)),
        ],
        out_specs=[
            pl.BlockSpec((bm, D), lambda f, k: (0, 0), **once),
            pl.BlockSpec((D // 2, bf), lambda f, k: (k, f)),
            pl.BlockSpec((D // 2, bf), lambda f, k: (k, f)),
            pl.BlockSpec((bf, D), lambda f, k: (f, 0)),
        ],
        out_shape=[
            jax.ShapeDtypeStruct((bm, D), F32),
            jax.ShapeDtypeStruct((D, d_ff), BF16),
            jax.ShapeDtypeStruct((D, d_ff), BF16),
            jax.ShapeDtypeStruct((d_ff, D), BF16),
        ],
        scratch_shapes=[pltpu.VMEM((bm, D), BF16), pltpu.VMEM((bm, bf), F32), pltpu.VMEM((bm, bf), F32),
                        pltpu.VMEM((FFN_RING, D // 2, bf), F32), pltpu.VMEM((FFN_RING, D // 2, bf), F32),
                        pltpu.SemaphoreType.DMA((2, FFN_RING))],
        compiler_params=_params("arbitrary", "arbitrary"),
    )(x, g_pre, wg, wu, wd, g_post)
    if S == bm:
        return h
    nf = d_ff // bf
    c = FFN_COPY_STEPS
    assert nf >= c

    def ff_tile(i, f):
        return jnp.where(i == 0, 0, f)

    return pl.pallas_call(
        _ffn_kernel,
        name="ffn",
        grid=(S // bm, nf),
        in_specs=[
            pl.BlockSpec((bm, D), lambda i, f: (i, 0)),
            pl.BlockSpec((1, D), lambda i, f: (0, 0)),
            pl.BlockSpec((D, bf), lambda i, f: (0, ff_tile(i, f))),
            pl.BlockSpec((D, bf), lambda i, f: (0, ff_tile(i, f))),
            pl.BlockSpec((bf, D), lambda i, f: (ff_tile(i, f), 0)),
            pl.BlockSpec((1, D), lambda i, f: (0, 0)),
            pl.BlockSpec((bm // c, D), lambda i, f: (jnp.where(i == 0, jnp.minimum(f, c - 1), c - 1), 0)),
        ],
        out_specs=pl.BlockSpec((bm, D), lambda i, f: (i, 0)),
        out_shape=jax.ShapeDtypeStruct((S, D), F32),
        scratch_shapes=[pltpu.VMEM((bm, D), BF16)],
        compiler_params=_params("parallel", "arbitrary"),
    )(x, g_pre, wg16, wu16, wd16, g_post, h)


NT_DIMS = (((1,), (1,)), ((), ()))


def _inproj_kernel(h_ref, g_ref, wt_ref, wst_ref, scale_ref, o_ref, os_ref, a_ref):
    n = pl.program_id(1)

    @pl.when(n == 0)
    def _prologue():
        def emit(rows, y):
            a_ref[rows, :] = y.astype(BF16)
        _rms_rows(h_ref, g_ref, emit)
        os_ref[...] = lax.dot_general(a_ref[...], wst_ref[...], NT_DIMS, preferred_element_type=F32)

    for cols in _sub_tiles(wt_ref.shape[0]):
        acc = lax.dot_general(a_ref[...], wt_ref[cols, :], NT_DIMS, preferred_element_type=F32)
        o_ref[:, cols] = (acc * scale_ref[:, cols]).astype(BF16)


def _inproj(h, g, w_t, w_small_t, col_scale, *, bm, bn, n1, rest_row):
    S, D = h.shape
    n_cols = col_scale.shape[1]

    def w_row(i, n):
        return pl.multiple_of(jnp.where(n < n1, n * bn, rest_row + (n - n1) * bn), FOX_HEADS)

    return pl.pallas_call(
        _inproj_kernel,
        name="inproj",
        grid=(S // bm, n_cols // bn),
        in_specs=[
            pl.BlockSpec((bm, D), lambda i, n: (i, 0), pipeline_mode=pl.Buffered(1)),
            pl.BlockSpec((1, D), lambda i, n: (0, 0)),
            pl.BlockSpec((pl.Element(bn), pl.Element(D)), lambda i, n: (w_row(i, n), 0)),
            pl.BlockSpec((LANES, D), lambda i, n: (0, 0)),
            pl.BlockSpec((1, bn), lambda i, n: (0, n)),
        ],
        out_specs=[
            pl.BlockSpec((bm, bn), lambda i, n: (i, n)),
            pl.BlockSpec((bm, LANES), lambda i, n: (i, 0)),
        ],
        out_shape=[
            jax.ShapeDtypeStruct((S, n_cols), BF16),
            jax.ShapeDtypeStruct((S, LANES), F32),
        ],
        scratch_shapes=[pltpu.VMEM((bm, D), BF16)],
        compiler_params=_params("parallel", "arbitrary"),
    )(h, g, w_t, w_small_t, col_scale)


def _fox_gate_kernel(small_ref, bias_ref, c_ref, carry_ref):
    t = pl.program_id(0)
    T = small_ref.shape[0]

    @pl.when(t == 0)
    def _init():
        carry_ref[...] = jnp.zeros_like(carry_ref)

    log_f = _log_sigmoid(small_ref[...] + bias_ref[...])
    row = lax.broadcasted_iota(jnp.int32, (T, T), 0)
    col = lax.broadcasted_iota(jnp.int32, (T, T), 1)
    tri = jnp.where(row >= col, 1.0, 0.0).astype(F32)
    c = jnp.dot(tri, log_f, precision=HIGHEST, preferred_element_type=F32) + carry_ref[...]
    c_ref[...] = c * LOG2_E
    carry_ref[...] = c[T - 1:T, :]


def _fox_gate(small, bias, *, bt):
    S = small.shape[0]
    return pl.pallas_call(
        _fox_gate_kernel,
        name="fox_gate",
        grid=(S // bt,),
        in_specs=[
            pl.BlockSpec((bt, LANES), lambda t: (t, 0)),
            pl.BlockSpec((1, LANES), lambda t: (0, 0)),
        ],
        out_specs=pl.BlockSpec((bt, LANES), lambda t: (t, 0)),
        out_shape=jax.ShapeDtypeStruct((S, LANES), F32),
        scratch_shapes=[pltpu.VMEM((1, LANES), F32)],
        compiler_params=_params("arbitrary"),
    )(small, bias)


FOX_PAIR = 2
FOX_ROWS = 16


def _split3(c):
    hi = c.astype(BF16).astype(F32)
    mid = (c - hi).astype(BF16).astype(F32)
    lo = (c - hi - mid).astype(BF16).astype(F32)
    return hi, mid, lo


def _lane_tile(x, n):
    return jnp.concatenate([x] * n, axis=1)


def _head_column(block, head):
    lane = lax.broadcasted_iota(jnp.int32, block.shape, 1)
    return jnp.sum(jnp.where(lane == head, block, 0.0), axis=1, keepdims=True)


def _fox_kernel(q_ref, k_ref, v_ref, c_ref, o_ref, kaug_ref, vaug_ref, qaug_ref, s_ref, p_ref,
                m_ref, alpha_ref, acc_ref, *, bk):
    hp = pl.program_id(0)
    i = pl.program_id(1)
    bq = q_ref.shape[0]
    S = k_ref.shape[0]
    dh = FOX_HEAD_DIM
    R = FOX_ROWS

    @pl.when(i == 0)
    def _build_kv_side():
        rows_per = 128

        def body(r, carry):
            rows = pl.ds(pl.multiple_of(r * rows_per, rows_per), rows_per)
            cblk = c_ref[rows, :]
            lane = lax.broadcasted_iota(jnp.int32, (rows_per, LANES), 1)
            for hh in range(FOX_PAIR):
                hi, mid, lo = _split3(_head_column(cblk, hp * FOX_PAIR + hh))
                extra = jnp.where(lane < 3, 1.0,
                                  jnp.where(lane == 3, -hi,
                                            jnp.where(lane == 4, -mid,
                                                      jnp.where(lane == 5, -lo, 0.0))))
                kaug_ref[hh, rows, 0:dh] = k_ref[rows, hh * dh:(hh + 1) * dh]
                kaug_ref[hh, rows, dh:2 * dh] = extra.astype(BF16)
                vaug_ref[hh, rows, 0:dh] = v_ref[rows, hh * dh:(hh + 1) * dh]
                vaug_ref[hh, rows, dh:2 * dh] = jnp.ones((rows_per, dh), BF16)
            return carry
        lax.fori_loop(0, S // rows_per, body, 0, unroll=4)

    cq_blk = c_ref[pl.ds(pl.multiple_of(i * bq, bq), bq), :]
    lane_q = lax.broadcasted_iota(jnp.int32, (bq, LANES), 1)
    for hh in range(FOX_PAIR):
        hi, mid, lo = _split3(_head_column(cq_blk, hp * FOX_PAIR + hh))
        extra = jnp.where(lane_q == 0, hi,
                          jnp.where(lane_q == 1, mid,
                                    jnp.where(lane_q == 2, lo,
                                              jnp.where(lane_q < 6, 1.0, 0.0))))
        qaug_ref[hh, :, 0:dh] = q_ref[:, hh * dh:(hh + 1) * dh]
        qaug_ref[hh, :, dh:2 * dh] = extra.astype(BF16)
        m_ref[hh] = jnp.full((bq, LANES), -jnp.inf, F32)
        acc_ref[hh] = jnp.zeros((bq, 2 * dh), F32)

    groups = [slice(g * R, (g + 1) * R) for g in range(bq // R)]

    def chunk(k0, width, masked):
        ks = pl.ds(pl.multiple_of(k0, bq), width)
        for hh in range(FOX_PAIR):
            s_ref[hh, :, 0:width] = lax.dot_general(
                qaug_ref[hh], kaug_ref[hh, ks, :], (((1,), (1,)), ((), ())),
                preferred_element_type=F32)
        if masked:
            diff = (lax.broadcasted_iota(jnp.int32, (R, width), 1)
                    - lax.broadcasted_iota(jnp.int32, (R, width), 0))
            limit = i * bq - k0
        for hh in range(FOX_PAIR):
            for g, rows in enumerate(groups):
                s = s_ref[hh, rows, 0:width]
                if masked:
                    s = jnp.where(diff <= limit + g * R, s, -jnp.inf)
                    s_ref[hh, rows, 0:width] = s
                m_old = m_ref[hh, rows, :]
                m_new = jnp.maximum(m_old, jnp.broadcast_to(
                    jnp.max(s, axis=1, keepdims=True), (R, LANES)))
                alpha_ref[hh, rows, :] = jnp.exp2(m_old - m_new)
                m_ref[hh, rows, :] = m_new
        for hh in range(FOX_PAIR):
            for rows in groups:
                m = _lane_tile(m_ref[hh, rows, :], width // LANES)
                p_ref[hh, rows, 0:width] = jnp.exp2(s_ref[hh, rows, 0:width] - m).astype(BF16)
        for hh in range(FOX_PAIR):
            alpha = _lane_tile(alpha_ref[hh], 2)
            acc_ref[hh] = alpha * acc_ref[hh] + jnp.dot(
                p_ref[hh, :, 0:width], vaug_ref[hh, ks, :], preferred_element_type=F32)

    n_full = (i * bq) // bk

    def pair_body(j, carry):
        chunk(2 * j * bk, bk, False)
        chunk((2 * j + 1) * bk, bk, False)
        return carry
    lax.fori_loop(0, n_full // 2, pair_body, 0)

    @pl.when(n_full % 2 == 1)
    def _odd():
        chunk((n_full - 1) * bk, bk, False)

    tail_start = n_full * bk
    tail_blocks = (i + 1) - n_full * (bk // bq)
    for t in range(1, bk // bq + 1):
        @pl.when(tail_blocks == t)
        def _tail(t=t):
            chunk(tail_start, t * bq, True)

    for hh in range(FOX_PAIR):
        acc = acc_ref[hh]
        o_ref[:, hh * dh:(hh + 1) * dh] = (acc[:, 0:dh] / acc[:, dh:2 * dh]).astype(o_ref.dtype)


def _fox(proj, c2, *, bq, bk):
    S = proj.shape[0]
    dh = FOX_HEAD_DIM
    w = FOX_PAIR * dh
    return pl.pallas_call(
        functools.partial(_fox_kernel, bk=bk),
        name="fox",
        grid=(FOX_HEADS // FOX_PAIR, S // bq),
        in_specs=[
            pl.BlockSpec((bq, w), lambda h, i: (i, OFF_FQ // w + h)),
            pl.BlockSpec((S, w), lambda h, i: (0, OFF_FK // w + h)),
            pl.BlockSpec((S, w), lambda h, i: (0, OFF_FV // w + h)),
            pl.BlockSpec((S, LANES), lambda h, i: (0, 0)),
        ],
        out_specs=pl.BlockSpec((bq, w), lambda h, i: (i, h)),
        out_shape=jax.ShapeDtypeStruct((S, FOX_WIDTH), BF16),
        scratch_shapes=[
            pltpu.VMEM((FOX_PAIR, S, 2 * dh), BF16),
            pltpu.VMEM((FOX_PAIR, S, 2 * dh), BF16),
            pltpu.VMEM((FOX_PAIR, bq, 2 * dh), BF16),
            pltpu.VMEM((FOX_PAIR, bq, bk), F32),
            pltpu.VMEM((FOX_PAIR, bq, bk), BF16),
            pltpu.VMEM((FOX_PAIR, bq, LANES), F32),
            pltpu.VMEM((FOX_PAIR, bq, LANES), F32),
            pltpu.VMEM((FOX_PAIR, bq, 2 * dh), F32),
        ],
        compiler_params=_params("parallel", "arbitrary"),
    )(proj, proj, proj, c2)


def _gla_kernel(q_ref, k_ref, v_ref, gr_ref, small_ref, wgate_ref, bgate_ref,
                gnorm_ref, tri_ref, o_ref, st_ref):
    t = pl.program_id(1)
    T = q_ref.shape[0]
    C = GLA_CHUNK

    @pl.when(t == 0)
    def _init():
        st_ref[...] = jnp.zeros_like(st_ref)

    dk = q_ref.shape[1]
    gate = jnp.dot(small_ref[...].astype(BF16), wgate_ref[...],
                   preferred_element_type=F32) + bgate_ref[...]
    log_a = _log_sigmoid(gate) * (1.0 / GLA_TAU)

    M = tri_ref.shape[0]
    parts = [slice(r, r + M) for r in range(0, T, M)]
    pieces = jnp.concatenate([x.astype(BF16) for x in _split3(log_a)], axis=1)
    tri16 = tri_ref[...].astype(BF16)
    sums = jnp.concatenate(
        [jnp.dot(tri16, pieces[rows, :], preferred_element_type=F32) for rows in parts], axis=0)
    b = sums[:, 0:dk] + sums[:, dk:2 * dk] + sums[:, 2 * dk:3 * dk]
    b_tot = jnp.concatenate(
        [jnp.broadcast_to(b[c * C + C - 1:c * C + C, :], (C, dk)) for c in range(T // C)], axis=0)
    causal = tri_ref[...] > 0.0

    q = q_ref[...].astype(F32)
    k = k_ref[...].astype(F32)
    v = v_ref[...]
    q_dec = (q * jnp.exp(b)).astype(BF16)
    k_inv = (k * jnp.exp(-b)).astype(BF16)
    k_end = (k * jnp.exp(b_tot - b)).astype(BF16)
    o_parts = []
    for rows in parts:
        a = lax.dot_general(q_dec[rows, :], k_inv[rows, :], (((1,), (1,)), ((), ())),
                            preferred_element_type=F32)
        a = jnp.where(causal, a, 0.0).astype(BF16)
        o_parts.append(jnp.dot(a, v[rows, :], preferred_element_type=F32))
    o_intra = jnp.concatenate(o_parts, axis=0)

    for ci in range(T // C):
        lo, hi = ci * C, (ci + 1) * C
        st = st_ref[...]
        o_c = o_intra[lo:hi, :] + lax.dot_general(
            q_dec[lo:hi, :], st.astype(BF16), (((1,), (1,)), ((), ())),
            preferred_element_type=F32)
        decay = jnp.exp(b_tot[lo:lo + 1, :])
        st_ref[...] = st * decay + lax.dot_general(
            v[lo:hi, :], k_end[lo:hi, :], (((0,), (0,)), ((), ())),
            preferred_element_type=F32)
        y = _rms(o_c, gnorm_ref[...])
        o_ref[lo:hi, :] = (y * _silu(gr_ref[lo:hi, :].astype(F32))).astype(o_ref.dtype)


def _chunk_causal_mask(n):
    r = jnp.arange(n)[:, None]
    c = jnp.arange(n)[None, :]
    return ((r // GLA_CHUNK == c // GLA_CHUNK) & (r >= c)).astype(F32)


def _gla(proj, small, wgate_pad, bgate, gnorm, *, bt):
    S = proj.shape[0]
    dk, dv = GLA_HEAD_K, GLA_HEAD_V
    mask_rows = min(bt, MXU_COLS)
    return pl.pallas_call(
        _gla_kernel,
        name="gla",
        grid=(GLA_HEADS, S // bt),
        in_specs=[
            pl.BlockSpec((bt, dk), lambda h, t: (t, OFF_GQ // dk + h)),
            pl.BlockSpec((bt, dk), lambda h, t: (t, OFF_GK // dk + h)),
            pl.BlockSpec((bt, dv), lambda h, t: (t, OFF_GV // dv + h)),
            pl.BlockSpec((bt, dv), lambda h, t: (t, OFF_GR // dv + h)),
            pl.BlockSpec((bt, LANES), lambda h, t: (t, 0)),
            pl.BlockSpec((LANES, dk), lambda h, t: (0, h)),
            pl.BlockSpec((1, dk), lambda h, t: (0, h)),
            pl.BlockSpec((1, dv), lambda h, t: (0, 0)),
            pl.BlockSpec((mask_rows, mask_rows), lambda h, t: (0, 0)),
        ],
        out_specs=pl.BlockSpec((bt, dv), lambda h, t: (t, h)),
        out_shape=jax.ShapeDtypeStruct((S, GLA_WIDTH), BF16),
        scratch_shapes=[pltpu.VMEM((dv, dk), F32)],
        compiler_params=_params("parallel", "arbitrary"),
    )(proj, proj, proj, proj, small, wgate_pad, bgate, gnorm, _chunk_causal_mask(mask_rows))


def _outproj_kernel(fox_ref, gla_ref, wt_ref, wb_ref, h_ref, g_ref, o_ref):
    n = pl.program_id(1)
    bn = wt_ref.shape[1]
    for sub in _sub_tiles(bn):
        cols = pl.ds(pl.multiple_of(n * bn + sub.start, MXU_COLS), MXU_COLS)
        o_ref[:, cols] = (jnp.dot(fox_ref[...], wt_ref[:, sub], preferred_element_type=F32)
                          + jnp.dot(gla_ref[...], wb_ref[:, sub], preferred_element_type=F32))

    @pl.when(n == pl.num_programs(1) - 1)
    def _epilogue():
        def emit(rows, y):
            o_ref[rows, :] = h_ref[rows, :] + y
        _rms_rows(o_ref, g_ref, emit)


def _outproj(o_fox, o_gla, w_o, h, g, *, bm, bn):
    S, D = h.shape
    half = o_fox.shape[1]
    return pl.pallas_call(
        _outproj_kernel,
        name="outproj",
        grid=(S // bm, D // bn),
        in_specs=[
            pl.BlockSpec((bm, half), lambda i, n: (i, 0)),
            pl.BlockSpec((bm, half), lambda i, n: (i, 0)),
            pl.BlockSpec((half, bn), lambda i, n: (0, n)),
            pl.BlockSpec((half, bn), lambda i, n: (1, n)),
            pl.BlockSpec((bm, D), lambda i, n: (i, 0)),
            pl.BlockSpec((1, D), lambda i, n: (0, 0)),
        ],
        out_specs=pl.BlockSpec((bm, D), lambda i, n: (i, 0)),
        out_shape=jax.ShapeDtypeStruct((S, D), F32),
        compiler_params=_params("parallel", "arbitrary"),
    )(o_fox, o_gla, w_o, w_o, h, g)


def _ple_kernel(h_ref, p_ref, wp_ref, g_ref, wg_ref, o_ref, hb_ref, e_ref):
    n = pl.program_id(1)
    bn = wg_ref.shape[1]

    @pl.when(n == 0)
    def _prologue():
        e_ref[...] = jnp.dot(p_ref[...].astype(BF16), wp_ref[...], preferred_element_type=F32)

        def emit(rows, y):
            hb_ref[rows, :] = h_ref[rows, :].astype(BF16)
            e_ref[rows, :] = y
        _rms_rows(e_ref, g_ref, emit)

    for sub in _sub_tiles(bn):
        cols = pl.ds(pl.multiple_of(n * bn + sub.start, MXU_COLS), MXU_COLS)
        gate = jax.nn.sigmoid(jnp.dot(hb_ref[...], wg_ref[:, sub], preferred_element_type=F32))
        o_ref[:, sub] = h_ref[:, cols] + e_ref[:, cols] * gate


def _ple(h, p, w_proj, g, w_gate, *, bm, bn):
    S, D = h.shape
    dp = p.shape[1]
    return pl.pallas_call(
        _ple_kernel,
        name="ple",
        grid=(S // bm, D // bn),
        in_specs=[
            pl.BlockSpec((bm, D), lambda i, n: (i, 0)),
            pl.BlockSpec((bm, dp), lambda i, n: (i, 0)),
            pl.BlockSpec((dp, D), lambda i, n: (0, 0)),
            pl.BlockSpec((1, D), lambda i, n: (0, 0)),
            pl.BlockSpec((D, bn), lambda i, n: (0, n)),
        ],
        out_specs=pl.BlockSpec((bm, bn), lambda i, n: (i, n)),
        out_shape=jax.ShapeDtypeStruct((S, D), F32),
        scratch_shapes=[pltpu.VMEM((bm, D), BF16), pltpu.VMEM((bm, D), F32)],
        compiler_params=_params("parallel", "arbitrary"),
    )(h, p, w_proj, g, w_gate)


def _tile(n, preferred):
    t = min(n, preferred)
    while n % t:
        t //= 2
    return t


def _split_w_in(w_in):
    D = w_in.shape[0]
    ff_lo = 3 * FOX_WIDTH
    ff_hi = ff_lo + FOX_HEADS
    glr_lo = ff_hi + 2 * GLA_KEY_WIDTH + 2 * GLA_WIDTH
    glr_hi = glr_lo + GLA_GATE_RANK
    w_t = w_in.T.astype(BF16)
    pad = jnp.zeros((LANES - FOX_HEADS - GLA_GATE_RANK, D), BF16)
    small_t = jnp.concatenate([w_t[ff_lo:ff_hi], w_t[glr_lo:glr_hi], pad], axis=0)
    return w_t, small_t, ff_hi


def kernel(x, p, ffn1_norm_pre, ffn1_w_gate, ffn1_w_up, ffn1_w_down, ffn1_norm_post, mix_norm_pre, w_in, fox_b_f, gla_w_gate, gla_b_gate, gla_norm_g, w_o, mix_norm_post, ffn2_norm_pre, ffn2_w_gate, ffn2_w_up, ffn2_w_down, ffn2_norm_post, ple_w_proj, ple_norm, ple_w_gate):
    B, S, D = x.shape
    assert B == 1, "the attention kernels treat the row axis as one sequence"
    depth = w_in.shape[0]
    h = x.reshape(S, D)

    bm = _tile(S, 512)
    bf = _tile(ffn1_w_gate.shape[-1], 256)
    bq = _tile(S, 512)
    bk = max(bq, _tile(S, 1024))
    bt = _tile(S, 512)

    col_scale = jnp.ones((1, MAIN_COLS), F32)
    col_scale = col_scale.at[:, OFF_FQ:OFF_FQ + FOX_WIDTH].set(FOX_HEAD_DIM ** -0.5 * LOG2_E)
    col_scale = col_scale.at[:, OFF_GQ:OFF_GQ + GLA_KEY_WIDTH].set(GLA_HEAD_K ** -0.5)

    def row(v):
        return v.reshape(1, -1).astype(F32)

    for i in range(depth):
        h = _ffn(h, row(ffn1_norm_pre[i]), ffn1_w_gate[i], ffn1_w_up[i], ffn1_w_down[i],
                 row(ffn1_norm_post[i]), bm=bm, bf=bf)

        w_t, w_small_t, rest_row = _split_w_in(w_in[i])
        bn = 1024
        proj, small = _inproj(h, row(mix_norm_pre[i]), w_t, w_small_t, col_scale, bm=_tile(S, 1024),
                              bn=bn, n1=3 * FOX_WIDTH // bn, rest_row=rest_row)

        bias_f = jnp.zeros((1, LANES), F32).at[0, :FOX_HEADS].set(fox_b_f[i])
        c2 = _fox_gate(small, bias_f, bt=bt)
        o_fox = _fox(proj, c2, bq=bq, bk=bk)

        wgate_pad = jnp.zeros((LANES, GLA_KEY_WIDTH), F32).at[
            FOX_HEADS:FOX_HEADS + GLA_GATE_RANK, :].set(gla_w_gate[i]).astype(BF16)
        o_gla = _gla(proj, small, wgate_pad, row(gla_b_gate[i]), row(gla_norm_g[i]), bt=bt)

        h = _outproj(o_fox, o_gla, w_o[i].astype(BF16), h, row(mix_norm_post[i]), bm=bm, bn=512)

        h = _ffn(h, row(ffn2_norm_pre[i]), ffn2_w_gate[i], ffn2_w_up[i], ffn2_w_down[i],
                 row(ffn2_norm_post[i]), bm=bm, bf=bf)

        h = _ple(h, p[i].reshape(S, -1), ple_w_proj[i].astype(BF16), row(ple_norm[i]),
                 ple_w_gate[i].astype(BF16), bm=bm, bn=1024)
    return h.reshape(B, S, D)
```
